```python
import jax, jax.numpy as jnp
from jax import lax
import numpy as np

D_MODEL = 2048
BATCH = 8
SEQ = 2048
DEPTH = 1
DEC_BATCH = 128
DEC_SEQ = 1
PAST_LEN = 8192
PAGE_SIZE = 128

M_HEADS = 4
M_DK = 256
M_DV = 256
M_CHUNK = 64
A_HEADS = 16
A_KV_HEADS = 4
A_HEAD_DIM = 64
A_GROUP = A_HEADS // A_KV_HEADS
WINDOW = 128
A_BLOCK = WINDOW
M_QK_W = M_HEADS * M_DK
M_V_W = M_HEADS * M_DV
A_Q_W = A_HEADS * A_HEAD_DIM
A_KV_W = A_KV_HEADS * A_HEAD_DIM
SPLIT_SIZES = (M_QK_W, M_QK_W, M_V_W, M_V_W, M_HEADS, M_HEADS, A_Q_W, A_KV_W, A_KV_W)
IN_COLS = 2 * M_QK_W + 2 * M_V_W + 2 * M_HEADS + A_Q_W + 2 * A_KV_W
MIX_WIDTH = M_V_W + A_Q_W
N_GROUPS = 4
EXPERTS_PER_GROUP = 8
N_EXPERTS = N_GROUPS * EXPERTS_PER_GROUP
TOP_K = 2
D_EXPERT = 512
MOE_BLOCK = 128
PLE_DIM = 256
EPS = 1e-6

kernel_name = 'hymba_mlstm_swa_hmoe_step'


def rmsnorm(x, g):
    xf = x.astype(jnp.float32)
    r = lax.rsqrt(jnp.mean(xf * xf, axis=-1, keepdims=True) + EPS)
    return (xf * r).astype(x.dtype) * g


def split_in(z):
    offs = np.cumsum(np.array(SPLIT_SIZES))[:-1].tolist()
    return jnp.split(z, offs, axis=-1)


def mlstm_prepare(mq, mk, mv, mi, mf, b_i, b_f):
    B, T = mq.shape[:2]
    f32 = jnp.float32
    q = mq.reshape(B, T, M_HEADS, M_DK).transpose(0, 2, 1, 3).astype(f32)
    k = mk.reshape(B, T, M_HEADS, M_DK).transpose(0, 2, 1, 3).astype(f32) * (M_DK ** -0.5)
    v = mv.reshape(B, T, M_HEADS, M_DV).transpose(0, 2, 1, 3).astype(f32)
    i_pre = (mi.astype(f32) + b_i.astype(f32)).transpose(0, 2, 1)
    logf = jax.nn.log_sigmoid(mf.astype(f32) + b_f.astype(f32)).transpose(0, 2, 1)
    return q, k, v, i_pre, logf


def mlstm_chunk(carry, inp):
    c0, n0, m0 = carry
    q, k, v, i_pre, logf = inp
    L = q.shape[2]
    F = jnp.cumsum(logf, axis=-1)
    a = i_pre - F
    m_t = F + jnp.maximum(m0[..., None], lax.cummax(a, axis=2))
    dec0 = jnp.exp(F + m0[..., None] - m_t)
    causal = jnp.tril(jnp.ones((L, L), dtype=bool))
    log_d = (F - m_t)[..., :, None] + a[..., None, :]
    d = jnp.exp(jnp.where(causal, log_d, -jnp.inf))
    w = jnp.einsum('bhtk,bhsk->bhts', q, k) * d
    num = jnp.einsum('bhts,bhsv->bhtv', w, v) + dec0[..., None] * jnp.einsum('bhvk,bhtk->bhtv', c0, q)
    den = jnp.sum(w, axis=-1) + dec0 * jnp.einsum('bhk,bhtk->bht', n0, q)
    h = num / jnp.maximum(jnp.abs(den), jnp.exp(-m_t))[..., None]
    m_end = m_t[..., -1]
    w_end = jnp.exp(F[..., -1:] + a - m_end[..., None])
    dec_end = jnp.exp(F[..., -1] + m0 - m_end)
    c1 = dec_end[..., None, None] * c0 + jnp.einsum('bhs,bhsv,bhsk->bhvk', w_end, v, k)
    n1 = dec_end[..., None] * n0 + jnp.einsum('bhs,bhsk->bhk', w_end, k)
    return (c1, n1, m_end), h


def mlstm_prompt(q, k, v, i_pre, logf):
    B, H, T, _ = q.shape
    nc = T // M_CHUNK

    def to_chunks(a):
        a = a.reshape((B, H, nc, M_CHUNK) + a.shape[3:])
        return jnp.moveaxis(a, 2, 0)

    f32 = jnp.float32
    carry0 = (jnp.zeros((B, H, M_DV, M_DK), f32), jnp.zeros((B, H, M_DK), f32), jnp.zeros((B, H), f32))
    carry, h = lax.scan(mlstm_chunk, carry0, tuple(to_chunks(a) for a in (q, k, v, i_pre, logf)))
    h = jnp.moveaxis(h, 0, 2).reshape(B, H, T, M_DV)
    return h, carry


def mlstm_out(h, mo, g_mh):
    B, H, T, _ = h.shape
    h = h.transpose(0, 2, 1, 3)
    h = h * lax.rsqrt(jnp.mean(h * h, axis=-1, keepdims=True) + EPS)
    h = h.reshape(B, T, M_V_W).astype(mo.dtype) * g_mh
    return h * jax.nn.sigmoid(mo)


def sink_attention(q, k, v, allowed, sinks):
    s = jnp.einsum('...thgd,...khd->...hgtk', q, k).astype(jnp.float32) * (A_HEAD_DIM ** -0.5)
    s = jnp.where(allowed, s, -jnp.inf)
    sink = sinks.astype(jnp.float32).reshape(A_KV_HEADS, A_GROUP, 1, 1)
    mx = jnp.maximum(jnp.max(s, axis=-1, keepdims=True), sink)
    e = jnp.exp(s - mx)
    p = e / (jnp.sum(e, axis=-1, keepdims=True) + jnp.exp(sink - mx))
    return jnp.einsum('...hgtk,...khd->...thgd', p.astype(v.dtype), v)


def swa_prompt(aq, ak, av, sinks):
    B, T, _ = aq.shape
    nb = T // A_BLOCK
    q = aq.reshape(B, nb, A_BLOCK, A_KV_HEADS, A_GROUP, A_HEAD_DIM)
    k = ak.reshape(B, nb, A_BLOCK, A_KV_HEADS, A_HEAD_DIM)
    v = av.reshape(B, nb, A_BLOCK, A_KV_HEADS, A_HEAD_DIM)
    pad = jnp.zeros_like(k[:, :1])
    k2 = jnp.concatenate([jnp.concatenate([pad, k[:, :-1]], axis=1), k], axis=2)
    v2 = jnp.concatenate([jnp.concatenate([pad, v[:, :-1]], axis=1), v], axis=2)
    blk = jnp.arange(nb)[:, None] * A_BLOCK
    qpos = blk + jnp.arange(A_BLOCK)[None, :]
    kpos = blk - A_BLOCK + jnp.arange(2 * A_BLOCK)[None, :]
    diff = qpos[:, :, None] - kpos[:, None, :]
    allowed = (diff >= 0) & (diff < WINDOW) & (kpos[:, None, :] >= 0)
    o = sink_attention(q, k2, v2, allowed[None, :, None, None], sinks)
    kf = ak.reshape(B, T, A_KV_HEADS, A_HEAD_DIM)
    vf = av.reshape(B, T, A_KV_HEADS, A_HEAD_DIM)
    return o.reshape(B, T, A_Q_W), kf[:, -WINDOW:], vf[:, -WINDOW:]


def swa_sample(aq, ak, av, buf_k, buf_v, sinks):
    B, T, _ = aq.shape
    q = aq.reshape(B, T, A_KV_HEADS, A_GROUP, A_HEAD_DIM)
    k_all = jnp.concatenate([buf_k.astype(ak.dtype), ak.reshape(B, T, A_KV_HEADS, A_HEAD_DIM)], axis=1)
    v_all = jnp.concatenate([buf_v.astype(av.dtype), av.reshape(B, T, A_KV_HEADS, A_HEAD_DIM)], axis=1)
    qpos = WINDOW + jnp.arange(T)
    kpos = jnp.arange(WINDOW + T)
    diff = qpos[:, None] - kpos[None, :]
    allowed = (diff >= 0) & (diff < WINDOW)
    o = sink_attention(q, k_all, v_all, allowed, sinks)
    return o.reshape(B, T, A_Q_W), k_all[:, -WINDOW:], v_all[:, -WINDOW:]


def token_mix(x, g_mix, w_in, b_igate, b_fgate, g_mh, attn_sinks, w_out, state):
    h = rmsnorm(x, g_mix)
    mq, mk, mv, mo, mi, mf, aq, ak, av = split_in(h @ w_in)
    q, k, v, ip, lf = mlstm_prepare(mq, mk, mv, mi, mf, b_igate, b_fgate)
    if state is None:
        h_m, (c, n, m) = mlstm_prompt(q, k, v, ip, lf)
        o_a, kw, vw = swa_prompt(aq, ak, av, attn_sinks)
    else:
        c0, n0, m0, kb, vb = state
        f32 = jnp.float32
        (c, n, m), h_m = mlstm_chunk((c0.astype(f32), n0.astype(f32), m0.astype(f32)), (q, k, v, ip, lf))
        o_a, kw, vw = swa_sample(aq, ak, av, kb, vb, attn_sinks)
    o_m = mlstm_out(h_m, mo, g_mh)
    y = x + jnp.concatenate([o_m, o_a], axis=-1) @ w_out
    dt = x.dtype
    return y, (c.astype(dt), n.astype(dt), m.astype(dt), kw, vw)


def hier_moe(h, w_router_g, b_router_g, w_router_e, b_router_e, w_gate, w_up, w_down):
    B, T, D = h.shape
    hf = h.reshape(B * T, D)
    n_tok = B * T
    f32 = jnp.float32
    hr = hf.astype(f32)
    g_logits = hr @ w_router_g.astype(f32) + b_router_g.astype(f32)
    g_probs = jax.nn.softmax(g_logits, axis=-1)
    g_idx = jnp.argmax(g_logits, axis=-1)
    g_w = jnp.take_along_axis(g_probs, g_idx[:, None], axis=-1)
    e_logits = (hr @ w_router_e.astype(f32) + b_router_e.astype(f32)).reshape(n_tok, N_GROUPS, EXPERTS_PER_GROUP)
    e_logits = jnp.take_along_axis(e_logits, g_idx[:, None, None], axis=1)[:, 0]
    top_l, top_i = lax.top_k(e_logits, TOP_K)
    weights = g_w * jax.nn.softmax(top_l, axis=-1)
    expert = g_idx[:, None] * EXPERTS_PER_GROUP + top_i
    n_assign = n_tok * TOP_K
    e_flat = expert.reshape(n_assign)
    tok_flat = jnp.arange(n_assign) // TOP_K
    order = jnp.argsort(e_flat)
    e_sorted = e_flat[order]
    tok_sorted = tok_flat[order]
    w_sorted = weights.reshape(n_assign)[order]
    counts = jnp.bincount(e_flat, length=N_EXPERTS)
    padded = ((counts + MOE_BLOCK - 1) // MOE_BLOCK) * MOE_BLOCK
    pad_end = jnp.cumsum(padded)
    pad_start = pad_end - padded
    start = jnp.cumsum(counts) - counts
    dest = pad_start[e_sorted] + (jnp.arange(n_assign) - start[e_sorted])
    n_rows = (-(-n_assign // MOE_BLOCK) + N_EXPERTS) * MOE_BLOCK
    n_blocks = n_rows // MOE_BLOCK
    x_pad = jnp.zeros((n_rows, D), h.dtype).at[dest].set(hf[tok_sorted])
    block_expert = jnp.minimum(jnp.searchsorted(pad_end, jnp.arange(n_blocks) * MOE_BLOCK, side='right'), N_EXPERTS - 1)

    def expert_block(args):
        xb, e = args
        return (jax.nn.silu(xb @ w_gate[e]) * (xb @ w_up[e])) @ w_down[e]

    y_pad = lax.map(expert_block, (x_pad.reshape(n_blocks, MOE_BLOCK, D), block_expert)).reshape(n_rows, D)
    y = y_pad[dest] * w_sorted[:, None].astype(h.dtype)
    out = jax.ops.segment_sum(y, tok_sorted, num_segments=n_tok)
    return out.reshape(B, T, D)


def channel_mix(x, p_i, g_ffn, w_router_g, b_router_g, w_router_e, b_router_e, w_gate, w_up, w_down,
                g_ple, w_ple_gate, w_ple_proj):
    x = x + hier_moe(rmsnorm(x, g_ffn), w_router_g, b_router_g, w_router_e, b_router_e, w_gate, w_up, w_down)
    gate = jax.nn.sigmoid(rmsnorm(x, g_ple) @ w_ple_gate)
    return x + gate * (p_i.astype(x.dtype) @ w_ple_proj)


def setup_inputs(seed: int = 0) -> dict:
    key = jax.random.key(seed)
    ks = jax.random.split(key, 32)
    f32 = jnp.float32

    def nrm(k, shape, scale):
        return jax.random.normal(k, shape, f32) * scale

    D = D_MODEL
    return {
        'x_prompt': nrm(ks[0], (BATCH, SEQ, D), 1.0),
        'x_sample': nrm(ks[1], (DEC_BATCH, DEC_SEQ, D), 1.0),
        'p_prompt': nrm(ks[2], (DEPTH, BATCH, SEQ, PLE_DIM), 1.0),
        'p_sample': nrm(ks[3], (DEPTH, DEC_BATCH, DEC_SEQ, PLE_DIM), 1.0),
        'state_mlstm_C': nrm(ks[4], (DEPTH, DEC_BATCH, M_HEADS, M_DV, M_DK), 0.1),
        'state_mlstm_n': nrm(ks[5], (DEPTH, DEC_BATCH, M_HEADS, M_DK), 0.5),
        'state_mlstm_m': nrm(ks[6], (DEPTH, DEC_BATCH, M_HEADS), 1.0),
        'state_swa_k': nrm(ks[7], (DEPTH, DEC_BATCH, WINDOW, A_KV_HEADS, A_HEAD_DIM), 1.0),
        'state_swa_v': nrm(ks[8], (DEPTH, DEC_BATCH, WINDOW, A_KV_HEADS, A_HEAD_DIM), 1.0),
        'g_mix': 1.0 + nrm(ks[9], (DEPTH, D), 0.05),
        'w_in': nrm(ks[10], (DEPTH, D, IN_COLS), D ** -0.5),
        'b_igate': nrm(ks[11], (DEPTH, M_HEADS), 0.1),
        'b_fgate': 3.0 + 3.0 * jax.random.uniform(ks[12], (DEPTH, M_HEADS), f32),
        'g_mh': 1.0 + nrm(ks[13], (DEPTH, M_V_W), 0.05),
        'attn_sinks': nrm(ks[14], (DEPTH, A_HEADS), 0.5),
        'w_out': nrm(ks[15], (DEPTH, MIX_WIDTH, D), MIX_WIDTH ** -0.5),
        'g_ffn': 1.0 + nrm(ks[16], (DEPTH, D), 0.05),
        'w_router_g': nrm(ks[17], (DEPTH, D, N_GROUPS), D ** -0.5),
        'b_router_g': nrm(ks[18], (DEPTH, N_GROUPS), 0.01),
        'w_router_e': nrm(ks[19], (DEPTH, D, N_EXPERTS), D ** -0.5),
        'b_router_e': nrm(ks[20], (DEPTH, N_EXPERTS), 0.01),
        'w_gate': nrm(ks[21], (DEPTH, N_EXPERTS, D, D_EXPERT), D ** -0.5),
        'w_up': nrm(ks[22], (DEPTH, N_EXPERTS, D, D_EXPERT), D ** -0.5),
        'w_down': nrm(ks[23], (DEPTH, N_EXPERTS, D_EXPERT, D), D_EXPERT ** -0.5),
        'g_ple': 1.0 + nrm(ks[24], (DEPTH, D), 0.05),
        'w_ple_gate': nrm(ks[25], (DEPTH, D, D), D ** -0.5),
        'w_ple_proj': nrm(ks[26], (DEPTH, PLE_DIM, D), PLE_DIM ** -0.5),
        'g_final': 1.0 + nrm(ks[27], (D,), 0.05),
    }


def reference(x_prompt, x_sample, p_prompt, p_sample, state_mlstm_C, state_mlstm_n, state_mlstm_m,
              state_swa_k, state_swa_v, g_mix, w_in, b_igate, b_fgate, g_mh, attn_sinks, w_out, g_ffn,
              w_router_g, b_router_g, w_router_e, b_router_e, w_gate, w_up, w_down, g_ple, w_ple_gate,
              w_ple_proj, g_final):
    xp, xs = x_prompt, x_sample
    sp = [[], [], [], [], []]
    ss = [[], [], [], [], []]
    for l in range(DEPTH):
        mix_w = (g_mix[l], w_in[l], b_igate[l], b_fgate[l], g_mh[l], attn_sinks[l], w_out[l])
        ch_w = (g_ffn[l], w_router_g[l], b_router_g[l], w_router_e[l], b_router_e[l], w_gate[l], w_up[l],
                w_down[l], g_ple[l], w_ple_gate[l], w_ple_proj[l])
        xp, st_p = token_mix(xp, *mix_w, None)
        xp = channel_mix(xp, p_prompt[l], *ch_w)
        st_in = (state_mlstm_C[l], state_mlstm_n[l], state_mlstm_m[l], state_swa_k[l], state_swa_v[l])
        xs, st_s = token_mix(xs, *mix_w, st_in)
        xs = channel_mix(xs, p_sample[l], *ch_w)
        for j in range(5):
            sp[j].append(st_p[j])
            ss[j].append(st_s[j])
    y_prompt = rmsnorm(xp, g_final)
    y_sample = rmsnorm(xs, g_final)
    return (y_prompt, y_sample,
            jnp.stack(sp[0]), jnp.stack(sp[1]), jnp.stack(sp[2]), jnp.stack(sp[3]), jnp.stack(sp[4]),
            jnp.stack(ss[0]), jnp.stack(ss[1]), jnp.stack(ss[2]), jnp.stack(ss[3]), jnp.stack(ss[4]))
```

```python
import functools

import jax
import jax.numpy as jnp
from jax import lax
from jax.experimental import pallas as pl
from jax.experimental.pallas import tpu as pltpu

F32 = jnp.float32
BF16 = jnp.bfloat16
HI = lax.Precision.HIGHEST

D_MODEL = 2048
M_HEADS = 4
M_DK = 256
M_DV = 256
A_HEADS = 16
A_KV_HEADS = 4
A_HEAD_DIM = 64
A_GROUP = A_HEADS // A_KV_HEADS
WINDOW = 128
M_W = M_HEADS * M_DK
A_Q_W = A_HEADS * A_HEAD_DIM
A_KV_W = A_KV_HEADS * A_HEAD_DIM
GATE_OFF = 4 * M_W
ATT_OFF = GATE_OFF + 2 * M_HEADS
Z_W = 4 * M_W + A_Q_W + 2 * A_KV_W
N_GROUPS = 4
EXPERTS_PER_GROUP = 8
N_EXPERTS = N_GROUPS * EXPERTS_PER_GROUP
D_EXPERT = 512
PLE_DIM = 256
EPS = 1e-6

LANES = 128
MLSTM_CHUNK = 256
EXPERT_ROWS = 256
VMEM_LIMIT = 56 * 1024 * 1024

NT_DIMS = (((1,), (1,)), ((), ()))
TN_DIMS = (((0,), (0,)), ((), ()))


def _params(*sem):
    return pltpu.CompilerParams(dimension_semantics=sem, vmem_limit_bytes=VMEM_LIMIT)


def _rms(x, g):
    r = lax.rsqrt(jnp.mean(x * x, axis=-1, keepdims=True) + EPS)
    return (x * r) * g


def _log_sigmoid(x):
    return jnp.minimum(x, 0.0) - jnp.log1p(jnp.exp(-jnp.abs(x)))


def _inproj_kernel(x_ref, g_ref, w_ref, wgi_ref, wgf_ref, wgt_ref, brow_ref, bcol_ref,
                   z_ref, gi_ref, gf_ref, gr_ref, hn_scr):
    @pl.when(pl.program_id(1) == 0)
    def _():
        hn = _rms(x_ref[...], g_ref[...])
        hn_scr[...] = hn.astype(BF16)
        gi = jnp.dot(hn, wgi_ref[...], precision=HI, preferred_element_type=F32) + brow_ref[0:1, :]
        gf = jnp.dot(hn, wgf_ref[...], precision=HI, preferred_element_type=F32) + brow_ref[1:2, :]
        gi_ref[...] = gi
        gf_ref[...] = _log_sigmoid(gf)
        gr = lax.dot_general(wgt_ref[...], hn, NT_DIMS, precision=HI, preferred_element_type=F32) + bcol_ref[...]
        is_f = lax.broadcasted_iota(jnp.int32, gr.shape, 0) >= M_HEADS
        gr_ref[...] = jnp.where(is_f, _log_sigmoid(gr), gr)

    z_ref[...] = jnp.dot(hn_scr[...], w_ref[...], preferred_element_type=F32)


def _inproj(x, g_mix, w_cat, wgi, wgf, wgt, brow, bcol, tm, tn):
    n = x.shape[0]
    const = lambda i, j: (0, 0)
    return pl.pallas_call(
        _inproj_kernel,
        grid=(n // tm, Z_W // tn),
        in_specs=[
            pl.BlockSpec((tm, D_MODEL), lambda i, j: (i, 0)),
            pl.BlockSpec((1, D_MODEL), const),
            pl.BlockSpec((D_MODEL, tn), lambda i, j: (0, j)),
            pl.BlockSpec((D_MODEL, LANES), const),
            pl.BlockSpec((D_MODEL, LANES), const),
            pl.BlockSpec((8, D_MODEL), const),
            pl.BlockSpec((2, LANES), const),
            pl.BlockSpec((8, 1), const),
        ],
        out_specs=[
            pl.BlockSpec((tm, tn), lambda i, j: (i, j)),
            pl.BlockSpec((tm, LANES), lambda i, j: (i, 0)),
            pl.BlockSpec((tm, LANES), lambda i, j: (i, 0)),
            pl.BlockSpec((8, tm), lambda i, j: (0, i)),
        ],
        out_shape=[
            jax.ShapeDtypeStruct((n, Z_W), F32),
            jax.ShapeDtypeStruct((n, LANES), F32),
            jax.ShapeDtypeStruct((n, LANES), F32),
            jax.ShapeDtypeStruct((8, n), F32),
        ],
        scratch_shapes=[pltpu.VMEM((tm, D_MODEL), BF16)],
        compiler_params=_params("parallel", "arbitrary"),
        name="inproj",
    )(x, g_mix, w_cat, wgi, wgf, wgt, brow, bcol)


def _mlstm_out(h, gmh, mo):
    hn = h * lax.rsqrt(jnp.mean(h * h, axis=-1, keepdims=True) + EPS)
    return (hn * gmh) * jax.nn.sigmoid(mo)


def _mlstm_kernel(q_ref, k_ref, v_ref, mo_ref, gi_ref, gf_ref, gr_ref, gmh_ref,
                  o_ref, c_out, n_out, m_out, c_scr, n_scr, m_scr, *, chunk, n_chunks):
    c_idx = pl.program_id(1)

    @pl.when(c_idx == 0)
    def _():
        c_scr[...] = jnp.zeros_like(c_scr)
        n_scr[...] = jnp.zeros_like(n_scr)
        m_scr[...] = jnp.zeros_like(m_scr)

    L = chunk
    row = lax.broadcasted_iota(jnp.int32, (L, L), 0)
    col = lax.broadcasted_iota(jnp.int32, (L, L), 1)
    causal = row >= col
    ltri = causal.astype(F32)
    utri = (row <= col).astype(F32)
    ipre_c = gi_ref[...]
    f_c = jnp.dot(ltri, gf_ref[...], precision=HI, preferred_element_type=F32)
    a_c = ipre_c - f_c
    gr = gr_ref[...]
    f_r = jnp.dot(gr, utri, precision=HI, preferred_element_type=F32)

    for h in range(M_HEADS):
        hs = slice(h * M_DK, (h + 1) * M_DK)
        a_row = gr[h:h + 1, :] - f_r[M_HEADS + h:M_HEADS + h + 1, :]
        a_col = a_c[:, h:h + 1]
        f_col = f_c[:, h:h + 1]
        m0 = m_scr[h:h + 1, 0:1]
        a_max = jnp.max(jnp.where(causal, a_row, -jnp.inf), axis=1, keepdims=True)
        m_t = f_col + jnp.maximum(m0, a_max)
        dec0 = jnp.exp(f_col + m0 - m_t)
        d = jnp.exp(jnp.where(causal, (f_col - m_t) + a_row, -jnp.inf))
        q = q_ref[:, hs]
        k = k_ref[:, hs] * (M_DK ** -0.5)
        qb = q.astype(BF16)
        kb = k.astype(BF16)
        vb = v_ref[:, hs].astype(BF16)
        w = lax.dot_general(qb, kb, NT_DIMS, preferred_element_type=F32) * d
        c0 = c_scr[h]
        n0 = n_scr[h:h + 1, :]
        num = jnp.dot(w.astype(BF16), vb, preferred_element_type=F32) + dec0 * lax.dot_general(
            qb, c0.astype(BF16), NT_DIMS, preferred_element_type=F32)
        den = jnp.sum(w, axis=1, keepdims=True) + dec0 * jnp.sum(q * n0, axis=1, keepdims=True)
        hh = num / jnp.maximum(jnp.abs(den), jnp.exp(-m_t))
        m_end = m_t[L - 1:L, :]
        f_end = f_col[L - 1:L, :]
        w_end = jnp.exp(f_end + a_col - m_end)
        dec_end = jnp.exp(f_end + m0 - m_end)
        kw = k * w_end
        c_scr[h] = dec_end * c0 + lax.dot_general(vb, kw.astype(BF16), TN_DIMS, preferred_element_type=F32)
        n_scr[h:h + 1, :] = dec_end * n0 + jnp.sum(kw, axis=0, keepdims=True)
        m_scr[h:h + 1, :] = jnp.broadcast_to(m_end, (1, LANES))
        o_ref[:, hs] = _mlstm_out(hh, gmh_ref[:, hs], mo_ref[:, hs]).astype(o_ref.dtype)

    @pl.when(c_idx == n_chunks - 1)
    def _():
        c_out[0] = c_scr[...]
        n_out[0] = n_scr[...]
        m_out[0] = m_scr[...]


def _mlstm_prompt(z, gi, gf, gr, g_mh, batch, seq):
    L = MLSTM_CHUNK
    nc = seq // L
    rows = lambda b, c: b * nc + c
    return pl.pallas_call(
        functools.partial(_mlstm_kernel, chunk=L, n_chunks=nc),
        grid=(batch, nc),
        in_specs=[
            pl.BlockSpec((L, M_W), lambda b, c: (rows(b, c), 0)),
            pl.BlockSpec((L, M_W), lambda b, c: (rows(b, c), 1)),
            pl.BlockSpec((L, M_W), lambda b, c: (rows(b, c), 2)),
            pl.BlockSpec((L, M_W), lambda b, c: (rows(b, c), 3)),
            pl.BlockSpec((L, LANES), lambda b, c: (rows(b, c), 0)),
            pl.BlockSpec((L, LANES), lambda b, c: (rows(b, c), 0)),
            pl.BlockSpec((8, L), lambda b, c: (0, rows(b, c))),
            pl.BlockSpec((1, M_W), lambda b, c: (0, 0)),
        ],
        out_specs=[
            pl.BlockSpec((L, M_W), lambda b, c: (rows(b, c), 0)),
            pl.BlockSpec((1, M_HEADS, M_DV, M_DK), lambda b, c: (b, 0, 0, 0)),
            pl.BlockSpec((1, 8, M_DK), lambda b, c: (b, 0, 0)),
            pl.BlockSpec((1, 8, LANES), lambda b, c: (b, 0, 0)),
        ],
        out_shape=[
            jax.ShapeDtypeStruct((batch * seq, M_W), BF16),
            jax.ShapeDtypeStruct((batch, M_HEADS, M_DV, M_DK), F32),
            jax.ShapeDtypeStruct((batch, 8, M_DK), F32),
            jax.ShapeDtypeStruct((batch, 8, LANES), F32),
        ],
        scratch_shapes=[
            pltpu.VMEM((M_HEADS, M_DV, M_DK), F32),
            pltpu.VMEM((8, M_DK), F32),
            pltpu.VMEM((8, LANES), F32),
        ],
        compiler_params=_params("parallel", "arbitrary"),
        name="mlstm_prompt",
    )(z, z, z, z, gi, gf, gr, g_mh)


def _mlstm_step_kernel(z_ref, gi_ref, gf_ref, c_ref, n_ref, m_ref, gmh_ref, o_ref, c_out, n_out, m_out):
    z = z_ref[0]
    gi = gi_ref[0]
    gf = gf_ref[0]
    m_in = m_ref[0]
    row = lax.broadcasted_iota(jnp.int32, (M_DV, M_DV), 0)
    col = lax.broadcasted_iota(jnp.int32, (M_DV, M_DV), 1)
    eye = row == col
    lane = lax.broadcasted_iota(jnp.int32, (1, LANES), 1)
    m_new = jnp.zeros((1, LANES), F32)
    for h in range(M_HEADS):
        hs = slice(h * M_DK, (h + 1) * M_DK)
        q = z[:, h * M_DK:(h + 1) * M_DK]
        k = z[:, M_W + h * M_DK:M_W + (h + 1) * M_DK] * (M_DK ** -0.5)
        v = z[:, 2 * M_W + h * M_DV:2 * M_W + (h + 1) * M_DV]
        mo = z[:, 3 * M_W + h * M_DV:3 * M_W + (h + 1) * M_DV]
        ip = gi[:, h:h + 1]
        lf = gf[:, h:h + 1]
        m0 = m_in[:, h:h + 1]
        m_t = jnp.maximum(lf + m0, ip)
        dec0 = jnp.exp(lf + m0 - m_t)
        d = jnp.exp(ip - m_t)
        c0 = c_ref[0, h]
        n0 = n_ref[0, :, hs]
        w = jnp.sum(q * k, axis=1, keepdims=True) * d
        cq = lax.dot_general(q.astype(BF16), c0.astype(BF16), NT_DIMS, preferred_element_type=F32)
        num = w * v + dec0 * cq
        den = w + dec0 * jnp.sum(n0 * q, axis=1, keepdims=True)
        hh = num / jnp.maximum(jnp.abs(den), jnp.exp(-m_t))
        v_col = jnp.sum(jnp.where(eye, v, 0.0), axis=1, keepdims=True)
        c_out[0, h] = dec0 * c0 + (d * v_col) * k
        n_out[0, :, hs] = dec0 * n0 + d * k
        m_new = jnp.where(lane == h, m_t, m_new)
        o_ref[0, :, hs] = _mlstm_out(hh, gmh_ref[:, hs], mo).astype(o_ref.dtype)
    m_out[0] = m_new


def _mlstm_sample(z3, gi3, gf3, c0, n0, m0, g_mh):
    nb = z3.shape[0]
    tok = lambda b: (b, 0, 0)
    return pl.pallas_call(
        _mlstm_step_kernel,
        grid=(nb,),
        in_specs=[
            pl.BlockSpec((1, 1, Z_W), tok),
            pl.BlockSpec((1, 1, LANES), tok),
            pl.BlockSpec((1, 1, LANES), tok),
            pl.BlockSpec((1, M_HEADS, M_DV, M_DK), lambda b: (b, 0, 0, 0)),
            pl.BlockSpec((1, 1, M_W), tok),
            pl.BlockSpec((1, 1, M_HEADS), tok),
            pl.BlockSpec((1, M_W), lambda b: (0, 0)),
        ],
        out_specs=[
            pl.BlockSpec((1, 1, M_W), tok),
            pl.BlockSpec((1, M_HEADS, M_DV, M_DK), lambda b: (b, 0, 0, 0)),
            pl.BlockSpec((1, 1, M_W), tok),
            pl.BlockSpec((1, 1, LANES), tok),
        ],
        out_shape=[
            jax.ShapeDtypeStruct((nb, 1, M_W), BF16),
            jax.ShapeDtypeStruct((nb, M_HEADS, M_DV, M_DK), F32),
            jax.ShapeDtypeStruct((nb, 1, M_W), F32),
            jax.ShapeDtypeStruct((nb, 1, LANES), F32),
        ],
        compiler_params=_params("parallel"),
        name="mlstm_sample",
    )(z3, gi3, gf3, c0, n0, m0, g_mh)


def _swa_kernel(sink_ref, q_ref, kc_ref, kp_ref, vc_ref, vp_ref, o_ref):
    j = pl.program_id(1)
    t = lax.broadcasted_iota(jnp.int32, (WINDOW, WINDOW), 0)
    kk = lax.broadcasted_iota(jnp.int32, (WINDOW, WINDOW), 1)
    mask_cur = kk <= t
    mask_prev = jnp.logical_and(kk > t, j > 0)
    scale = A_HEAD_DIM ** -0.5
    for g in range(A_KV_HEADS):
        gs = slice(g * A_HEAD_DIM, (g + 1) * A_HEAD_DIM)
        kc = kc_ref[:, gs].astype(BF16)
        kp = kp_ref[:, gs].astype(BF16)
        vc = vc_ref[:, gs].astype(BF16)
        vp = vp_ref[:, gs].astype(BF16)
        for i in range(A_GROUP):
            hd = g * A_GROUP + i
            hsl = slice(hd * A_HEAD_DIM, (hd + 1) * A_HEAD_DIM)
            q = q_ref[:, hsl].astype(BF16)
            sc = lax.dot_general(q, kc, NT_DIMS, preferred_element_type=F32) * scale
            sp = lax.dot_general(q, kp, NT_DIMS, preferred_element_type=F32) * scale
            sc = jnp.where(mask_cur, sc, -jnp.inf)
            sp = jnp.where(mask_prev, sp, -jnp.inf)
            sink = sink_ref[0, hd]
            mx = jnp.maximum(jnp.maximum(jnp.max(sc, axis=1, keepdims=True),
                                         jnp.max(sp, axis=1, keepdims=True)), sink)
            ec = jnp.exp(sc - mx)
            ep = jnp.exp(sp - mx)
            den = jnp.sum(ec, axis=1, keepdims=True) + jnp.sum(ep, axis=1, keepdims=True) + jnp.exp(sink - mx)
            o = (jnp.dot((ec / den).astype(BF16), vc, preferred_element_type=F32)
                 + jnp.dot((ep / den).astype(BF16), vp, preferred_element_type=F32))
            o_ref[:, hsl] = o.astype(o_ref.dtype)


def _swa_prompt(z, sinks, batch, seq):
    nb = seq // WINDOW
    q_blk = (4 * M_W) // A_Q_W
    k_blk = (4 * M_W + A_Q_W) // A_KV_W
    cur = lambda b, j: b * nb + j
    prev = lambda b, j: b * nb + jnp.maximum(j - 1, 0)
    return pl.pallas_call(
        _swa_kernel,
        grid=(batch, nb),
        in_specs=[
            pl.BlockSpec(memory_space=pltpu.SMEM),
            pl.BlockSpec((WINDOW, A_Q_W), lambda b, j: (cur(b, j), q_blk)),
            pl.BlockSpec((WINDOW, A_KV_W), lambda b, j: (cur(b, j), k_blk)),
            pl.BlockSpec((WINDOW, A_KV_W), lambda b, j: (prev(b, j), k_blk)),
            pl.BlockSpec((WINDOW, A_KV_W), lambda b, j: (cur(b, j), k_blk + 1)),
            pl.BlockSpec((WINDOW, A_KV_W), lambda b, j: (prev(b, j), k_blk + 1)),
        ],
        out_specs=pl.BlockSpec((WINDOW, A_Q_W), lambda b, j: (cur(b, j), 0)),
        out_shape=jax.ShapeDtypeStruct((batch * seq, A_Q_W), BF16),
        compiler_params=_params("parallel", "arbitrary"),
        name="swa_prompt",
    )(sinks, z, z, z, z, z)


def _swa_step_kernel(q_ref, kn_ref, vn_ref, bk_ref, bv_ref, sink_ref, o_ref, ko_ref, vo_ref):
    bk = bk_ref[0]
    bv = bv_ref[0]
    kn = kn_ref[0]
    vn = vn_ref[0]
    in_window = lax.broadcasted_iota(jnp.int32, (1, WINDOW), 1) >= 1
    scale = A_HEAD_DIM ** -0.5
    for g in range(A_KV_HEADS):
        gs = slice(g * A_HEAD_DIM, (g + 1) * A_HEAD_DIM)
        rs = slice(g * A_GROUP, (g + 1) * A_GROUP)
        q = q_ref[0, rs, :]
        s = lax.dot_general(q.astype(BF16), bk[:, gs].astype(BF16), NT_DIMS, preferred_element_type=F32) * scale
        s = jnp.where(in_window, s, -jnp.inf)
        s_new = jnp.sum(q * kn[:, gs], axis=1, keepdims=True) * scale
        sink = sink_ref[rs, :]
        mx = jnp.maximum(jnp.maximum(jnp.max(s, axis=1, keepdims=True), s_new), sink)
        e = jnp.exp(s - mx)
        e_new = jnp.exp(s_new - mx)
        den = jnp.sum(e, axis=1, keepdims=True) + e_new + jnp.exp(sink - mx)
        o = jnp.dot((e / den).astype(BF16), bv[:, gs].astype(BF16), preferred_element_type=F32)
        o_ref[0, rs, :] = o + (e_new / den) * vn[:, gs]
    ko_ref[0, 0:WINDOW - 1, :] = bk[1:WINDOW, :]
    ko_ref[0, WINDOW - 1:WINDOW, :] = kn
    vo_ref[0, 0:WINDOW - 1, :] = bv[1:WINDOW, :]
    vo_ref[0, WINDOW - 1:WINDOW, :] = vn


def _swa_sample(q3, kn3, vn3, buf_k, buf_v, sink_col):
    nb = q3.shape[0]
    tok = lambda b: (b, 0, 0)
    return pl.pallas_call(
        _swa_step_kernel,
        grid=(nb,),
        in_specs=[
            pl.BlockSpec((1, A_HEADS, A_HEAD_DIM), tok),
            pl.BlockSpec((1, 1, A_KV_W), tok),
            pl.BlockSpec((1, 1, A_KV_W), tok),
            pl.BlockSpec((1, WINDOW, A_KV_W), tok),
            pl.BlockSpec((1, WINDOW, A_KV_W), tok),
            pl.BlockSpec((A_HEADS, 1), lambda b: (0, 0)),
        ],
        out_specs=[
            pl.BlockSpec((1, A_HEADS, A_HEAD_DIM), tok),
            pl.BlockSpec((1, WINDOW, A_KV_W), tok),
            pl.BlockSpec((1, WINDOW, A_KV_W), tok),
        ],
        out_shape=[
            jax.ShapeDtypeStruct((nb, A_HEADS, A_HEAD_DIM), F32),
            jax.ShapeDtypeStruct((nb, WINDOW, A_KV_W), F32),
            jax.ShapeDtypeStruct((nb, WINDOW, A_KV_W), F32),
        ],
        compiler_params=_params("parallel"),
        name="swa_sample",
    )(q3, kn3, vn3, buf_k, buf_v, sink_col)


def _outproj_kernel(x_ref, om_ref, oa_ref, wm_ref, wa_ref, y_ref):
    y_ref[...] = (x_ref[...]
                  + jnp.dot(om_ref[...], wm_ref[...], preferred_element_type=F32)
                  + jnp.dot(oa_ref[...], wa_ref[...], preferred_element_type=F32))


def _outproj(x, o_m, o_a, w_m, w_a, tm):
    n = x.shape[0]
    return pl.pallas_call(
        _outproj_kernel,
        grid=(n // tm,),
        in_specs=[
            pl.BlockSpec((tm, D_MODEL), lambda i: (i, 0)),
            pl.BlockSpec((tm, M_W), lambda i: (i, 0)),
            pl.BlockSpec((tm, A_Q_W), lambda i: (i, 0)),
            pl.BlockSpec((M_W, D_MODEL), lambda i: (0, 0)),
            pl.BlockSpec((A_Q_W, D_MODEL), lambda i: (0, 0)),
        ],
        out_specs=pl.BlockSpec((tm, D_MODEL), lambda i: (i, 0)),
        out_shape=jax.ShapeDtypeStruct((n, D_MODEL), F32),
        compiler_params=_params("parallel"),
        name="outproj",
    )(x, o_m, o_a, w_m, w_a)


GROUP_LANE0 = 0
EXPERT_LANE0 = N_GROUPS


def _route_kernel(x_ref, g_ref, wr_ref, br_ref, cin_ref, info_ref, cnt_ref, carry):
    @pl.when(pl.program_id(0) == 0)
    def _():
        carry[...] = cin_ref[...]

    tm = x_ref.shape[0]
    hn = _rms(x_ref[...], g_ref[...])
    logits = jnp.dot(hn, wr_ref[...], precision=HI, preferred_element_type=F32) + br_ref[...]
    lane = lax.broadcasted_iota(jnp.int32, (tm, LANES), 1)
    neg = -jnp.inf
    gl = jnp.where(lane < N_GROUPS, logits, neg)
    g_max = jnp.max(gl, axis=1, keepdims=True)
    g_idx = jnp.min(jnp.where(gl == g_max, lane, LANES), axis=1, keepdims=True)
    g_w = 1.0 / jnp.sum(jnp.exp(gl - g_max), axis=1, keepdims=True)
    lo = EXPERT_LANE0 + EXPERTS_PER_GROUP * g_idx
    el = jnp.where(jnp.logical_and(lane >= lo, lane < lo + EXPERTS_PER_GROUP), logits, neg)
    l1 = jnp.max(el, axis=1, keepdims=True)
    i1 = jnp.min(jnp.where(el == l1, lane, LANES), axis=1, keepdims=True)
    el2 = jnp.where(lane == i1, neg, el)
    l2 = jnp.max(el2, axis=1, keepdims=True)
    i2 = jnp.min(jnp.where(el2 == l2, lane, LANES), axis=1, keepdims=True)
    e21 = jnp.exp(l2 - l1)
    w1 = g_w * (1.0 / (1.0 + e21))
    w2 = g_w * (e21 / (1.0 + e21))
    hit1 = lane == i1
    hit2 = lane == i2
    onehot = jnp.logical_or(hit1, hit2).astype(F32)
    r = lax.broadcasted_iota(jnp.int32, (tm, tm), 0)
    c = lax.broadcasted_iota(jnp.int32, (tm, tm), 1)
    before = (c < r).astype(BF16)
    cum = jnp.dot(before, onehot.astype(BF16), preferred_element_type=F32) + carry[0:1, :]
    rank1 = jnp.sum(jnp.where(hit1, cum, 0.0), axis=1, keepdims=True)
    rank2 = jnp.sum(jnp.where(hit2, cum, 0.0), axis=1, keepdims=True)
    info = jnp.where(lane == 0, (i1 - EXPERT_LANE0).astype(F32), 0.0)
    info = jnp.where(lane == 1, (i2 - EXPERT_LANE0).astype(F32), info)
    info = jnp.where(lane == 2, w1, info)
    info = jnp.where(lane == 3, w2, info)
    info = jnp.where(lane == 4, rank1, info)
    info = jnp.where(lane == 5, rank2, info)
    info_ref[...] = info
    total = carry[0:1, :] + jnp.sum(onehot, axis=0, keepdims=True)
    carry[...] = jnp.broadcast_to(total, carry.shape)
    cnt_ref[...] = jnp.broadcast_to(total, cnt_ref.shape)


def _route(x, g_ffn, w_r, b_r, count_in, tm):
    n = x.shape[0]
    const = lambda i: (0, 0)
    return pl.pallas_call(
        _route_kernel,
        grid=(n // tm,),
        in_specs=[
            pl.BlockSpec((tm, D_MODEL), lambda i: (i, 0)),
            pl.BlockSpec((1, D_MODEL), const),
            pl.BlockSpec((D_MODEL, LANES), const),
            pl.BlockSpec((1, LANES), const),
            pl.BlockSpec((8, LANES), const),
        ],
        out_specs=[
            pl.BlockSpec((tm, LANES), lambda i: (i, 0)),
            pl.BlockSpec((8, LANES), const),
        ],
        out_shape=[
            jax.ShapeDtypeStruct((n, LANES), F32),
            jax.ShapeDtypeStruct((8, LANES), F32),
        ],
        scratch_shapes=[pltpu.VMEM((8, LANES), F32)],
        compiler_params=_params("arbitrary"),
        name="route",
    )(x, g_ffn, w_r, b_r, count_in)


def _dispatch_kernel(dest_ref, x_hbm, pad_in, pad_out, sem, *, tile):
    del pad_in
    base = pl.program_id(0) * tile

    def row_copy(src_row, dst_row):
        return pltpu.make_async_copy(x_hbm.at[pl.ds(src_row, 1), :], pad_out.at[pl.ds(dst_row, 1), :], sem)

    def start(r, carry):
        row_copy(base + r, dest_ref[0, 0, 2 * r]).start()
        row_copy(base + r, dest_ref[0, 0, 2 * r + 1]).start()
        return carry

    def wait(r, carry):
        row_copy(0, 0).wait()
        row_copy(0, 0).wait()
        return carry

    lax.fori_loop(0, tile, start, 0)
    lax.fori_loop(0, tile, wait, 0)


def _dispatch(dest, x, x_pad, tile):
    n = x.shape[0]
    dest3 = dest.reshape(n // tile, 1, 2 * tile)
    return pl.pallas_call(
        functools.partial(_dispatch_kernel, tile=tile),
        grid=(n // tile,),
        in_specs=[
            pl.BlockSpec((1, 1, 2 * tile), lambda i: (i, 0, 0), memory_space=pltpu.SMEM),
            pl.BlockSpec(memory_space=pl.ANY),
            pl.BlockSpec(memory_space=pl.ANY),
        ],
        out_specs=pl.BlockSpec(memory_space=pl.ANY),
        out_shape=jax.ShapeDtypeStruct(x_pad.shape, x_pad.dtype),
        scratch_shapes=[pltpu.SemaphoreType.DMA(())],
        input_output_aliases={2: 0},
        compiler_params=_params("arbitrary"),
        name="dispatch",
    )(dest3, x, x_pad)


def _expert_kernel(be_ref, nused_ref, x_ref, g_ref, wg_ref, wu_ref, wd_ref, y_ref, wg_s, wu_s, wd_s):
    i = pl.program_id(0)
    prev = be_ref[jnp.maximum(i - 1, 0)]
    active = i < nused_ref[0]

    @pl.when(jnp.logical_and(active, jnp.logical_or(i == 0, be_ref[i] != prev)))
    def _():
        wg_s[...] = wg_ref[...].astype(BF16)
        wu_s[...] = wu_ref[...].astype(BF16)
        wd_s[...] = wd_ref[...].astype(BF16)

    @pl.when(active)
    def _():
        hn = _rms(x_ref[...], g_ref[...]).astype(BF16)
        gate = jnp.dot(hn, wg_s[...], preferred_element_type=F32)
        up = jnp.dot(hn, wu_s[...], preferred_element_type=F32)
        act = (gate * jax.nn.sigmoid(gate)) * up
        y_ref[...] = jnp.dot(act.astype(BF16), wd_s[...], preferred_element_type=F32)

    @pl.when(jnp.logical_not(active))
    def _():
        y_ref[...] = jnp.zeros_like(y_ref)


def _experts(block_expert, n_used, x_pad, g_ffn, w_gate, w_up, w_down):
    n_rows = x_pad.shape[0]
    nb = n_rows // EXPERT_ROWS
    grid_spec = pltpu.PrefetchScalarGridSpec(
        num_scalar_prefetch=2,
        grid=(nb,),
        in_specs=[
            pl.BlockSpec((EXPERT_ROWS, D_MODEL), lambda i, be, nu: (i, 0)),
            pl.BlockSpec((1, D_MODEL), lambda i, be, nu: (0, 0)),
            pl.BlockSpec((None, None, D_MODEL, D_EXPERT), lambda i, be, nu: (0, be[i], 0, 0)),
            pl.BlockSpec((None, None, D_MODEL, D_EXPERT), lambda i, be, nu: (0, be[i], 0, 0)),
            pl.BlockSpec((None, None, D_EXPERT, D_MODEL), lambda i, be, nu: (0, be[i], 0, 0)),
        ],
        out_specs=pl.BlockSpec((EXPERT_ROWS, D_MODEL), lambda i, be, nu: (i, 0)),
        scratch_shapes=[
            pltpu.VMEM((D_MODEL, D_EXPERT), BF16),
            pltpu.VMEM((D_MODEL, D_EXPERT), BF16),
            pltpu.VMEM((D_EXPERT, D_MODEL), BF16),
        ],
    )
    return pl.pallas_call(
        _expert_kernel,
        grid_spec=grid_spec,
        out_shape=jax.ShapeDtypeStruct((n_rows, D_MODEL), F32),
        compiler_params=_params("arbitrary"),
        name="experts",
    )(block_expert, n_used, x_pad, g_ffn, w_gate, w_up, w_down)


def _final_kernel(d0_ref, d1_ref, x_ref, info_ref, p_ref, y_hbm, wpg_ref, wpp_ref, gple_ref, gfin_ref,
                  out_ref, ybuf, sem):
    tm = x_ref.shape[0]

    def row_copy(src_row, slot, r):
        return pltpu.make_async_copy(y_hbm.at[pl.ds(src_row, 1), :], ybuf.at[slot, pl.ds(r, 1), :], sem)

    def start(r, carry):
        row_copy(d0_ref[0, 0, r], 0, r).start()
        row_copy(d1_ref[0, 0, r], 1, r).start()
        return carry

    def wait(r, carry):
        row_copy(0, 0, 0).wait()
        row_copy(0, 1, 0).wait()
        return carry

    lax.fori_loop(0, tm, start, 0)
    lax.fori_loop(0, tm, wait, 0)

    info = info_ref[...]
    x2 = x_ref[...] + (ybuf[0] * info[:, 2:3] + ybuf[1] * info[:, 3:4])
    gate = jax.nn.sigmoid(jnp.dot(_rms(x2, gple_ref[...]).astype(BF16), wpg_ref[...], preferred_element_type=F32))
    proj = jnp.dot(p_ref[...].astype(BF16), wpp_ref[...], preferred_element_type=F32)
    out_ref[...] = _rms(x2 + gate * proj, gfin_ref[...])


def _final(dest, x1, info, p, y_pad, w_pg, w_pp, g_ple, g_final, tm):
    n = x1.shape[0]
    d0 = dest[:, 0].reshape(n // tm, 1, tm)
    d1 = dest[:, 1].reshape(n // tm, 1, tm)
    const = lambda i: (0, 0)
    smem = lambda: pl.BlockSpec((1, 1, tm), lambda i: (i, 0, 0), memory_space=pltpu.SMEM)
    return pl.pallas_call(
        _final_kernel,
        grid=(n // tm,),
        in_specs=[
            smem(), smem(),
            pl.BlockSpec((tm, D_MODEL), lambda i: (i, 0)),
            pl.BlockSpec((tm, LANES), lambda i: (i, 0)),
            pl.BlockSpec((tm, PLE_DIM), lambda i: (i, 0)),
            pl.BlockSpec(memory_space=pl.ANY),
            pl.BlockSpec((D_MODEL, D_MODEL), const),
            pl.BlockSpec((PLE_DIM, D_MODEL), const),
            pl.BlockSpec((1, D_MODEL), const),
            pl.BlockSpec((1, D_MODEL), const),
        ],
        out_specs=pl.BlockSpec((tm, D_MODEL), lambda i: (i, 0)),
        out_shape=jax.ShapeDtypeStruct((n, D_MODEL), F32),
        scratch_shapes=[pltpu.VMEM((2, tm, D_MODEL), F32), pltpu.SemaphoreType.DMA(())],
        compiler_params=_params("arbitrary"),
        name="final",
    )(d0, d1, x1, info, p, y_pad, w_pg, w_pp, g_ple, g_final)


def _pad_lanes(a):
    return jnp.pad(a, ((0, 0), (0, LANES - a.shape[1])))


def kernel(x_prompt, x_sample, p_prompt, p_sample, state_mlstm_C, state_mlstm_n, state_mlstm_m, state_swa_k, state_swa_v, g_mix, w_in, b_igate, b_fgate, g_mh, attn_sinks, w_out, g_ffn, w_router_g, b_router_g, w_router_e, b_router_e, w_gate, w_up, w_down, g_ple, w_ple_gate, w_ple_proj, g_final):
    batch, seq, d = x_prompt.shape
    dec = x_sample.shape[0]
    n_p = batch * seq
    xp = x_prompt.reshape(n_p, d)
    xs = x_sample.reshape(dec, d)

    w_in0 = w_in[0]
    w_cat = jnp.concatenate([w_in0[:, :GATE_OFF], w_in0[:, ATT_OFF:]], axis=1).astype(BF16)
    w_g = w_in0[:, GATE_OFF:ATT_OFF]
    wgi = _pad_lanes(w_g[:, :M_HEADS])
    wgf = _pad_lanes(w_g[:, M_HEADS:])
    wgt = w_g.T
    brow = jnp.concatenate([_pad_lanes(b_igate), _pad_lanes(b_fgate)], axis=0)
    bcol = jnp.concatenate([b_igate[0], b_fgate[0]]).reshape(2 * M_HEADS, 1)
    w_om = w_out[0, :M_W].astype(BF16)
    w_oa = w_out[0, M_W:].astype(BF16)
    w_r = _pad_lanes(jnp.concatenate([w_router_g[0], w_router_e[0]], axis=1))
    b_r = _pad_lanes(jnp.concatenate([b_router_g, b_router_e], axis=1))
    w_pg = w_ple_gate[0].astype(BF16)
    w_pp = w_ple_proj[0].astype(BF16)
    sink_col = attn_sinks.reshape(A_HEADS, 1)

    zp, gip, gfp, grp = _inproj(xp, g_mix, w_cat, wgi, wgf, wgt, brow, bcol, tm=512, tn=512)
    om_p, c_p, n_p8, m_p8 = _mlstm_prompt(zp, gip, gfp, grp, g_mh, batch, seq)
    oa_p = _swa_prompt(zp, attn_sinks, batch, seq)
    x1p = _outproj(xp, om_p, oa_p, w_om, w_oa, tm=512)

    zs, gis, gfs, _ = _inproj(xs, g_mix, w_cat, wgi, wgf, wgt, brow, bcol, tm=dec, tn=512)
    om_s, c_s, n_s, m_s = _mlstm_sample(
        zs.reshape(dec, 1, Z_W), gis.reshape(dec, 1, LANES), gfs.reshape(dec, 1, LANES),
        state_mlstm_C[0], state_mlstm_n[0].reshape(dec, 1, M_W), state_mlstm_m[0].reshape(dec, 1, M_HEADS), g_mh)
    k_off = 4 * M_W + A_Q_W
    oa_s, k_s, v_s = _swa_sample(
        zs[:, 4 * M_W:k_off].reshape(dec, A_HEADS, A_HEAD_DIM),
        zs[:, k_off:k_off + A_KV_W].reshape(dec, 1, A_KV_W),
        zs[:, k_off + A_KV_W:].reshape(dec, 1, A_KV_W),
        state_swa_k[0].reshape(dec, WINDOW, A_KV_W), state_swa_v[0].reshape(dec, WINDOW, A_KV_W), sink_col)
    x1s = _outproj(xs, om_s.reshape(dec, M_W), oa_s.reshape(dec, A_Q_W).astype(BF16), w_om, w_oa, tm=dec)

    info_p, cnt_p = _route(x1p, g_ffn, w_r, b_r, jnp.zeros((8, LANES), F32), tm=512)
    info_s, cnt = _route(x1s, g_ffn, w_r, b_r, cnt_p, tm=dec)
    counts = cnt[0, EXPERT_LANE0:EXPERT_LANE0 + N_EXPERTS].astype(jnp.int32)
    padded = ((counts + EXPERT_ROWS - 1) // EXPERT_ROWS) * EXPERT_ROWS
    pad_end = jnp.cumsum(padded)
    pad_start = pad_end - padded
    n_assign = 2 * (n_p + dec)
    n_blocks = -(-n_assign // EXPERT_ROWS) + N_EXPERTS
    block_expert = jnp.minimum(
        jnp.searchsorted(pad_end, jnp.arange(n_blocks, dtype=jnp.int32) * EXPERT_ROWS, side='right'),
        N_EXPERTS - 1).astype(jnp.int32)
    n_used = (pad_end[-1:] // EXPERT_ROWS).astype(jnp.int32)

    def dest_of(info):
        e = info[:, 0:2].astype(jnp.int32)
        return pad_start[e] + info[:, 4:6].astype(jnp.int32)

    dest_p = dest_of(info_p)
    dest_s = dest_of(info_s)

    x_pad = jnp.zeros((n_blocks * EXPERT_ROWS, d), F32)
    x_pad = _dispatch(dest_p, x1p, x_pad, tile=512)
    x_pad = _dispatch(dest_s, x1s, x_pad, tile=dec)
    y_pad = _experts(block_expert, n_used, x_pad, g_ffn, w_gate, w_up, w_down)
    y_p = _final(dest_p, x1p, info_p, p_prompt[0].reshape(n_p, PLE_DIM), y_pad, w_pg, w_pp, g_ple, g_final.reshape(1, d), tm=256)
    y_s = _final(dest_s, x1s, info_s, p_sample[0].reshape(dec, PLE_DIM), y_pad, w_pg, w_pp, g_ple, g_final.reshape(1, d), tm=dec)

    zp3 = zp.reshape(batch, seq, Z_W)
    swa_k_p = zp3[:, seq - WINDOW:, k_off:k_off + A_KV_W].reshape(1, batch, WINDOW, A_KV_HEADS, A_HEAD_DIM)
    swa_v_p = zp3[:, seq - WINDOW:, k_off + A_KV_W:].reshape(1, batch, WINDOW, A_KV_HEADS, A_HEAD_DIM)
    return (
        y_p.reshape(batch, seq, d),
        y_s.reshape(dec, 1, d),
        c_p[None],
        n_p8[None, :, :M_HEADS, :],
        m_p8[None, :, :M_HEADS, 0],
        swa_k_p,
        swa_v_p,
        c_s[None],
        n_s.reshape(1, dec, M_HEADS, M_DK),
        m_s[None, :, 0, :M_HEADS],
        k_s.reshape(1, dec, WINDOW, A_KV_HEADS, A_HEAD_DIM),
        v_s.reshape(1, dec, WINDOW, A_KV_HEADS, A_HEAD_DIM),
    )
```

```python
import functools

import jax
import jax.numpy as jnp
from jax import lax
from jax.experimental import pallas as pl
from jax.experimental.pallas import tpu as pltpu

F32 = jnp.float32
BF16 = jnp.bfloat16
HI = lax.Precision.HIGHEST

D_MODEL = 2048
M_HEADS = 4
M_DK = 256
M_DV = 256
A_HEADS = 16
A_KV_HEADS = 4
A_HEAD_DIM = 64
A_GROUP = A_HEADS // A_KV_HEADS
WINDOW = 128
M_W = M_HEADS * M_DK
A_Q_W = A_HEADS * A_HEAD_DIM
A_KV_W = A_KV_HEADS * A_HEAD_DIM
GATE_OFF = 4 * M_W
ATT_OFF = GATE_OFF + 2 * M_HEADS
GATE_COL = 4 * M_W + A_Q_W + 2 * A_KV_W
Z_W = GATE_COL + 128
N_GROUPS = 4
EXPERTS_PER_GROUP = 8
N_EXPERTS = N_GROUPS * EXPERTS_PER_GROUP
D_EXPERT = 512
PLE_DIM = 256
EPS = 1e-6

LANES = 128
MLSTM_CHUNK = 256
EXPERT_ROWS = 256
VMEM_LIMIT = 56 * 1024 * 1024

NT_DIMS = (((1,), (1,)), ((), ()))
TN_DIMS = (((0,), (0,)), ((), ()))


def _params(*sem):
    return pltpu.CompilerParams(dimension_semantics=sem, vmem_limit_bytes=VMEM_LIMIT)


def _rms(x, g):
    r = lax.rsqrt(jnp.mean(x * x, axis=-1, keepdims=True) + EPS)
    return (x * r) * g


def _log_sigmoid(x):
    return jnp.minimum(x, 0.0) - jnp.log1p(jnp.exp(-jnp.abs(x)))


def _inproj_kernel(x_ref, g_ref, w_ref, z_ref, hn_scr, *, precision):
    @pl.when(pl.program_id(1) == 0)
    def _():
        hn_scr[...] = _rms(x_ref[...], g_ref[...]).astype(hn_scr.dtype)

    z_ref[...] = jnp.dot(hn_scr[...], w_ref[...], precision=precision, preferred_element_type=F32)


def _inproj(x, g_mix, w, tm, tn, precision=None):
    n = x.shape[0]
    width = w.shape[1]
    return pl.pallas_call(
        functools.partial(_inproj_kernel, precision=precision),
        grid=(n // tm, pl.cdiv(width, tn)),
        in_specs=[
            pl.BlockSpec((tm, D_MODEL), lambda i, j: (i, 0)),
            pl.BlockSpec((1, D_MODEL), lambda i, j: (0, 0)),
            pl.BlockSpec((D_MODEL, tn), lambda i, j: (0, j)),
        ],
        out_specs=pl.BlockSpec((tm, tn), lambda i, j: (i, j)),
        out_shape=jax.ShapeDtypeStruct((n, width), F32),
        scratch_shapes=[pltpu.VMEM((tm, D_MODEL), w.dtype)],
        compiler_params=_params("parallel", "arbitrary"),
        name="inproj",
    )(x, g_mix, w)


def _mlstm_out(h, gmh, mo):
    hn = h * lax.rsqrt(jnp.mean(h * h, axis=-1, keepdims=True) + EPS)
    return (hn * gmh) * jax.nn.sigmoid(mo)


def _gates(g_raw, bias):
    g = g_raw + bias
    lane = lax.broadcasted_iota(jnp.int32, g.shape, g.ndim - 1)
    return jnp.where(lane < M_HEADS, g, _log_sigmoid(g))


def _mlstm_kernel(q_ref, k_ref, v_ref, mo_ref, g_ref, bias_ref, gmh_ref,
                  o_ref, c_out, n_out, m_out, c_scr, n_scr, m_scr, *, chunk, n_chunks):
    c_idx = pl.program_id(1)

    @pl.when(c_idx == 0)
    def _():
        c_scr[...] = jnp.zeros_like(c_scr)
        n_scr[...] = jnp.zeros_like(n_scr)
        m_scr[...] = jnp.zeros_like(m_scr)

    L = chunk
    row = lax.broadcasted_iota(jnp.int32, (L, L), 0)
    col = lax.broadcasted_iota(jnp.int32, (L, L), 1)
    causal = row >= col
    ltri = causal.astype(F32)
    utri = (row <= col).astype(F32)
    g_c = _gates(g_ref[...], bias_ref[...])
    f_c = jnp.dot(ltri, g_c, precision=HI, preferred_element_type=F32)
    g_r = g_c.T[0:2 * M_HEADS, :]
    f_r = jnp.dot(g_r, utri, precision=HI, preferred_element_type=F32)

    for h in range(M_HEADS):
        hs = slice(h * M_DK, (h + 1) * M_DK)
        hf = M_HEADS + h
        a_row = g_r[h:h + 1, :] - f_r[hf:hf + 1, :]
        f_col = f_c[:, hf:hf + 1]
        a_col = g_c[:, h:h + 1] - f_col
        m0 = m_scr[h:h + 1, 0:1]
        a_max = jnp.max(jnp.where(causal, a_row, -jnp.inf), axis=1, keepdims=True)
        m_t = f_col + jnp.maximum(m0, a_max)
        dec0 = jnp.exp(f_col + m0 - m_t)
        d = jnp.exp(jnp.where(causal, (f_col - m_t) + a_row, -jnp.inf))
        q = q_ref[:, hs]
        k = k_ref[:, hs] * (M_DK ** -0.5)
        qb = q.astype(BF16)
        kb = k.astype(BF16)
        vb = v_ref[:, hs].astype(BF16)
        w = lax.dot_general(qb, kb, NT_DIMS, preferred_element_type=F32) * d
        c0 = c_scr[h]
        n0 = n_scr[h:h + 1, :]
        num = jnp.dot(w.astype(BF16), vb, preferred_element_type=F32) + dec0 * lax.dot_general(
            qb, c0.astype(BF16), NT_DIMS, preferred_element_type=F32)
        den = jnp.sum(w, axis=1, keepdims=True) + dec0 * jnp.sum(q * n0, axis=1, keepdims=True)
        hh = num / jnp.maximum(jnp.abs(den), jnp.exp(-m_t))
        m_end = m_t[L - 1:L, :]
        f_end = f_col[L - 1:L, :]
        w_end = jnp.exp(f_end + a_col - m_end)
        dec_end = jnp.exp(f_end + m0 - m_end)
        kw = k * w_end
        c_scr[h] = dec_end * c0 + lax.dot_general(vb, kw.astype(BF16), TN_DIMS, preferred_element_type=F32)
        n_scr[h:h + 1, :] = dec_end * n0 + jnp.sum(kw, axis=0, keepdims=True)
        m_scr[h:h + 1, :] = jnp.broadcast_to(m_end, (1, LANES))
        o_ref[:, hs] = _mlstm_out(hh, gmh_ref[:, hs], mo_ref[:, hs]).astype(o_ref.dtype)

    @pl.when(c_idx == n_chunks - 1)
    def _():
        c_out[0] = c_scr[...]
        n_out[0] = n_scr[...]
        m_out[0] = m_scr[...]


def _mlstm_prompt(z, gate_bias, g_mh, batch, seq):
    L = MLSTM_CHUNK
    nc = seq // L
    rows = lambda b, c: b * nc + c
    gate_blk = GATE_COL // LANES
    return pl.pallas_call(
        functools.partial(_mlstm_kernel, chunk=L, n_chunks=nc),
        grid=(batch, nc),
        in_specs=[
            pl.BlockSpec((L, M_W), lambda b, c: (rows(b, c), 0)),
            pl.BlockSpec((L, M_W), lambda b, c: (rows(b, c), 1)),
            pl.BlockSpec((L, M_W), lambda b, c: (rows(b, c), 2)),
            pl.BlockSpec((L, M_W), lambda b, c: (rows(b, c), 3)),
            pl.BlockSpec((L, LANES), lambda b, c: (rows(b, c), gate_blk)),
            pl.BlockSpec((1, LANES), lambda b, c: (0, 0)),
            pl.BlockSpec((1, M_W), lambda b, c: (0, 0)),
        ],
        out_specs=[
            pl.BlockSpec((L, M_W), lambda b, c: (rows(b, c), 0)),
            pl.BlockSpec((1, M_HEADS, M_DV, M_DK), lambda b, c: (b, 0, 0, 0)),
            pl.BlockSpec((1, 8, M_DK), lambda b, c: (b, 0, 0)),
            pl.BlockSpec((1, 8, LANES), lambda b, c: (b, 0, 0)),
        ],
        out_shape=[
            jax.ShapeDtypeStruct((batch * seq, M_W), BF16),
            jax.ShapeDtypeStruct((batch, M_HEADS, M_DV, M_DK), F32),
            jax.ShapeDtypeStruct((batch, 8, M_DK), F32),
            jax.ShapeDtypeStruct((batch, 8, LANES), F32),
        ],
        scratch_shapes=[
            pltpu.VMEM((M_HEADS, M_DV, M_DK), F32),
            pltpu.VMEM((8, M_DK), F32),
            pltpu.VMEM((8, LANES), F32),
        ],
        compiler_params=_params("parallel", "arbitrary"),
        name="mlstm_prompt",
    )(z, z, z, z, z, gate_bias, g_mh)


def _mlstm_step_kernel(z_ref, bias_ref, c_ref, n_ref, m_ref, gmh_ref, o_ref, c_out, n_out, m_out):
    z = z_ref[0]
    gates = _gates(z[:, GATE_OFF:ATT_OFF], bias_ref[...])
    m_in = m_ref[0]
    row = lax.broadcasted_iota(jnp.int32, (M_DV, M_DV), 0)
    col = lax.broadcasted_iota(jnp.int32, (M_DV, M_DV), 1)
    eye = row == col
    lane = lax.broadcasted_iota(jnp.int32, (1, LANES), 1)
    m_new = jnp.zeros((1, LANES), F32)
    for h in range(M_HEADS):
        hs = slice(h * M_DK, (h + 1) * M_DK)
        q = z[:, h * M_DK:(h + 1) * M_DK]
        k = z[:, M_W + h * M_DK:M_W + (h + 1) * M_DK] * (M_DK ** -0.5)
        v = z[:, 2 * M_W + h * M_DV:2 * M_W + (h + 1) * M_DV]
        mo = z[:, 3 * M_W + h * M_DV:3 * M_W + (h + 1) * M_DV]
        ip = gates[:, h:h + 1]
        lf = gates[:, M_HEADS + h:M_HEADS + h + 1]
        m0 = m_in[:, h:h + 1]
        m_t = jnp.maximum(lf + m0, ip)
        dec0 = jnp.exp(lf + m0 - m_t)
        d = jnp.exp(ip - m_t)
        c0 = c_ref[0, h]
        n0 = n_ref[0, :, hs]
        w = jnp.sum(q * k, axis=1, keepdims=True) * d
        cq = lax.dot_general(q, c0, NT_DIMS, precision=HI, preferred_element_type=F32)
        num = w * v + dec0 * cq
        den = w + dec0 * jnp.sum(n0 * q, axis=1, keepdims=True)
        hh = num / jnp.maximum(jnp.abs(den), jnp.exp(-m_t))
        v_col = jnp.sum(jnp.where(eye, v, 0.0), axis=1, keepdims=True)
        c_out[0, h] = dec0 * c0 + (d * v_col) * k
        n_out[0, :, hs] = dec0 * n0 + d * k
        m_new = jnp.where(lane == h, m_t, m_new)
        o_ref[0, :, hs] = _mlstm_out(hh, gmh_ref[:, hs], mo).astype(o_ref.dtype)
    m_out[0] = m_new


def _mlstm_sample(z3, gate_bias, c0, n0, m0, g_mh):
    nb = z3.shape[0]
    tok = lambda b: (b, 0, 0)
    return pl.pallas_call(
        _mlstm_step_kernel,
        grid=(nb,),
        in_specs=[
            pl.BlockSpec((1, 1, z3.shape[2]), tok),
            pl.BlockSpec((1, 2 * M_HEADS), lambda b: (0, 0)),
            pl.BlockSpec((1, M_HEADS, M_DV, M_DK), lambda b: (b, 0, 0, 0)),
            pl.BlockSpec((1, 1, M_W), tok),
            pl.BlockSpec((1, 1, M_HEADS), tok),
            pl.BlockSpec((1, M_W), lambda b: (0, 0)),
        ],
        out_specs=[
            pl.BlockSpec((1, 1, M_W), tok),
            pl.BlockSpec((1, M_HEADS, M_DV, M_DK), lambda b: (b, 0, 0, 0)),
            pl.BlockSpec((1, 1, M_W), tok),
            pl.BlockSpec((1, 1, LANES), tok),
        ],
        out_shape=[
            jax.ShapeDtypeStruct((nb, 1, M_W), F32),
            jax.ShapeDtypeStruct((nb, M_HEADS, M_DV, M_DK), F32),
            jax.ShapeDtypeStruct((nb, 1, M_W), F32),
            jax.ShapeDtypeStruct((nb, 1, LANES), F32),
        ],
        compiler_params=_params("parallel"),
        name="mlstm_sample",
    )(z3, gate_bias, c0, n0, m0, g_mh)


def _swa_kernel(sink_ref, q_ref, kc_ref, kp_ref, vc_ref, vp_ref, o_ref):
    j = pl.program_id(1)
    t = lax.broadcasted_iota(jnp.int32, (WINDOW, WINDOW), 0)
    kk = lax.broadcasted_iota(jnp.int32, (WINDOW, WINDOW), 1)
    mask_cur = kk <= t
    mask_prev = jnp.logical_and(kk > t, j > 0)
    scale = A_HEAD_DIM ** -0.5
    for g in range(A_KV_HEADS):
        gs = slice(g * A_HEAD_DIM, (g + 1) * A_HEAD_DIM)
        kc = kc_ref[:, gs].astype(BF16)
        kp = kp_ref[:, gs].astype(BF16)
        vc = vc_ref[:, gs].astype(BF16)
        vp = vp_ref[:, gs].astype(BF16)
        for i in range(A_GROUP):
            hd = g * A_GROUP + i
            hsl = slice(hd * A_HEAD_DIM, (hd + 1) * A_HEAD_DIM)
            q = q_ref[:, hsl].astype(BF16)
            sc = lax.dot_general(q, kc, NT_DIMS, preferred_element_type=F32) * scale
            sp = lax.dot_general(q, kp, NT_DIMS, preferred_element_type=F32) * scale
            sc = jnp.where(mask_cur, sc, -jnp.inf)
            sp = jnp.where(mask_prev, sp, -jnp.inf)
            sink = sink_ref[0, hd]
            mx = jnp.maximum(jnp.maximum(jnp.max(sc, axis=1, keepdims=True),
                                         jnp.max(sp, axis=1, keepdims=True)), sink)
            ec = jnp.exp(sc - mx)
            ep = jnp.exp(sp - mx)
            den = jnp.sum(ec, axis=1, keepdims=True) + jnp.sum(ep, axis=1, keepdims=True) + jnp.exp(sink - mx)
            o = (jnp.dot((ec / den).astype(BF16), vc, preferred_element_type=F32)
                 + jnp.dot((ep / den).astype(BF16), vp, preferred_element_type=F32))
            o_ref[:, hsl] = o.astype(o_ref.dtype)


def _swa_prompt(z, sinks, batch, seq):
    nb = seq // WINDOW
    q_blk = (4 * M_W) // A_Q_W
    k_blk = (4 * M_W + A_Q_W) // A_KV_W
    cur = lambda b, j: b * nb + j
    prev = lambda b, j: b * nb + jnp.maximum(j - 1, 0)
    return pl.pallas_call(
        _swa_kernel,
        grid=(batch, nb),
        in_specs=[
            pl.BlockSpec(memory_space=pltpu.SMEM),
            pl.BlockSpec((WINDOW, A_Q_W), lambda b, j: (cur(b, j), q_blk)),
            pl.BlockSpec((WINDOW, A_KV_W), lambda b, j: (cur(b, j), k_blk)),
            pl.BlockSpec((WINDOW, A_KV_W), lambda b, j: (prev(b, j), k_blk)),
            pl.BlockSpec((WINDOW, A_KV_W), lambda b, j: (cur(b, j), k_blk + 1)),
            pl.BlockSpec((WINDOW, A_KV_W), lambda b, j: (prev(b, j), k_blk + 1)),
        ],
        out_specs=pl.BlockSpec((WINDOW, A_Q_W), lambda b, j: (cur(b, j), 0)),
        out_shape=jax.ShapeDtypeStruct((batch * seq, A_Q_W), BF16),
        compiler_params=_params("parallel", "arbitrary"),
        name="swa_prompt",
    )(sinks, z, z, z, z, z)


def _swa_step_kernel(q_ref, kn_ref, vn_ref, bk_ref, bv_ref, sink_ref, o_ref, ko_ref, vo_ref):
    bk = bk_ref[0]
    bv = bv_ref[0]
    kn = kn_ref[0]
    vn = vn_ref[0]
    in_window = lax.broadcasted_iota(jnp.int32, (1, WINDOW), 1) >= 1
    scale = A_HEAD_DIM ** -0.5
    for g in range(A_KV_HEADS):
        gs = slice(g * A_HEAD_DIM, (g + 1) * A_HEAD_DIM)
        rs = slice(g * A_GROUP, (g + 1) * A_GROUP)
        q = q_ref[0, rs, :]
        s = lax.dot_general(q, bk[:, gs], NT_DIMS, precision=HI, preferred_element_type=F32) * scale
        s = jnp.where(in_window, s, -jnp.inf)
        s_new = jnp.sum(q * kn[:, gs], axis=1, keepdims=True) * scale
        sink = sink_ref[rs, :]
        mx = jnp.maximum(jnp.maximum(jnp.max(s, axis=1, keepdims=True), s_new), sink)
        e = jnp.exp(s - mx)
        e_new = jnp.exp(s_new - mx)
        den = jnp.sum(e, axis=1, keepdims=True) + e_new + jnp.exp(sink - mx)
        o = jnp.dot(e / den, bv[:, gs], precision=HI, preferred_element_type=F32)
        o_ref[0, rs, :] = o + (e_new / den) * vn[:, gs]
    ko_ref[0, 0:WINDOW - 1, :] = bk[1:WINDOW, :]
    ko_ref[0, WINDOW - 1:WINDOW, :] = kn
    vo_ref[0, 0:WINDOW - 1, :] = bv[1:WINDOW, :]
    vo_ref[0, WINDOW - 1:WINDOW, :] = vn


def _swa_sample(q3, kn3, vn3, buf_k, buf_v, sink_col):
    nb = q3.shape[0]
    tok = lambda b: (b, 0, 0)
    return pl.pallas_call(
        _swa_step_kernel,
        grid=(nb,),
        in_specs=[
            pl.BlockSpec((1, A_HEADS, A_HEAD_DIM), tok),
            pl.BlockSpec((1, 1, A_KV_W), tok),
            pl.BlockSpec((1, 1, A_KV_W), tok),
            pl.BlockSpec((1, WINDOW, A_KV_W), tok),
            pl.BlockSpec((1, WINDOW, A_KV_W), tok),
            pl.BlockSpec((A_HEADS, 1), lambda b: (0, 0)),
        ],
        out_specs=[
            pl.BlockSpec((1, A_HEADS, A_HEAD_DIM), tok),
            pl.BlockSpec((1, WINDOW, A_KV_W), tok),
            pl.BlockSpec((1, WINDOW, A_KV_W), tok),
        ],
        out_shape=[
            jax.ShapeDtypeStruct((nb, A_HEADS, A_HEAD_DIM), F32),
            jax.ShapeDtypeStruct((nb, WINDOW, A_KV_W), F32),
            jax.ShapeDtypeStruct((nb, WINDOW, A_KV_W), F32),
        ],
        compiler_params=_params("parallel"),
        name="swa_sample",
    )(q3, kn3, vn3, buf_k, buf_v, sink_col)


def _outproj_kernel(x_ref, om_ref, oa_ref, wm_ref, wa_ref, y_ref, *, precision):
    y_ref[...] = (x_ref[...]
                  + jnp.dot(om_ref[...], wm_ref[...], precision=precision, preferred_element_type=F32)
                  + jnp.dot(oa_ref[...], wa_ref[...], precision=precision, preferred_element_type=F32))


def _outproj(x, o_m, o_a, w, tm, precision=None):
    n = x.shape[0]
    return pl.pallas_call(
        functools.partial(_outproj_kernel, precision=precision),
        grid=(n // tm,),
        in_specs=[
            pl.BlockSpec((tm, D_MODEL), lambda i: (i, 0)),
            pl.BlockSpec((tm, M_W), lambda i: (i, 0)),
            pl.BlockSpec((tm, A_Q_W), lambda i: (i, 0)),
            pl.BlockSpec((M_W, D_MODEL), lambda i: (0, 0)),
            pl.BlockSpec((A_Q_W, D_MODEL), lambda i: (1, 0)),
        ],
        out_specs=pl.BlockSpec((tm, D_MODEL), lambda i: (i, 0)),
        out_shape=jax.ShapeDtypeStruct((n, D_MODEL), F32),
        compiler_params=_params("parallel"),
        name="outproj",
    )(x, o_m, o_a, w, w)


GROUP_LANE0 = 0
EXPERT_LANE0 = N_GROUPS


def _route_kernel(x_ref, g_ref, wr_ref, br_ref, cin_ref, info_ref, cnt_ref, carry):
    @pl.when(pl.program_id(0) == 0)
    def _():
        carry[...] = cin_ref[...]

    tm = x_ref.shape[0]
    hn = _rms(x_ref[...], g_ref[...])
    logits = jnp.dot(hn, wr_ref[...], precision=HI, preferred_element_type=F32) + br_ref[...]
    lane = lax.broadcasted_iota(jnp.int32, (tm, LANES), 1)
    neg = -jnp.inf
    gl = jnp.where(lane < N_GROUPS, logits, neg)
    g_max = jnp.max(gl, axis=1, keepdims=True)
    g_idx = jnp.min(jnp.where(gl == g_max, lane, LANES), axis=1, keepdims=True)
    g_w = 1.0 / jnp.sum(jnp.exp(gl - g_max), axis=1, keepdims=True)
    lo = EXPERT_LANE0 + EXPERTS_PER_GROUP * g_idx
    el = jnp.where(jnp.logical_and(lane >= lo, lane < lo + EXPERTS_PER_GROUP), logits, neg)
    l1 = jnp.max(el, axis=1, keepdims=True)
    i1 = jnp.min(jnp.where(el == l1, lane, LANES), axis=1, keepdims=True)
    el2 = jnp.where(lane == i1, neg, el)
    l2 = jnp.max(el2, axis=1, keepdims=True)
    i2 = jnp.min(jnp.where(el2 == l2, lane, LANES), axis=1, keepdims=True)
    e21 = jnp.exp(l2 - l1)
    w1 = g_w * (1.0 / (1.0 + e21))
    w2 = g_w * (e21 / (1.0 + e21))
    hit1 = lane == i1
    hit2 = lane == i2
    onehot = jnp.logical_or(hit1, hit2).astype(F32)
    r = lax.broadcasted_iota(jnp.int32, (tm, tm), 0)
    c = lax.broadcasted_iota(jnp.int32, (tm, tm), 1)
    before = (c < r).astype(BF16)
    cum = jnp.dot(before, onehot.astype(BF16), preferred_element_type=F32) + carry[0:1, :]
    rank1 = jnp.sum(jnp.where(hit1, cum, 0.0), axis=1, keepdims=True)
    rank2 = jnp.sum(jnp.where(hit2, cum, 0.0), axis=1, keepdims=True)
    info = jnp.where(lane == 0, (i1 - EXPERT_LANE0).astype(F32), 0.0)
    info = jnp.where(lane == 1, (i2 - EXPERT_LANE0).astype(F32), info)
    info = jnp.where(lane == 2, w1, info)
    info = jnp.where(lane == 3, w2, info)
    info = jnp.where(lane == 4, rank1, info)
    info = jnp.where(lane == 5, rank2, info)
    info_ref[...] = info
    total = carry[0:1, :] + jnp.sum(onehot, axis=0, keepdims=True)
    carry[...] = jnp.broadcast_to(total, carry.shape)
    cnt_ref[...] = jnp.broadcast_to(total, cnt_ref.shape)


def _route(x, g_ffn, w_r, b_r, count_in, tm):
    n = x.shape[0]
    const = lambda i: (0, 0)
    return pl.pallas_call(
        _route_kernel,
        grid=(n // tm,),
        in_specs=[
            pl.BlockSpec((tm, D_MODEL), lambda i: (i, 0)),
            pl.BlockSpec((1, D_MODEL), const),
            pl.BlockSpec((D_MODEL, LANES), const),
            pl.BlockSpec((1, LANES), const),
            pl.BlockSpec((8, LANES), const),
        ],
        out_specs=[
            pl.BlockSpec((tm, LANES), lambda i: (i, 0)),
            pl.BlockSpec((8, LANES), const),
        ],
        out_shape=[
            jax.ShapeDtypeStruct((n, LANES), F32),
            jax.ShapeDtypeStruct((8, LANES), F32),
        ],
        scratch_shapes=[pltpu.VMEM((8, LANES), F32)],
        compiler_params=_params("arbitrary"),
        name="route",
    )(x, g_ffn, w_r, b_r, count_in)


DMA_UNROLL = 8


def _scatter_rows(dest_ref, x_ref, pad_out, sem, tile):
    def row_copy(r, dst_row):
        return pltpu.make_async_copy(x_ref.at[pl.ds(r, 1), :], pad_out.at[pl.ds(dst_row, 1), :], sem)

    def start(r, carry):
        row_copy(r, dest_ref[0, 0, 2 * r]).start()
        row_copy(r, dest_ref[0, 0, 2 * r + 1]).start()
        return carry

    def wait(r, carry):
        row_copy(0, 0).wait()
        row_copy(0, 0).wait()
        return carry

    lax.fori_loop(0, tile, start, 0, unroll=DMA_UNROLL)
    lax.fori_loop(0, tile, wait, 0, unroll=DMA_UNROLL)


def _dispatch_kernel(zero_ref, dest_p_ref, dest_s_ref, xp_ref, xs_ref, pad_out, zero_scr, sem, zsem,
                     *, tile, n_tiles):
    i = pl.program_id(0)

    @pl.when(i == 0)
    def _():
        zero_scr[...] = jnp.zeros_like(zero_scr)

        def zero_copy(e):
            row0 = pl.multiple_of(zero_ref[e], EXPERT_ROWS)
            return pltpu.make_async_copy(zero_scr, pad_out.at[pl.ds(row0, EXPERT_ROWS), :], zsem)

        for e in range(zero_ref.shape[0]):
            @pl.when(zero_ref[e] >= 0)
            def _():
                zero_copy(e).start()
        for e in range(zero_ref.shape[0]):
            @pl.when(zero_ref[e] >= 0)
            def _():
                zero_copy(e).wait()

    @pl.when(i < n_tiles)
    def _():
        _scatter_rows(dest_p_ref, xp_ref, pad_out, sem, tile)

    @pl.when(i == n_tiles)
    def _():
        _scatter_rows(dest_s_ref, xs_ref, pad_out, sem, xs_ref.shape[0])


def _dispatch(zero_rows, dest_p, dest_s, xp, xs, n_rows, tile):
    n_tiles = xp.shape[0] // tile
    n_s = xs.shape[0]
    last = n_tiles - 1
    return pl.pallas_call(
        functools.partial(_dispatch_kernel, tile=tile, n_tiles=n_tiles),
        grid=(n_tiles + 1,),
        in_specs=[
            pl.BlockSpec(memory_space=pltpu.SMEM),
            pl.BlockSpec((1, 1, 2 * tile), lambda i: (jnp.minimum(i, last), 0, 0), memory_space=pltpu.SMEM),
            pl.BlockSpec(memory_space=pltpu.SMEM),
            pl.BlockSpec((tile, D_MODEL), lambda i: (jnp.minimum(i, last), 0)),
            pl.BlockSpec((n_s, D_MODEL), lambda i: (0, 0)),
        ],
        out_specs=pl.BlockSpec(memory_space=pl.ANY),
        out_shape=jax.ShapeDtypeStruct((n_rows, D_MODEL), F32),
        scratch_shapes=[pltpu.VMEM((EXPERT_ROWS, D_MODEL), F32), pltpu.SemaphoreType.DMA(()),
                        pltpu.SemaphoreType.DMA(())],
        compiler_params=_params("arbitrary"),
        name="dispatch",
    )(zero_rows, dest_p.reshape(n_tiles, 1, 2 * tile), dest_s.reshape(1, 1, 2 * n_s), xp, xs)


def _expert_kernel(be_ref, nused_ref, x_ref, g_ref, wg_ref, wu_ref, wd_ref, y_ref, wg_s, wu_s, wd_s):
    i = pl.program_id(0)
    prev = be_ref[jnp.maximum(i - 1, 0)]
    active = i < nused_ref[0]

    @pl.when(jnp.logical_and(active, jnp.logical_or(i == 0, be_ref[i] != prev)))
    def _():
        wg_s[...] = wg_ref[...].astype(BF16)
        wu_s[...] = wu_ref[...].astype(BF16)
        wd_s[...] = wd_ref[...].astype(BF16)

    @pl.when(active)
    def _():
        hn = _rms(x_ref[...], g_ref[...]).astype(BF16)
        gate = jnp.dot(hn, wg_s[...], preferred_element_type=F32)
        up = jnp.dot(hn, wu_s[...], preferred_element_type=F32)
        act = (gate * jax.nn.sigmoid(gate)) * up
        y_ref[...] = jnp.dot(act.astype(BF16), wd_s[...], preferred_element_type=F32)

    @pl.when(jnp.logical_not(active))
    def _():
        y_ref[...] = jnp.zeros_like(y_ref)


def _experts(block_expert, n_used, x_pad, g_ffn, w_gate, w_up, w_down):
    n_rows = x_pad.shape[0]
    nb = n_rows // EXPERT_ROWS
    grid_spec = pltpu.PrefetchScalarGridSpec(
        num_scalar_prefetch=2,
        grid=(nb,),
        in_specs=[
            pl.BlockSpec((EXPERT_ROWS, D_MODEL), lambda i, be, nu: (i, 0)),
            pl.BlockSpec((1, D_MODEL), lambda i, be, nu: (0, 0)),
            pl.BlockSpec((None, None, D_MODEL, D_EXPERT), lambda i, be, nu: (0, be[i], 0, 0)),
            pl.BlockSpec((None, None, D_MODEL, D_EXPERT), lambda i, be, nu: (0, be[i], 0, 0)),
            pl.BlockSpec((None, None, D_EXPERT, D_MODEL), lambda i, be, nu: (0, be[i], 0, 0)),
        ],
        out_specs=pl.BlockSpec((EXPERT_ROWS, D_MODEL), lambda i, be, nu: (i, 0)),
        scratch_shapes=[
            pltpu.VMEM((D_MODEL, D_EXPERT), BF16),
            pltpu.VMEM((D_MODEL, D_EXPERT), BF16),
            pltpu.VMEM((D_EXPERT, D_MODEL), BF16),
        ],
    )
    return pl.pallas_call(
        _expert_kernel,
        grid_spec=grid_spec,
        out_shape=jax.ShapeDtypeStruct((n_rows, D_MODEL), F32),
        compiler_params=_params("arbitrary"),
        name="experts",
    )(block_expert, n_used, x_pad, g_ffn, w_gate, w_up, w_down)


def _final_kernel(d0_ref, d1_ref, x_ref, info_ref, p_ref, y_hbm, wpg_ref, wpp_ref, gple_ref, gfin_ref,
                  out_ref, ybuf, sem):
    tm = x_ref.shape[0]

    def row_copy(src_row, slot, r):
        return pltpu.make_async_copy(y_hbm.at[pl.ds(src_row, 1), :], ybuf.at[slot, pl.ds(r, 1), :], sem)

    def start(r, carry):
        row_copy(d0_ref[0, 0, r], 0, r).start()
        row_copy(d1_ref[0, 0, r], 1, r).start()
        return carry

    def wait(r, carry):
        row_copy(0, 0, 0).wait()
        row_copy(0, 1, 0).wait()
        return carry

    lax.fori_loop(0, tm, start, 0, unroll=DMA_UNROLL)
    lax.fori_loop(0, tm, wait, 0, unroll=DMA_UNROLL)

    info = info_ref[...]
    x2 = x_ref[...] + (ybuf[0] * info[:, 2:3] + ybuf[1] * info[:, 3:4])
    gate = jax.nn.sigmoid(jnp.dot(_rms(x2, gple_ref[...]).astype(BF16), wpg_ref[...], preferred_element_type=F32))
    proj = jnp.dot(p_ref[...].astype(BF16), wpp_ref[...], preferred_element_type=F32)
    out_ref[...] = _rms(x2 + gate * proj, gfin_ref[...])


def _final(dest, x1, info, p, y_pad, w_pg, w_pp, g_ple, g_final, tm):
    n = x1.shape[0]
    d0 = dest[:, 0].reshape(n // tm, 1, tm)
    d1 = dest[:, 1].reshape(n // tm, 1, tm)
    const = lambda i: (0, 0)
    smem = lambda: pl.BlockSpec((1, 1, tm), lambda i: (i, 0, 0), memory_space=pltpu.SMEM)
    return pl.pallas_call(
        _final_kernel,
        grid=(n // tm,),
        in_specs=[
            smem(), smem(),
            pl.BlockSpec((tm, D_MODEL), lambda i: (i, 0)),
            pl.BlockSpec((tm, LANES), lambda i: (i, 0)),
            pl.BlockSpec((tm, PLE_DIM), lambda i: (i, 0)),
            pl.BlockSpec(memory_space=pl.ANY),
            pl.BlockSpec((D_MODEL, D_MODEL), const),
            pl.BlockSpec((PLE_DIM, D_MODEL), const),
            pl.BlockSpec((1, D_MODEL), const),
            pl.BlockSpec((1, D_MODEL), const),
        ],
        out_specs=pl.BlockSpec((tm, D_MODEL), lambda i: (i, 0)),
        out_shape=jax.ShapeDtypeStruct((n, D_MODEL), F32),
        scratch_shapes=[pltpu.VMEM((2, tm, D_MODEL), F32), pltpu.SemaphoreType.DMA(())],
        compiler_params=_params("arbitrary"),
        name="final",
    )(d0, d1, x1, info, p, y_pad, w_pg, w_pp, g_ple, g_final)


def _pad_lanes(a):
    return jnp.pad(a, ((0, 0), (0, LANES - a.shape[1])))


def kernel(x_prompt, x_sample, p_prompt, p_sample, state_mlstm_C, state_mlstm_n, state_mlstm_m, state_swa_k, state_swa_v, g_mix, w_in, b_igate, b_fgate, g_mh, attn_sinks, w_out, g_ffn, w_router_g, b_router_g, w_router_e, b_router_e, w_gate, w_up, w_down, g_ple, w_ple_gate, w_ple_proj, g_final):
    batch, seq, d = x_prompt.shape
    dec = x_sample.shape[0]
    n_p = batch * seq
    xp = x_prompt.reshape(n_p, d)
    xs = x_sample.reshape(dec, d)

    w_in0 = w_in[0]
    w_cat = jnp.concatenate(
        [w_in0[:, :GATE_OFF], w_in0[:, ATT_OFF:], _pad_lanes(w_in0[:, GATE_OFF:ATT_OFF])], axis=1).astype(BF16)
    gate_bias8 = jnp.concatenate([b_igate, b_fgate], axis=1)
    gate_bias = _pad_lanes(gate_bias8)
    w_o = w_out[0].astype(BF16)
    w_r = _pad_lanes(jnp.concatenate([w_router_g[0], w_router_e[0]], axis=1))
    b_r = _pad_lanes(jnp.concatenate([b_router_g, b_router_e], axis=1))
    w_pg = w_ple_gate[0].astype(BF16)
    w_pp = w_ple_proj[0].astype(BF16)
    sink_col = attn_sinks.reshape(A_HEADS, 1)

    zp = _inproj(xp, g_mix, w_cat, tm=1024, tn=640)
    om_p, c_p, n_p8, m_p8 = _mlstm_prompt(zp, gate_bias, g_mh, batch, seq)
    oa_p = _swa_prompt(zp, attn_sinks, batch, seq)
    x1p = _outproj(xp, om_p, oa_p, w_o, tm=512)
    k_off = 4 * M_W + A_Q_W
    v_off = k_off + A_KV_W

    zs = _inproj(xs, g_mix, w_in0, tm=dec, tn=512, precision=HI)
    om_s, c_s, n_s, m_s = _mlstm_sample(
        zs.reshape(dec, 1, zs.shape[1]), gate_bias8,
        state_mlstm_C[0], state_mlstm_n[0].reshape(dec, 1, M_W), state_mlstm_m[0].reshape(dec, 1, M_HEADS), g_mh)
    ks_off = ATT_OFF + A_Q_W
    vs_off = ks_off + A_KV_W
    oa_s, k_s, v_s = _swa_sample(
        zs[:, ATT_OFF:ks_off].reshape(dec, A_HEADS, A_HEAD_DIM),
        zs[:, ks_off:vs_off].reshape(dec, 1, A_KV_W),
        zs[:, vs_off:].reshape(dec, 1, A_KV_W),
        state_swa_k[0].reshape(dec, WINDOW, A_KV_W), state_swa_v[0].reshape(dec, WINDOW, A_KV_W), sink_col)
    x1s = _outproj(xs, om_s.reshape(dec, M_W), oa_s.reshape(dec, A_Q_W), w_out[0], tm=dec, precision=HI)

    info_p, cnt_p = _route(x1p, g_ffn, w_r, b_r, jnp.zeros((8, LANES), F32), tm=512)
    info_s, cnt = _route(x1s, g_ffn, w_r, b_r, cnt_p, tm=dec)
    counts = cnt[0, EXPERT_LANE0:EXPERT_LANE0 + N_EXPERTS].astype(jnp.int32)
    padded = ((counts + EXPERT_ROWS - 1) // EXPERT_ROWS) * EXPERT_ROWS
    pad_end = jnp.cumsum(padded)
    pad_start = pad_end - padded
    n_assign = 2 * (n_p + dec)
    n_blocks = -(-n_assign // EXPERT_ROWS) + N_EXPERTS
    block_row0 = jnp.arange(n_blocks, dtype=jnp.int32) * EXPERT_ROWS
    block_expert = jnp.minimum(
        jnp.sum((pad_end[None, :] <= block_row0[:, None]).astype(jnp.int32), axis=1), N_EXPERTS - 1)
    n_used = (pad_end[-1:] // EXPERT_ROWS).astype(jnp.int32)
    spare = n_used + jnp.arange(N_EXPERTS, dtype=jnp.int32)
    zero_rows = jnp.concatenate([
        jnp.where(padded > 0, pad_end - EXPERT_ROWS, -1),
        jnp.where(spare < n_blocks, spare * EXPERT_ROWS, -1)]).astype(jnp.int32)

    def dest_of(info):
        e = info[:, 0:2].astype(jnp.int32)
        return pad_start[e] + info[:, 4:6].astype(jnp.int32)

    dest_p = dest_of(info_p)
    dest_s = dest_of(info_s)

    x_pad = _dispatch(zero_rows, dest_p, dest_s, x1p, x1s, n_blocks * EXPERT_ROWS, tile=512)
    y_pad = _experts(block_expert, n_used, x_pad, g_ffn, w_gate, w_up, w_down)
    y_p = _final(dest_p, x1p, info_p, p_prompt[0].reshape(n_p, PLE_DIM), y_pad, w_pg, w_pp, g_ple, g_final.reshape(1, d), tm=256)
    y_s = _final(dest_s, x1s, info_s, p_sample[0].reshape(dec, PLE_DIM), y_pad, w_pg, w_pp, g_ple, g_final.reshape(1, d), tm=dec)

    zp3 = zp.reshape(batch, seq, Z_W)
    swa_k_p = zp3[:, seq - WINDOW:, k_off:v_off].reshape(1, batch, WINDOW, A_KV_HEADS, A_HEAD_DIM)
    swa_v_p = zp3[:, seq - WINDOW:, v_off:GATE_COL].reshape(1, batch, WINDOW, A_KV_HEADS, A_HEAD_DIM)
    return (
        y_p.reshape(batch, seq, d),
        y_s.reshape(dec, 1, d),
        c_p[None],
        n_p8[None, :, :M_HEADS, :],
        m_p8[None, :, :M_HEADS, 0],
        swa_k_p,
        swa_v_p,
        c_s[None],
        n_s.reshape(1, dec, M_HEADS, M_DK),
        m_s[None, :, 0, :M_HEADS],
        k_s.reshape(1, dec, WINDOW, A_KV_HEADS, A_HEAD_DIM),
        v_s.reshape(1, dec, WINDOW, A_KV_HEADS, A_HEAD_DIM),
    )
```

```python
import functools

import jax
import jax.numpy as jnp
from jax import lax
from jax.experimental import pallas as pl
from jax.experimental.pallas import tpu as pltpu

F32 = jnp.float32
BF16 = jnp.bfloat16
HI = lax.Precision.HIGHEST

D_MODEL = 2048
M_HEADS = 4
M_DK = 256
M_DV = 256
A_HEADS = 16
A_KV_HEADS = 4
A_HEAD_DIM = 64
A_GROUP = A_HEADS // A_KV_HEADS
WINDOW = 128
M_W = M_HEADS * M_DK
A_Q_W = A_HEADS * A_HEAD_DIM
A_KV_W = A_KV_HEADS * A_HEAD_DIM
GATE_OFF = 4 * M_W
ATT_OFF = GATE_OFF + 2 * M_HEADS
GATE_COL = 4 * M_W + A_Q_W + 2 * A_KV_W
Z_W = GATE_COL + 128
N_GROUPS = 4
EXPERTS_PER_GROUP = 8
N_EXPERTS = N_GROUPS * EXPERTS_PER_GROUP
D_EXPERT = 512
PLE_DIM = 256
EPS = 1e-6

LANES = 128
MLSTM_CHUNK = 256
EXPERT_ROWS = 256
VMEM_LIMIT = 56 * 1024 * 1024

NT_DIMS = (((1,), (1,)), ((), ()))
TN_DIMS = (((0,), (0,)), ((), ()))


def _params(*sem):
    return pltpu.CompilerParams(dimension_semantics=sem, vmem_limit_bytes=VMEM_LIMIT)


def _rms(x, g):
    r = lax.rsqrt(jnp.mean(x * x, axis=-1, keepdims=True) + EPS)
    return (x * r) * g


def _log_sigmoid(x):
    return jnp.minimum(x, 0.0) - jnp.log1p(jnp.exp(-jnp.abs(x)))


def _inproj_kernel(x_ref, g_ref, w_ref, z_ref, hn_scr, *, precision):
    @pl.when(pl.program_id(1) == 0)
    def _():
        hn_scr[...] = _rms(x_ref[...], g_ref[...]).astype(hn_scr.dtype)

    z_ref[...] = jnp.dot(hn_scr[...], w_ref[...], precision=precision, preferred_element_type=F32)


def _inproj(x, g_mix, w, tm, tn, precision=None):
    n = x.shape[0]
    width = w.shape[1]
    return pl.pallas_call(
        functools.partial(_inproj_kernel, precision=precision),
        grid=(n // tm, pl.cdiv(width, tn)),
        in_specs=[
            pl.BlockSpec((tm, D_MODEL), lambda i, j: (i, 0)),
            pl.BlockSpec((1, D_MODEL), lambda i, j: (0, 0)),
            pl.BlockSpec((D_MODEL, tn), lambda i, j: (0, j)),
        ],
        out_specs=pl.BlockSpec((tm, tn), lambda i, j: (i, j)),
        out_shape=jax.ShapeDtypeStruct((n, width), F32),
        scratch_shapes=[pltpu.VMEM((tm, D_MODEL), w.dtype)],
        compiler_params=_params("parallel", "arbitrary"),
        name="inproj",
    )(x, g_mix, w)


def _mlstm_out(h, gmh, mo):
    hn = h * lax.rsqrt(jnp.mean(h * h, axis=-1, keepdims=True) + EPS)
    return (hn * gmh) * jax.nn.sigmoid(mo)


def _gates(g_raw, bias):
    g = g_raw + bias
    lane = lax.broadcasted_iota(jnp.int32, g.shape, g.ndim - 1)
    return jnp.where(lane < M_HEADS, g, _log_sigmoid(g))


def _mlstm_kernel(q_ref, k_ref, v_ref, mo_ref, g_ref, bias_ref, gmh_ref,
                  o_ref, c_out, n_out, m_out, c_scr, n_scr, m_scr, *, chunk, n_chunks):
    c_idx = pl.program_id(1)

    @pl.when(c_idx == 0)
    def _():
        c_scr[...] = jnp.zeros_like(c_scr)
        n_scr[...] = jnp.zeros_like(n_scr)
        m_scr[...] = jnp.zeros_like(m_scr)

    L = chunk
    row = lax.broadcasted_iota(jnp.int32, (L, L), 0)
    col = lax.broadcasted_iota(jnp.int32, (L, L), 1)
    causal = row >= col
    ltri = causal.astype(F32)
    utri = (row <= col).astype(F32)
    g_c = _gates(g_ref[...], bias_ref[...])
    f_c = jnp.dot(ltri, g_c, precision=HI, preferred_element_type=F32)
    g_r = g_c.T[0:2 * M_HEADS, :]
    f_r = jnp.dot(g_r, utri, precision=HI, preferred_element_type=F32)

    for h in range(M_HEADS):
        hs = slice(h * M_DK, (h + 1) * M_DK)
        hf = M_HEADS + h
        a_row = g_r[h:h + 1, :] - f_r[hf:hf + 1, :]
        f_col = f_c[:, hf:hf + 1]
        a_col = g_c[:, h:h + 1] - f_col
        m0 = m_scr[h:h + 1, 0:1]
        a_max = jnp.max(jnp.where(causal, a_row, -jnp.inf), axis=1, keepdims=True)
        m_t = f_col + jnp.maximum(m0, a_max)
        dec0 = jnp.exp(f_col + m0 - m_t)
        d = jnp.exp(jnp.where(causal, (f_col - m_t) + a_row, -jnp.inf))
        q = q_ref[:, hs]
        k = k_ref[:, hs] * (M_DK ** -0.5)
        qb = q.astype(BF16)
        kb = k.astype(BF16)
        vb = v_ref[:, hs].astype(BF16)
        w = lax.dot_general(qb, kb, NT_DIMS, preferred_element_type=F32) * d
        c0 = c_scr[h]
        n0 = n_scr[h:h + 1, :]
        num = jnp.dot(w.astype(BF16), vb, preferred_element_type=F32) + dec0 * lax.dot_general(
            qb, c0.astype(BF16), NT_DIMS, preferred_element_type=F32)
        den = jnp.sum(w, axis=1, keepdims=True) + dec0 * jnp.sum(q * n0, axis=1, keepdims=True)
        hh = num / jnp.maximum(jnp.abs(den), jnp.exp(-m_t))
        m_end = m_t[L - 1:L, :]
        f_end = f_col[L - 1:L, :]
        w_end = jnp.exp(f_end + a_col - m_end)
        dec_end = jnp.exp(f_end + m0 - m_end)
        kw = k * w_end
        c_scr[h] = dec_end * c0 + lax.dot_general(vb, kw.astype(BF16), TN_DIMS, preferred_element_type=F32)
        n_scr[h:h + 1, :] = dec_end * n0 + jnp.sum(kw, axis=0, keepdims=True)
        m_scr[h:h + 1, :] = jnp.broadcast_to(m_end, (1, LANES))
        o_ref[:, hs] = _mlstm_out(hh, gmh_ref[:, hs], mo_ref[:, hs]).astype(o_ref.dtype)

    @pl.when(c_idx == n_chunks - 1)
    def _():
        c_out[0] = c_scr[...]
        n_out[0] = n_scr[...]
        m_out[0] = m_scr[...]


def _mlstm_prompt(z, gate_bias, g_mh, batch, seq):
    L = MLSTM_CHUNK
    nc = seq // L
    rows = lambda b, c: b * nc + c
    gate_blk = GATE_COL // LANES
    return pl.pallas_call(
        functools.partial(_mlstm_kernel, chunk=L, n_chunks=nc),
        grid=(batch, nc),
        in_specs=[
            pl.BlockSpec((L, M_W), lambda b, c: (rows(b, c), 0)),
            pl.BlockSpec((L, M_W), lambda b, c: (rows(b, c), 1)),
            pl.BlockSpec((L, M_W), lambda b, c: (rows(b, c), 2)),
            pl.BlockSpec((L, M_W), lambda b, c: (rows(b, c), 3)),
            pl.BlockSpec((L, LANES), lambda b, c: (rows(b, c), gate_blk)),
            pl.BlockSpec((1, LANES), lambda b, c: (0, 0)),
            pl.BlockSpec((1, M_W), lambda b, c: (0, 0)),
        ],
        out_specs=[
            pl.BlockSpec((L, M_W), lambda b, c: (rows(b, c), 0)),
            pl.BlockSpec((1, M_HEADS, M_DV, M_DK), lambda b, c: (b, 0, 0, 0)),
            pl.BlockSpec((1, 8, M_DK), lambda b, c: (b, 0, 0)),
            pl.BlockSpec((1, 8, LANES), lambda b, c: (b, 0, 0)),
        ],
        out_shape=[
            jax.ShapeDtypeStruct((batch * seq, M_W), BF16),
            jax.ShapeDtypeStruct((batch, M_HEADS, M_DV, M_DK), F32),
            jax.ShapeDtypeStruct((batch, 8, M_DK), F32),
            jax.ShapeDtypeStruct((batch, 8, LANES), F32),
        ],
        scratch_shapes=[
            pltpu.VMEM((M_HEADS, M_DV, M_DK), F32),
            pltpu.VMEM((8, M_DK), F32),
            pltpu.VMEM((8, LANES), F32),
        ],
        compiler_params=_params("parallel", "arbitrary"),
        name="mlstm_prompt",
    )(z, z, z, z, z, gate_bias, g_mh)


def _mlstm_step_kernel(z_ref, bias_ref, c_ref, n_ref, m_ref, gmh_ref, o_ref, c_out, n_out, m_out):
    z = z_ref[0]
    gates = _gates(z[:, GATE_OFF:ATT_OFF], bias_ref[...])
    m_in = m_ref[0]
    row = lax.broadcasted_iota(jnp.int32, (M_DV, M_DV), 0)
    col = lax.broadcasted_iota(jnp.int32, (M_DV, M_DV), 1)
    eye = row == col
    lane = lax.broadcasted_iota(jnp.int32, (1, LANES), 1)
    m_new = jnp.zeros((1, LANES), F32)
    for h in range(M_HEADS):
        hs = slice(h * M_DK, (h + 1) * M_DK)
        q = z[:, h * M_DK:(h + 1) * M_DK]
        k = z[:, M_W + h * M_DK:M_W + (h + 1) * M_DK] * (M_DK ** -0.5)
        v = z[:, 2 * M_W + h * M_DV:2 * M_W + (h + 1) * M_DV]
        mo = z[:, 3 * M_W + h * M_DV:3 * M_W + (h + 1) * M_DV]
        ip = gates[:, h:h + 1]
        lf = gates[:, M_HEADS + h:M_HEADS + h + 1]
        m0 = m_in[:, h:h + 1]
        m_t = jnp.maximum(lf + m0, ip)
        dec0 = jnp.exp(lf + m0 - m_t)
        d = jnp.exp(ip - m_t)
        c0 = c_ref[0, h]
        n0 = n_ref[0, :, hs]
        w = jnp.sum(q * k, axis=1, keepdims=True) * d
        cq_col = jnp.sum(c0 * q, axis=1, keepdims=True)
        cq = jnp.sum(jnp.where(eye, cq_col, 0.0), axis=0, keepdims=True)
        num = w * v + dec0 * cq
        den = w + dec0 * jnp.sum(n0 * q, axis=1, keepdims=True)
        hh = num / jnp.maximum(jnp.abs(den), jnp.exp(-m_t))
        v_col = jnp.sum(jnp.where(eye, v, 0.0), axis=1, keepdims=True)
        c_out[0, h] = dec0 * c0 + (d * v_col) * k
        n_out[0, :, hs] = dec0 * n0 + d * k
        m_new = jnp.where(lane == h, m_t, m_new)
        o_ref[0, :, hs] = _mlstm_out(hh, gmh_ref[:, hs], mo).astype(o_ref.dtype)
    m_out[0] = m_new


def _mlstm_sample(z3, gate_bias, c0, n0, m0, g_mh):
    nb = z3.shape[0]
    tok = lambda b: (b, 0, 0)
    return pl.pallas_call(
        _mlstm_step_kernel,
        grid=(nb,),
        in_specs=[
            pl.BlockSpec((1, 1, z3.shape[2]), tok),
            pl.BlockSpec((1, 2 * M_HEADS), lambda b: (0, 0)),
            pl.BlockSpec((1, M_HEADS, M_DV, M_DK), lambda b: (b, 0, 0, 0)),
            pl.BlockSpec((1, 1, M_W), tok),
            pl.BlockSpec((1, 1, M_HEADS), tok),
            pl.BlockSpec((1, M_W), lambda b: (0, 0)),
        ],
        out_specs=[
            pl.BlockSpec((1, 1, M_W), tok),
            pl.BlockSpec((1, M_HEADS, M_DV, M_DK), lambda b: (b, 0, 0, 0)),
            pl.BlockSpec((1, 1, M_W), tok),
            pl.BlockSpec((1, 1, LANES), tok),
        ],
        out_shape=[
            jax.ShapeDtypeStruct((nb, 1, M_W), F32),
            jax.ShapeDtypeStruct((nb, M_HEADS, M_DV, M_DK), F32),
            jax.ShapeDtypeStruct((nb, 1, M_W), F32),
            jax.ShapeDtypeStruct((nb, 1, LANES), F32),
        ],
        compiler_params=_params("parallel"),
        name="mlstm_sample",
    )(z3, gate_bias, c0, n0, m0, g_mh)


_EVEN_KV = [h for h in range(A_HEADS) if (h // A_GROUP) % 2 == 0]
_ODD_KV = [h for h in range(A_HEADS) if (h // A_GROUP) % 2 == 1]
SWA_HEAD_ORDER = tuple(h for pair in zip(_EVEN_KV, _ODD_KV) for h in pair)
HALF = LANES // 2
KV_SPAN = 2 * WINDOW


def _swa_kernel(sink_ref, q_ref, kc_ref, kp_ref, vc_ref, vp_ref, o_ref, qx_scr, p_scr):
    j = pl.program_id(1)
    lo_half = lax.broadcasted_iota(jnp.int32, (WINDOW, LANES), 1) < HALF
    zero_tile = jnp.zeros((WINDOW, LANES), BF16)
    for slot, head in enumerate(SWA_HEAD_ORDER):
        kv = head // A_GROUP
        tile = q_ref[:, (slot // 2) * LANES:(slot // 2 + 1) * LANES]
        keep = lo_half if slot % 2 == 0 else jnp.logical_not(lo_half)
        qm = jnp.where(keep, tile, 0.0).astype(BF16)
        parts = [qm, zero_tile] if kv // 2 == 0 else [zero_tile, qm]
        qx_scr[slot * WINDOW:(slot + 1) * WINDOW, :] = jnp.concatenate(parts, axis=1)

    k2 = jnp.concatenate([kp_ref[...], kc_ref[...]], axis=0).astype(BF16)
    v2 = jnp.concatenate([vp_ref[...], vc_ref[...]], axis=0).astype(BF16)
    s_all = lax.dot_general(qx_scr[...], k2, NT_DIMS, preferred_element_type=F32) * (A_HEAD_DIM ** -0.5)

    t = lax.broadcasted_iota(jnp.int32, (WINDOW, KV_SPAN), 0)
    kk = lax.broadcasted_iota(jnp.int32, (WINDOW, KV_SPAN), 1)
    in_prev = jnp.logical_and(jnp.logical_and(kk < WINDOW, kk > t), j > 0)
    in_cur = jnp.logical_and(kk >= WINDOW, kk - WINDOW <= t)
    allowed = jnp.logical_or(in_prev, in_cur)
    for slot, head in enumerate(SWA_HEAD_ORDER):
        rows = slice(slot * WINDOW, (slot + 1) * WINDOW)
        s = jnp.where(allowed, s_all[rows, :], -jnp.inf)
        sink = sink_ref[0, head]
        mx = jnp.maximum(jnp.max(s, axis=1, keepdims=True), sink)
        e = jnp.exp(s - mx)
        den = jnp.sum(e, axis=1, keepdims=True) + jnp.exp(sink - mx)
        p_scr[rows, :] = (e * (1.0 / den)).astype(BF16)

    o_all = jnp.dot(p_scr[...], v2, preferred_element_type=F32)
    for c in range(A_HEADS // 2):
        tiles = []
        for slot in (2 * c, 2 * c + 1):
            kv = SWA_HEAD_ORDER[slot] // A_GROUP
            tiles.append(o_all[slot * WINDOW:(slot + 1) * WINDOW, (kv // 2) * LANES:(kv // 2 + 1) * LANES])
        o_ref[:, c * LANES:(c + 1) * LANES] = jnp.where(lo_half, tiles[0], tiles[1]).astype(o_ref.dtype)


def _swa_prompt(z, sinks, batch, seq):
    nb = seq // WINDOW
    q_blk = (4 * M_W) // A_Q_W
    k_blk = (4 * M_W + A_Q_W) // A_KV_W
    cur = lambda b, j: b * nb + j
    prev = lambda b, j: b * nb + jnp.maximum(j - 1, 0)
    return pl.pallas_call(
        _swa_kernel,
        grid=(batch, nb),
        in_specs=[
            pl.BlockSpec(memory_space=pltpu.SMEM),
            pl.BlockSpec((WINDOW, A_Q_W), lambda b, j: (cur(b, j), q_blk)),
            pl.BlockSpec((WINDOW, A_KV_W), lambda b, j: (cur(b, j), k_blk)),
            pl.BlockSpec((WINDOW, A_KV_W), lambda b, j: (prev(b, j), k_blk)),
            pl.BlockSpec((WINDOW, A_KV_W), lambda b, j: (cur(b, j), k_blk + 1)),
            pl.BlockSpec((WINDOW, A_KV_W), lambda b, j: (prev(b, j), k_blk + 1)),
        ],
        out_specs=pl.BlockSpec((WINDOW, A_Q_W), lambda b, j: (cur(b, j), 0)),
        out_shape=jax.ShapeDtypeStruct((batch * seq, A_Q_W), BF16),
        scratch_shapes=[pltpu.VMEM((A_HEADS * WINDOW, A_KV_W), BF16), pltpu.VMEM((A_HEADS * WINDOW, KV_SPAN), BF16)],
        compiler_params=_params("parallel", "arbitrary"),
        name="swa_prompt",
    )(sinks, z, z, z, z, z)


SWA_STEP_TOKENS = 4


def _swa_step_kernel(qx_ref, kn_ref, vn_ref, bk_ref, bv_ref, sink_ref, o_ref, ko_ref, vo_ref):
    in_window = lax.broadcasted_iota(jnp.int32, (1, WINDOW), 1) >= 1
    scale = A_HEAD_DIM ** -0.5
    sink = sink_ref[...]
    for t in range(qx_ref.shape[0]):
        qx = qx_ref[t]
        bk = bk_ref[t]
        bv = bv_ref[t]
        kn = kn_ref[t]
        vn = vn_ref[t]
        s = lax.dot_general(qx, bk, NT_DIMS, precision=HI, preferred_element_type=F32) * scale
        s = jnp.where(in_window, s, -jnp.inf)
        s_new = jnp.sum(qx * kn, axis=1, keepdims=True) * scale
        mx = jnp.maximum(jnp.maximum(jnp.max(s, axis=1, keepdims=True), s_new), sink)
        e = jnp.exp(s - mx)
        e_new = jnp.exp(s_new - mx)
        den = jnp.sum(e, axis=1, keepdims=True) + e_new + jnp.exp(sink - mx)
        o_ref[t] = jnp.dot(e / den, bv, precision=HI, preferred_element_type=F32) + (e_new / den) * vn
        ko_ref[t, 0:WINDOW - 1, :] = bk[1:WINDOW, :]
        ko_ref[t, WINDOW - 1:WINDOW, :] = kn
        vo_ref[t, 0:WINDOW - 1, :] = bv[1:WINDOW, :]
        vo_ref[t, WINDOW - 1:WINDOW, :] = vn


def _swa_sample(qx, kn3, vn3, buf_k, buf_v, sink_col):
    nb = qx.shape[0]
    nt = SWA_STEP_TOKENS
    tok = lambda b: (b, 0, 0)
    return pl.pallas_call(
        _swa_step_kernel,
        grid=(nb // nt,),
        in_specs=[
            pl.BlockSpec((nt, A_HEADS, A_KV_W), tok),
            pl.BlockSpec((nt, 1, A_KV_W), tok),
            pl.BlockSpec((nt, 1, A_KV_W), tok),
            pl.BlockSpec((nt, WINDOW, A_KV_W), tok),
            pl.BlockSpec((nt, WINDOW, A_KV_W), tok),
            pl.BlockSpec((A_HEADS, 1), lambda b: (0, 0)),
        ],
        out_specs=[
            pl.BlockSpec((nt, A_HEADS, A_KV_W), tok),
            pl.BlockSpec((nt, WINDOW, A_KV_W), tok),
            pl.BlockSpec((nt, WINDOW, A_KV_W), tok),
        ],
        out_shape=[
            jax.ShapeDtypeStruct((nb, A_HEADS, A_KV_W), F32),
            jax.ShapeDtypeStruct((nb, WINDOW, A_KV_W), F32),
            jax.ShapeDtypeStruct((nb, WINDOW, A_KV_W), F32),
        ],
        compiler_params=_params("parallel"),
        name="swa_sample",
    )(qx, kn3, vn3, buf_k, buf_v, sink_col)


def _outproj_kernel(x_ref, om_ref, oa_ref, wm_ref, wa_ref, y_ref, *, precision):
    y_ref[...] = (x_ref[...]
                  + jnp.dot(om_ref[...], wm_ref[...], precision=precision, preferred_element_type=F32)
                  + jnp.dot(oa_ref[...], wa_ref[...], precision=precision, preferred_element_type=F32))


def _outproj(x, o_m, o_a, w, tm, precision=None):
    n = x.shape[0]
    return pl.pallas_call(
        functools.partial(_outproj_kernel, precision=precision),
        grid=(n // tm,),
        in_specs=[
            pl.BlockSpec((tm, D_MODEL), lambda i: (i, 0)),
            pl.BlockSpec((tm, M_W), lambda i: (i, 0)),
            pl.BlockSpec((tm, A_Q_W), lambda i: (i, 0)),
            pl.BlockSpec((M_W, D_MODEL), lambda i: (0, 0)),
            pl.BlockSpec((A_Q_W, D_MODEL), lambda i: (1, 0)),
        ],
        out_specs=pl.BlockSpec((tm, D_MODEL), lambda i: (i, 0)),
        out_shape=jax.ShapeDtypeStruct((n, D_MODEL), F32),
        compiler_params=_params("parallel"),
        name="outproj",
    )(x, o_m, o_a, w, w)


GROUP_LANE0 = 0
EXPERT_LANE0 = N_GROUPS


def _route_kernel(x_ref, g_ref, wr_ref, br_ref, cin_ref, info_ref, rows_ref, cnt_ref, carry):
    @pl.when(pl.program_id(0) == 0)
    def _():
        carry[...] = cin_ref[...]

    tm = x_ref.shape[0]
    hn = _rms(x_ref[...], g_ref[...])
    logits = jnp.dot(hn, wr_ref[...], precision=HI, preferred_element_type=F32) + br_ref[...]
    lane = lax.broadcasted_iota(jnp.int32, (tm, LANES), 1)
    neg = -jnp.inf
    gl = jnp.where(lane < N_GROUPS, logits, neg)
    g_max = jnp.max(gl, axis=1, keepdims=True)
    g_idx = jnp.min(jnp.where(gl == g_max, lane, LANES), axis=1, keepdims=True)
    g_w = 1.0 / jnp.sum(jnp.exp(gl - g_max), axis=1, keepdims=True)
    lo = EXPERT_LANE0 + EXPERTS_PER_GROUP * g_idx
    el = jnp.where(jnp.logical_and(lane >= lo, lane < lo + EXPERTS_PER_GROUP), logits, neg)
    l1 = jnp.max(el, axis=1, keepdims=True)
    i1 = jnp.min(jnp.where(el == l1, lane, LANES), axis=1, keepdims=True)
    el2 = jnp.where(lane == i1, neg, el)
    l2 = jnp.max(el2, axis=1, keepdims=True)
    i2 = jnp.min(jnp.where(el2 == l2, lane, LANES), axis=1, keepdims=True)
    e21 = jnp.exp(l2 - l1)
    w1 = g_w * (1.0 / (1.0 + e21))
    w2 = g_w * (e21 / (1.0 + e21))
    hit1 = lane == i1
    hit2 = lane == i2
    onehot = jnp.logical_or(hit1, hit2).astype(F32)
    r = lax.broadcasted_iota(jnp.int32, (tm, tm), 0)
    c = lax.broadcasted_iota(jnp.int32, (tm, tm), 1)
    before = (c < r).astype(BF16)
    cum = jnp.dot(before, onehot.astype(BF16), preferred_element_type=F32) + carry[0:1, :]
    rank1 = jnp.sum(jnp.where(hit1, cum, 0.0), axis=1, keepdims=True)
    rank2 = jnp.sum(jnp.where(hit2, cum, 0.0), axis=1, keepdims=True)
    info = jnp.where(lane == 0, (i1 - EXPERT_LANE0).astype(F32), 0.0)
    info = jnp.where(lane == 1, (i2 - EXPERT_LANE0).astype(F32), info)
    info = jnp.where(lane == 2, w1, info)
    info = jnp.where(lane == 3, w2, info)
    info = jnp.where(lane == 4, rank1, info)
    info = jnp.where(lane == 5, rank2, info)
    info_ref[...] = info
    rows_ref[...] = info.T[0:8, :]
    total = carry[0:1, :] + jnp.sum(onehot, axis=0, keepdims=True)
    carry[...] = jnp.broadcast_to(total, carry.shape)
    cnt_ref[...] = jnp.broadcast_to(total, cnt_ref.shape)


def _route(x, g_ffn, w_r, b_r, count_in, tm):
    n = x.shape[0]
    const = lambda i: (0, 0)
    return pl.pallas_call(
        _route_kernel,
        grid=(n // tm,),
        in_specs=[
            pl.BlockSpec((tm, D_MODEL), lambda i: (i, 0)),
            pl.BlockSpec((1, D_MODEL), const),
            pl.BlockSpec((D_MODEL, LANES), const),
            pl.BlockSpec((1, LANES), const),
            pl.BlockSpec((8, LANES), const),
        ],
        out_specs=[
            pl.BlockSpec((tm, LANES), lambda i: (i, 0)),
            pl.BlockSpec((8, tm), lambda i: (0, i)),
            pl.BlockSpec((8, LANES), const),
        ],
        out_shape=[
            jax.ShapeDtypeStruct((n, LANES), F32),
            jax.ShapeDtypeStruct((8, n), F32),
            jax.ShapeDtypeStruct((8, LANES), F32),
        ],
        scratch_shapes=[pltpu.VMEM((8, LANES), F32)],
        compiler_params=_params("arbitrary"),
        name="route",
    )(x, g_ffn, w_r, b_r, count_in)


DMA_UNROLL = 8


def _scatter_rows(d0_ref, d1_ref, x_ref, pad_out, sem, tile):
    def row_copy(r, dst_row):
        return pltpu.make_async_copy(x_ref.at[pl.ds(r, 1), :], pad_out.at[pl.ds(dst_row, 1), :], sem)

    def start(r, carry):
        row_copy(r, d0_ref[0, 0, r]).start()
        row_copy(r, d1_ref[0, 0, r]).start()
        return carry

    def wait(r, carry):
        row_copy(0, 0).wait()
        row_copy(0, 0).wait()
        return carry

    lax.fori_loop(0, tile, start, 0, unroll=DMA_UNROLL)
    lax.fori_loop(0, tile, wait, 0, unroll=DMA_UNROLL)


def _dispatch_kernel(zero_ref, d0p_ref, d1p_ref, d0s_ref, d1s_ref, xp_ref, xs_ref, pad_out, zero_scr, sem, zsem,
                     *, tile, n_tiles):
    i = pl.program_id(0)

    @pl.when(i == 0)
    def _():
        zero_scr[...] = jnp.zeros_like(zero_scr)

        def zero_copy(e):
            row0 = pl.multiple_of(zero_ref[e], EXPERT_ROWS)
            return pltpu.make_async_copy(zero_scr, pad_out.at[pl.ds(row0, EXPERT_ROWS), :], zsem)

        for e in range(zero_ref.shape[0]):
            @pl.when(zero_ref[e] >= 0)
            def _():
                zero_copy(e).start()
        for e in range(zero_ref.shape[0]):
            @pl.when(zero_ref[e] >= 0)
            def _():
                zero_copy(e).wait()

    @pl.when(i < n_tiles)
    def _():
        _scatter_rows(d0p_ref, d1p_ref, xp_ref, pad_out, sem, tile)

    @pl.when(i == n_tiles)
    def _():
        _scatter_rows(d0s_ref, d1s_ref, xs_ref, pad_out, sem, xs_ref.shape[0])


def _dispatch(zero_rows, dest_p, dest_s, xp, xs, n_rows, tile):
    n_tiles = xp.shape[0] // tile
    n_s = xs.shape[0]
    last = n_tiles - 1
    tile_spec = pl.BlockSpec((1, 1, tile), lambda i: (jnp.minimum(i, last), 0, 0), memory_space=pltpu.SMEM)
    return pl.pallas_call(
        functools.partial(_dispatch_kernel, tile=tile, n_tiles=n_tiles),
        grid=(n_tiles + 1,),
        in_specs=[
            pl.BlockSpec(memory_space=pltpu.SMEM),
            tile_spec, tile_spec,
            pl.BlockSpec(memory_space=pltpu.SMEM),
            pl.BlockSpec(memory_space=pltpu.SMEM),
            pl.BlockSpec((tile, D_MODEL), lambda i: (jnp.minimum(i, last), 0)),
            pl.BlockSpec((n_s, D_MODEL), lambda i: (0, 0)),
        ],
        out_specs=pl.BlockSpec(memory_space=pl.ANY),
        out_shape=jax.ShapeDtypeStruct((n_rows, D_MODEL), F32),
        scratch_shapes=[pltpu.VMEM((EXPERT_ROWS, D_MODEL), F32), pltpu.SemaphoreType.DMA(()),
                        pltpu.SemaphoreType.DMA(())],
        compiler_params=_params("arbitrary"),
        name="dispatch",
    )(zero_rows, dest_p[0].reshape(n_tiles, 1, tile), dest_p[1].reshape(n_tiles, 1, tile),
      dest_s[0].reshape(1, 1, n_s), dest_s[1].reshape(1, 1, n_s), xp, xs)


def _expert_kernel(be_ref, nused_ref, x_ref, g_ref, wg_ref, wu_ref, wd_ref, y_ref, wg_s, wu_s, wd_s):
    i = pl.program_id(0)
    prev = be_ref[jnp.maximum(i - 1, 0)]
    active = i < nused_ref[0]

    @pl.when(jnp.logical_and(active, jnp.logical_or(i == 0, be_ref[i] != prev)))
    def _():
        wg_s[...] = wg_ref[...].astype(BF16)
        wu_s[...] = wu_ref[...].astype(BF16)
        wd_s[...] = wd_ref[...].astype(BF16)

    @pl.when(active)
    def _():
        hn = _rms(x_ref[...], g_ref[...]).astype(BF16)
        gate = jnp.dot(hn, wg_s[...], preferred_element_type=F32)
        up = jnp.dot(hn, wu_s[...], preferred_element_type=F32)
        act = (gate * jax.nn.sigmoid(gate)) * up
        y_ref[...] = jnp.dot(act.astype(BF16), wd_s[...], preferred_element_type=F32)

    @pl.when(jnp.logical_not(active))
    def _():
        y_ref[...] = jnp.zeros_like(y_ref)


def _experts(block_expert, n_used, x_pad, g_ffn, w_gate, w_up, w_down):
    n_rows = x_pad.shape[0]
    nb = n_rows // EXPERT_ROWS
    grid_spec = pltpu.PrefetchScalarGridSpec(
        num_scalar_prefetch=2,
        grid=(nb,),
        in_specs=[
            pl.BlockSpec((EXPERT_ROWS, D_MODEL), lambda i, be, nu: (i, 0)),
            pl.BlockSpec((1, D_MODEL), lambda i, be, nu: (0, 0)),
            pl.BlockSpec((None, None, D_MODEL, D_EXPERT), lambda i, be, nu: (0, be[i], 0, 0)),
            pl.BlockSpec((None, None, D_MODEL, D_EXPERT), lambda i, be, nu: (0, be[i], 0, 0)),
            pl.BlockSpec((None, None, D_EXPERT, D_MODEL), lambda i, be, nu: (0, be[i], 0, 0)),
        ],
        out_specs=pl.BlockSpec((EXPERT_ROWS, D_MODEL), lambda i, be, nu: (i, 0)),
        scratch_shapes=[
            pltpu.VMEM((D_MODEL, D_EXPERT), BF16),
            pltpu.VMEM((D_MODEL, D_EXPERT), BF16),
            pltpu.VMEM((D_EXPERT, D_MODEL), BF16),
        ],
    )
    return pl.pallas_call(
        _expert_kernel,
        grid_spec=grid_spec,
        out_shape=jax.ShapeDtypeStruct((n_rows, D_MODEL), F32),
        compiler_params=_params("arbitrary"),
        name="experts",
    )(block_expert, n_used, x_pad, g_ffn, w_gate, w_up, w_down)


def _final_kernel(d0_ref, d1_ref, x_ref, info_ref, p_ref, y_hbm, wpg_ref, wpp_ref, gple_ref, gfin_ref,
                  out_ref, ybuf, sem):
    tm = x_ref.shape[0]

    def row_copy(src_row, slot, r):
        return pltpu.make_async_copy(y_hbm.at[pl.ds(src_row, 1), :], ybuf.at[slot, pl.ds(r, 1), :], sem)

    def start(r, carry):
        row_copy(d0_ref[0, 0, r], 0, r).start()
        row_copy(d1_ref[0, 0, r], 1, r).start()
        return carry

    def wait(r, carry):
        row_copy(0, 0, 0).wait()
        row_copy(0, 1, 0).wait()
        return carry

    lax.fori_loop(0, tm, start, 0, unroll=DMA_UNROLL)
    lax.fori_loop(0, tm, wait, 0, unroll=DMA_UNROLL)

    info = info_ref[...]
    x2 = x_ref[...] + (ybuf[0] * info[:, 2:3] + ybuf[1] * info[:, 3:4])
    gate = jax.nn.sigmoid(jnp.dot(_rms(x2, gple_ref[...]).astype(BF16), wpg_ref[...], preferred_element_type=F32))
    proj = jnp.dot(p_ref[...].astype(BF16), wpp_ref[...], preferred_element_type=F32)
    out_ref[...] = _rms(x2 + gate * proj, gfin_ref[...])


def _final(dest, x1, info, p, y_pad, w_pg, w_pp, g_ple, g_final, tm):
    n = x1.shape[0]
    d0 = dest[0].reshape(n // tm, 1, tm)
    d1 = dest[1].reshape(n // tm, 1, tm)
    const = lambda i: (0, 0)
    smem = lambda: pl.BlockSpec((1, 1, tm), lambda i: (i, 0, 0), memory_space=pltpu.SMEM)
    return pl.pallas_call(
        _final_kernel,
        grid=(n // tm,),
        in_specs=[
            smem(), smem(),
            pl.BlockSpec((tm, D_MODEL), lambda i: (i, 0)),
            pl.BlockSpec((tm, LANES), lambda i: (i, 0)),
            pl.BlockSpec((tm, PLE_DIM), lambda i: (i, 0)),
            pl.BlockSpec(memory_space=pl.ANY),
            pl.BlockSpec((D_MODEL, D_MODEL), const),
            pl.BlockSpec((PLE_DIM, D_MODEL), const),
            pl.BlockSpec((1, D_MODEL), const),
            pl.BlockSpec((1, D_MODEL), const),
        ],
        out_specs=pl.BlockSpec((tm, D_MODEL), lambda i: (i, 0)),
        out_shape=jax.ShapeDtypeStruct((n, D_MODEL), F32),
        scratch_shapes=[pltpu.VMEM((2, tm, D_MODEL), F32), pltpu.SemaphoreType.DMA(())],
        compiler_params=_params("arbitrary"),
        name="final",
    )(d0, d1, x1, info, p, y_pad, w_pg, w_pp, g_ple, g_final)


def _pad_lanes(a):
    return jnp.pad(a, ((0, 0), (0, LANES - a.shape[1])))


def kernel(x_prompt, x_sample, p_prompt, p_sample, state_mlstm_C, state_mlstm_n, state_mlstm_m, state_swa_k, state_swa_v, g_mix, w_in, b_igate, b_fgate, g_mh, attn_sinks, w_out, g_ffn, w_router_g, b_router_g, w_router_e, b_router_e, w_gate, w_up, w_down, g_ple, w_ple_gate, w_ple_proj, g_final):
    batch, seq, d = x_prompt.shape
    dec = x_sample.shape[0]
    n_p = batch * seq
    xp = x_prompt.reshape(n_p, d)
    xs = x_sample.reshape(dec, d)

    w_in0 = w_in[0]
    head_order = jnp.array(SWA_HEAD_ORDER, jnp.int32)
    w_aq = w_in0[:, ATT_OFF:ATT_OFF + A_Q_W].reshape(d, A_HEADS, A_HEAD_DIM)[:, head_order].reshape(d, A_Q_W)
    w_cat = jnp.concatenate(
        [w_in0[:, :GATE_OFF], w_aq, w_in0[:, ATT_OFF + A_Q_W:], _pad_lanes(w_in0[:, GATE_OFF:ATT_OFF])],
        axis=1).astype(BF16)
    gate_bias8 = jnp.concatenate([b_igate, b_fgate], axis=1)
    gate_bias = _pad_lanes(gate_bias8)
    w_oa = w_out[0, M_W:].reshape(A_HEADS, A_HEAD_DIM, d)[head_order].reshape(A_Q_W, d)
    w_o = jnp.concatenate([w_out[0, :M_W], w_oa], axis=0).astype(BF16)
    kv_of_head = jnp.arange(A_HEADS, dtype=jnp.int32) // A_GROUP
    head_mask = (kv_of_head[:, None] == jnp.arange(A_KV_HEADS, dtype=jnp.int32)[None, :]).astype(F32)
    w_r = _pad_lanes(jnp.concatenate([w_router_g[0], w_router_e[0]], axis=1))
    b_r = _pad_lanes(jnp.concatenate([b_router_g, b_router_e], axis=1))
    w_pg = w_ple_gate[0].astype(BF16)
    w_pp = w_ple_proj[0].astype(BF16)
    sink_col = attn_sinks.reshape(A_HEADS, 1)

    zp = _inproj(xp, g_mix, w_cat, tm=1024, tn=640)
    om_p, c_p, n_p8, m_p8 = _mlstm_prompt(zp, gate_bias, g_mh, batch, seq)
    oa_p = _swa_prompt(zp, attn_sinks, batch, seq)
    x1p = _outproj(xp, om_p, oa_p, w_o, tm=512)
    k_off = 4 * M_W + A_Q_W
    v_off = k_off + A_KV_W

    zs = _inproj(xs, g_mix, w_in0, tm=dec, tn=512, precision=HI)
    om_s, c_s, n_s, m_s = _mlstm_sample(
        zs.reshape(dec, 1, zs.shape[1]), gate_bias8,
        state_mlstm_C[0], state_mlstm_n[0].reshape(dec, 1, M_W), state_mlstm_m[0].reshape(dec, 1, M_HEADS), g_mh)
    ks_off = ATT_OFF + A_Q_W
    vs_off = ks_off + A_KV_W
    q_heads = zs[:, ATT_OFF:ks_off].reshape(dec, A_HEADS, 1, A_HEAD_DIM)
    qx = (q_heads * head_mask[None, :, :, None]).reshape(dec, A_HEADS, A_KV_W)
    ox_s, k_s, v_s = _swa_sample(
        qx,
        zs[:, ks_off:vs_off].reshape(dec, 1, A_KV_W),
        zs[:, vs_off:].reshape(dec, 1, A_KV_W),
        state_swa_k[0].reshape(dec, WINDOW, A_KV_W), state_swa_v[0].reshape(dec, WINDOW, A_KV_W), sink_col)
    oa_s = jnp.sum(ox_s.reshape(dec, A_HEADS, A_KV_HEADS, A_HEAD_DIM) * head_mask[None, :, :, None], axis=2)
    x1s = _outproj(xs, om_s.reshape(dec, M_W), oa_s.reshape(dec, A_Q_W), w_out[0], tm=dec, precision=HI)

    info_p, rows_p, cnt_p = _route(x1p, g_ffn, w_r, b_r, jnp.zeros((8, LANES), F32), tm=512)
    info_s, rows_s, cnt = _route(x1s, g_ffn, w_r, b_r, cnt_p, tm=dec)
    counts = cnt[0, EXPERT_LANE0:EXPERT_LANE0 + N_EXPERTS].astype(jnp.int32)
    padded = ((counts + EXPERT_ROWS - 1) // EXPERT_ROWS) * EXPERT_ROWS
    pad_end = jnp.cumsum(padded)
    pad_start = pad_end - padded
    n_assign = 2 * (n_p + dec)
    n_blocks = -(-n_assign // EXPERT_ROWS) + N_EXPERTS
    block_row0 = jnp.arange(n_blocks, dtype=jnp.int32) * EXPERT_ROWS
    block_expert = jnp.minimum(
        jnp.sum((pad_end[None, :] <= block_row0[:, None]).astype(jnp.int32), axis=1), N_EXPERTS - 1)
    n_used = (pad_end[-1:] // EXPERT_ROWS).astype(jnp.int32)
    spare = n_used + jnp.arange(N_EXPERTS, dtype=jnp.int32)
    zero_rows = jnp.concatenate([
        jnp.where(padded > 0, pad_end - EXPERT_ROWS, -1),
        jnp.where(spare < n_blocks, spare * EXPERT_ROWS, -1)]).astype(jnp.int32)

    def dest_of(rows):
        e = rows[0:2].astype(jnp.int32)
        hit = e[:, :, None] == jnp.arange(N_EXPERTS, dtype=jnp.int32)
        return jnp.sum(jnp.where(hit, pad_start, 0), axis=2) + rows[4:6].astype(jnp.int32)

    dest_p = dest_of(rows_p)
    dest_s = dest_of(rows_s)

    x_pad = _dispatch(zero_rows, dest_p, dest_s, x1p, x1s, n_blocks * EXPERT_ROWS, tile=512)
    y_pad = _experts(block_expert, n_used, x_pad, g_ffn, w_gate, w_up, w_down)
    y_p = _final(dest_p, x1p, info_p, p_prompt[0].reshape(n_p, PLE_DIM), y_pad, w_pg, w_pp, g_ple, g_final.reshape(1, d), tm=256)
    y_s = _final(dest_s, x1s, info_s, p_sample[0].reshape(dec, PLE_DIM), y_pad, w_pg, w_pp, g_ple, g_final.reshape(1, d), tm=dec)

    zp3 = zp.reshape(batch, seq, Z_W)
    swa_k_p = zp3[:, seq - WINDOW:, k_off:v_off].reshape(1, batch, WINDOW, A_KV_HEADS, A_HEAD_DIM)
    swa_v_p = zp3[:, seq - WINDOW:, v_off:GATE_COL].reshape(1, batch, WINDOW, A_KV_HEADS, A_HEAD_DIM)
    return (
        y_p.reshape(batch, seq, d),
        y_s.reshape(dec, 1, d),
        c_p[None],
        n_p8[None, :, :M_HEADS, :],
        m_p8[None, :, :M_HEADS, 0],
        swa_k_p,
        swa_v_p,
        c_s[None],
        n_s.reshape(1, dec, M_HEADS, M_DK),
        m_s[None, :, 0, :M_HEADS],
        k_s.reshape(1, dec, WINDOW, A_KV_HEADS, A_HEAD_DIM),
        v_s.reshape(1, dec, WINDOW, A_KV_HEADS, A_HEAD_DIM),
    )
```

```python
import functools

import jax
import jax.numpy as jnp
from jax import lax
from jax.experimental import pallas as pl
from jax.experimental.pallas import tpu as pltpu

F32 = jnp.float32
BF16 = jnp.bfloat16
HI = lax.Precision.HIGHEST

D_MODEL = 2048
M_HEADS = 4
M_DK = 256
M_DV = 256
A_HEADS = 16
A_KV_HEADS = 4
A_HEAD_DIM = 64
A_GROUP = A_HEADS // A_KV_HEADS
WINDOW = 128
M_W = M_HEADS * M_DK
A_Q_W = A_HEADS * A_HEAD_DIM
A_KV_W = A_KV_HEADS * A_HEAD_DIM
GATE_OFF = 4 * M_W
ATT_OFF = GATE_OFF + 2 * M_HEADS
ZB_W = 3 * M_W + A_Q_W
ZF_MO, ZF_K, ZF_V, ZF_GATE = 0, M_W, M_W + A_KV_W, M_W + 2 * A_KV_W
ZF_W = 2 * M_W
N_GROUPS = 4
EXPERTS_PER_GROUP = 8
N_EXPERTS = N_GROUPS * EXPERTS_PER_GROUP
D_EXPERT = 512
PLE_DIM = 256
EPS = 1e-6

LANES = 128
MLSTM_CHUNK = 256
EXPERT_ROWS = 256
VMEM_LIMIT = 56 * 1024 * 1024

NT_DIMS = (((1,), (1,)), ((), ()))
TN_DIMS = (((0,), (0,)), ((), ()))


def _params(*sem):
    return pltpu.CompilerParams(dimension_semantics=sem, vmem_limit_bytes=VMEM_LIMIT)


def _rms(x, g):
    r = lax.rsqrt(jnp.mean(x * x, axis=-1, keepdims=True) + EPS)
    return (x * r) * g


def _log_sigmoid(x):
    return jnp.minimum(x, 0.0) - jnp.log1p(jnp.exp(-jnp.abs(x)))


def _inproj_kernel(x_ref, g_ref, w_ref, z_ref, hn_scr, *, precision):
    @pl.when(pl.program_id(1) == 0)
    def _():
        hn_scr[...] = _rms(x_ref[...], g_ref[...]).astype(hn_scr.dtype)

    z_ref[...] = jnp.dot(hn_scr[...], w_ref[...], precision=precision, preferred_element_type=F32)


def _inproj(x, g_mix, w, tm, tn, precision=None):
    n = x.shape[0]
    width = w.shape[1]
    return pl.pallas_call(
        functools.partial(_inproj_kernel, precision=precision),
        grid=(n // tm, pl.cdiv(width, tn)),
        in_specs=[
            pl.BlockSpec((tm, D_MODEL), lambda i, j: (i, 0)),
            pl.BlockSpec((1, D_MODEL), lambda i, j: (0, 0)),
            pl.BlockSpec((D_MODEL, tn), lambda i, j: (0, j)),
        ],
        out_specs=pl.BlockSpec((tm, tn), lambda i, j: (i, j)),
        out_shape=jax.ShapeDtypeStruct((n, width), F32),
        scratch_shapes=[pltpu.VMEM((tm, D_MODEL), w.dtype)],
        compiler_params=_params("parallel", "arbitrary"),
        name="inproj",
    )(x, g_mix, w)


def _inproj_split_kernel(x_ref, g_ref, w_ref, zb_ref, zf_ref, hn_scr, *, n_bf16_blocks):
    j = pl.program_id(1)

    @pl.when(j == 0)
    def _():
        hn_scr[...] = _rms(x_ref[...], g_ref[...]).astype(BF16)

    acc = jnp.dot(hn_scr[...], w_ref[...], preferred_element_type=F32)

    @pl.when(j < n_bf16_blocks)
    def _():
        zb_ref[...] = acc.astype(BF16)

    @pl.when(j >= n_bf16_blocks)
    def _():
        zf_ref[...] = acc


def _inproj_split(x, g_mix, w, tm, tn):
    n = x.shape[0]
    nb = ZB_W // tn
    nf = ZF_W // tn
    return pl.pallas_call(
        functools.partial(_inproj_split_kernel, n_bf16_blocks=nb),
        grid=(n // tm, nb + nf),
        in_specs=[
            pl.BlockSpec((tm, D_MODEL), lambda i, j: (i, 0)),
            pl.BlockSpec((1, D_MODEL), lambda i, j: (0, 0)),
            pl.BlockSpec((D_MODEL, tn), lambda i, j: (0, j)),
        ],
        out_specs=[
            pl.BlockSpec((tm, tn), lambda i, j: (i, jnp.minimum(j, nb - 1))),
            pl.BlockSpec((tm, tn), lambda i, j: (i, jnp.maximum(j - nb, 0))),
        ],
        out_shape=[
            jax.ShapeDtypeStruct((n, ZB_W), BF16),
            jax.ShapeDtypeStruct((n, ZF_W), F32),
        ],
        scratch_shapes=[pltpu.VMEM((tm, D_MODEL), BF16)],
        compiler_params=_params("parallel", "arbitrary"),
        name="inproj_prompt",
    )(x, g_mix, w)


def _mlstm_out(h, gmh, mo):
    hn = h * lax.rsqrt(jnp.mean(h * h, axis=-1, keepdims=True) + EPS)
    return (hn * gmh) * jax.nn.sigmoid(mo)


def _gates(g_raw, bias):
    g = g_raw + bias
    lane = lax.broadcasted_iota(jnp.int32, g.shape, g.ndim - 1)
    return jnp.where(lane < M_HEADS, g, _log_sigmoid(g))


def _mlstm_kernel(q_ref, k_ref, v_ref, mo_ref, g_ref, bias_ref, gmh_ref,
                  o_ref, c_out, n_out, m_out, c_scr, n_scr, m_scr, *, chunk, n_chunks):
    c_idx = pl.program_id(1)

    @pl.when(c_idx == 0)
    def _():
        c_scr[...] = jnp.zeros_like(c_scr)
        n_scr[...] = jnp.zeros_like(n_scr)
        m_scr[...] = jnp.zeros_like(m_scr)

    L = chunk
    row = lax.broadcasted_iota(jnp.int32, (L, L), 0)
    col = lax.broadcasted_iota(jnp.int32, (L, L), 1)
    causal = row >= col
    ltri = causal.astype(F32)
    utri = (row <= col).astype(F32)
    g_c = _gates(g_ref[...], bias_ref[...])
    f_c = jnp.dot(ltri, g_c, precision=HI, preferred_element_type=F32)
    g_r = g_c.T[0:2 * M_HEADS, :]
    f_r = jnp.dot(g_r, utri, precision=HI, preferred_element_type=F32)

    for h in range(M_HEADS):
        hs = slice(h * M_DK, (h + 1) * M_DK)
        hf = M_HEADS + h
        a_row = g_r[h:h + 1, :] - f_r[hf:hf + 1, :]
        f_col = f_c[:, hf:hf + 1]
        a_col = g_c[:, h:h + 1] - f_col
        m0 = m_scr[h:h + 1, 0:1]
        a_max = jnp.max(jnp.where(causal, a_row, -jnp.inf), axis=1, keepdims=True)
        m_t = f_col + jnp.maximum(m0, a_max)
        dec0 = jnp.exp(f_col + m0 - m_t)
        d = jnp.exp(jnp.where(causal, (f_col - m_t) + a_row, -jnp.inf))
        qb = q_ref[:, hs]
        kb = k_ref[:, hs] * (M_DK ** -0.5)
        vb = v_ref[:, hs]
        q = qb.astype(F32)
        k = kb.astype(F32)
        w = lax.dot_general(qb, kb, NT_DIMS, preferred_element_type=F32) * d
        c0 = c_scr[h]
        n0 = n_scr[h:h + 1, :]
        num = jnp.dot(w.astype(BF16), vb, preferred_element_type=F32) + dec0 * lax.dot_general(
            qb, c0.astype(BF16), NT_DIMS, preferred_element_type=F32)
        den = jnp.sum(w, axis=1, keepdims=True) + dec0 * jnp.sum(q * n0, axis=1, keepdims=True)
        hh = num / jnp.maximum(jnp.abs(den), jnp.exp(-m_t))
        m_end = m_t[L - 1:L, :]
        f_end = f_col[L - 1:L, :]
        w_end = jnp.exp(f_end + a_col - m_end)
        dec_end = jnp.exp(f_end + m0 - m_end)
        kw = k * w_end
        c_scr[h] = dec_end * c0 + lax.dot_general(vb, kw.astype(BF16), TN_DIMS, preferred_element_type=F32)
        n_scr[h:h + 1, :] = dec_end * n0 + jnp.sum(kw, axis=0, keepdims=True)
        m_scr[h:h + 1, :] = jnp.broadcast_to(m_end, (1, LANES))
        o_ref[:, hs] = _mlstm_out(hh, gmh_ref[:, hs], mo_ref[:, hs]).astype(o_ref.dtype)

    @pl.when(c_idx == n_chunks - 1)
    def _():
        c_out[0] = c_scr[...]
        n_out[0] = n_scr[...]
        m_out[0] = m_scr[...]


def _mlstm_prompt(zb, zf, gate_bias, g_mh, batch, seq):
    L = MLSTM_CHUNK
    nc = seq // L
    rows = lambda b, c: b * nc + c
    gate_blk = ZF_GATE // LANES
    return pl.pallas_call(
        functools.partial(_mlstm_kernel, chunk=L, n_chunks=nc),
        grid=(batch, nc),
        in_specs=[
            pl.BlockSpec((L, M_W), lambda b, c: (rows(b, c), 0)),
            pl.BlockSpec((L, M_W), lambda b, c: (rows(b, c), 1)),
            pl.BlockSpec((L, M_W), lambda b, c: (rows(b, c), 2)),
            pl.BlockSpec((L, M_W), lambda b, c: (rows(b, c), ZF_MO // M_W)),
            pl.BlockSpec((L, LANES), lambda b, c: (rows(b, c), gate_blk)),
            pl.BlockSpec((1, LANES), lambda b, c: (0, 0)),
            pl.BlockSpec((1, M_W), lambda b, c: (0, 0)),
        ],
        out_specs=[
            pl.BlockSpec((L, M_W), lambda b, c: (rows(b, c), 0)),
            pl.BlockSpec((1, M_HEADS, M_DV, M_DK), lambda b, c: (b, 0, 0, 0)),
            pl.BlockSpec((1, 8, M_DK), lambda b, c: (b, 0, 0)),
            pl.BlockSpec((1, 8, LANES), lambda b, c: (b, 0, 0)),
        ],
        out_shape=[
            jax.ShapeDtypeStruct((batch * seq, M_W), BF16),
            jax.ShapeDtypeStruct((batch, M_HEADS, M_DV, M_DK), F32),
            jax.ShapeDtypeStruct((batch, 8, M_DK), F32),
            jax.ShapeDtypeStruct((batch, 8, LANES), F32),
        ],
        scratch_shapes=[
            pltpu.VMEM((M_HEADS, M_DV, M_DK), F32),
            pltpu.VMEM((8, M_DK), F32),
            pltpu.VMEM((8, LANES), F32),
        ],
        compiler_params=_params("parallel", "arbitrary"),
        name="mlstm_prompt",
    )(zb, zb, zb, zf, zf, gate_bias, g_mh)


def _mlstm_step_kernel(z_ref, bias_ref, c_ref, n_ref, m_ref, gmh_ref, o_ref, c_out, n_out, m_out):
    z = z_ref[0]
    gates = _gates(z[:, GATE_OFF:ATT_OFF], bias_ref[...])
    m_in = m_ref[0]
    row = lax.broadcasted_iota(jnp.int32, (M_DV, M_DV), 0)
    col = lax.broadcasted_iota(jnp.int32, (M_DV, M_DV), 1)
    eye = row == col
    lane = lax.broadcasted_iota(jnp.int32, (1, LANES), 1)
    m_new = jnp.zeros((1, LANES), F32)
    for h in range(M_HEADS):
        hs = slice(h * M_DK, (h + 1) * M_DK)
        q = z[:, h * M_DK:(h + 1) * M_DK]
        k = z[:, M_W + h * M_DK:M_W + (h + 1) * M_DK] * (M_DK ** -0.5)
        v = z[:, 2 * M_W + h * M_DV:2 * M_W + (h + 1) * M_DV]
        mo = z[:, 3 * M_W + h * M_DV:3 * M_W + (h + 1) * M_DV]
        ip = gates[:, h:h + 1]
        lf = gates[:, M_HEADS + h:M_HEADS + h + 1]
        m0 = m_in[:, h:h + 1]
        m_t = jnp.maximum(lf + m0, ip)
        dec0 = jnp.exp(lf + m0 - m_t)
        d = jnp.exp(ip - m_t)
        c0 = c_ref[0, h]
        n0 = n_ref[0, :, hs]
        w = jnp.sum(q * k, axis=1, keepdims=True) * d
        cq_col = jnp.sum(c0 * q, axis=1, keepdims=True)
        cq = jnp.sum(jnp.where(eye, cq_col, 0.0), axis=0, keepdims=True)
        num = w * v + dec0 * cq
        den = w + dec0 * jnp.sum(n0 * q, axis=1, keepdims=True)
        hh = num / jnp.maximum(jnp.abs(den), jnp.exp(-m_t))
        v_col = jnp.sum(jnp.where(eye, v, 0.0), axis=1, keepdims=True)
        c_out[0, h] = dec0 * c0 + (d * v_col) * k
        n_out[0, :, hs] = dec0 * n0 + d * k
        m_new = jnp.where(lane == h, m_t, m_new)
        o_ref[0, :, hs] = _mlstm_out(hh, gmh_ref[:, hs], mo).astype(o_ref.dtype)
    m_out[0] = m_new


def _mlstm_sample(z3, gate_bias, c0, n0, m0, g_mh):
    nb = z3.shape[0]
    tok = lambda b: (b, 0, 0)
    return pl.pallas_call(
        _mlstm_step_kernel,
        grid=(nb,),
        in_specs=[
            pl.BlockSpec((1, 1, z3.shape[2]), tok),
            pl.BlockSpec((1, 2 * M_HEADS), lambda b: (0, 0)),
            pl.BlockSpec((1, M_HEADS, M_DV, M_DK), lambda b: (b, 0, 0, 0)),
            pl.BlockSpec((1, 1, M_W), tok),
            pl.BlockSpec((1, 1, M_HEADS), tok),
            pl.BlockSpec((1, M_W), lambda b: (0, 0)),
        ],
        out_specs=[
            pl.BlockSpec((1, 1, M_W), tok),
            pl.BlockSpec((1, M_HEADS, M_DV, M_DK), lambda b: (b, 0, 0, 0)),
            pl.BlockSpec((1, 1, M_W), tok),
            pl.BlockSpec((1, 1, LANES), tok),
        ],
        out_shape=[
            jax.ShapeDtypeStruct((nb, 1, M_W), F32),
            jax.ShapeDtypeStruct((nb, M_HEADS, M_DV, M_DK), F32),
            jax.ShapeDtypeStruct((nb, 1, M_W), F32),
            jax.ShapeDtypeStruct((nb, 1, LANES), F32),
        ],
        compiler_params=_params("parallel"),
        name="mlstm_sample",
    )(z3, gate_bias, c0, n0, m0, g_mh)


_EVEN_KV = [h for h in range(A_HEADS) if (h // A_GROUP) % 2 == 0]
_ODD_KV = [h for h in range(A_HEADS) if (h // A_GROUP) % 2 == 1]
SWA_HEAD_ORDER = tuple(h for pair in zip(_EVEN_KV, _ODD_KV) for h in pair)
HALF = LANES // 2
KV_SPAN = 2 * WINDOW


def _swa_kernel(sink_ref, q_ref, kc_ref, kp_ref, vc_ref, vp_ref, o_ref, qx_scr, p_scr):
    j = pl.program_id(1)
    lo_half = lax.broadcasted_iota(jnp.int32, (WINDOW, LANES), 1) < HALF
    zero_tile = jnp.zeros((WINDOW, LANES), BF16)
    for slot, head in enumerate(SWA_HEAD_ORDER):
        kv = head // A_GROUP
        tile = q_ref[:, (slot // 2) * LANES:(slot // 2 + 1) * LANES]
        keep = lo_half if slot % 2 == 0 else jnp.logical_not(lo_half)
        qm = jnp.where(keep, tile, jnp.zeros_like(tile))
        parts = [qm, zero_tile] if kv // 2 == 0 else [zero_tile, qm]
        qx_scr[slot * WINDOW:(slot + 1) * WINDOW, :] = jnp.concatenate(parts, axis=1)

    k2 = jnp.concatenate([kp_ref[...], kc_ref[...]], axis=0).astype(BF16)
    v2 = jnp.concatenate([vp_ref[...], vc_ref[...]], axis=0).astype(BF16)
    s_all = lax.dot_general(qx_scr[...], k2, NT_DIMS, preferred_element_type=F32) * (A_HEAD_DIM ** -0.5)

    t = lax.broadcasted_iota(jnp.int32, (WINDOW, KV_SPAN), 0)
    kk = lax.broadcasted_iota(jnp.int32, (WINDOW, KV_SPAN), 1)
    in_prev = jnp.logical_and(jnp.logical_and(kk < WINDOW, kk > t), j > 0)
    in_cur = jnp.logical_and(kk >= WINDOW, kk - WINDOW <= t)
    allowed = jnp.logical_or(in_prev, in_cur)
    for slot, head in enumerate(SWA_HEAD_ORDER):
        rows = slice(slot * WINDOW, (slot + 1) * WINDOW)
        s = jnp.where(allowed, s_all[rows, :], -jnp.inf)
        sink = sink_ref[0, head]
        mx = jnp.maximum(jnp.max(s, axis=1, keepdims=True), sink)
        e = jnp.exp(s - mx)
        den = jnp.sum(e, axis=1, keepdims=True) + jnp.exp(sink - mx)
        p_scr[rows, :] = (e * (1.0 / den)).astype(BF16)

    o_all = jnp.dot(p_scr[...], v2, preferred_element_type=F32)
    for c in range(A_HEADS // 2):
        tiles = []
        for slot in (2 * c, 2 * c + 1):
            kv = SWA_HEAD_ORDER[slot] // A_GROUP
            tiles.append(o_all[slot * WINDOW:(slot + 1) * WINDOW, (kv // 2) * LANES:(kv // 2 + 1) * LANES])
        o_ref[:, c * LANES:(c + 1) * LANES] = jnp.where(lo_half, tiles[0], tiles[1]).astype(o_ref.dtype)


def _swa_prompt(zb, zf, sinks, batch, seq):
    nb = seq // WINDOW
    q_blk = (3 * M_W) // A_Q_W
    k_blk = ZF_K // A_KV_W
    cur = lambda b, j: b * nb + j
    prev = lambda b, j: b * nb + jnp.maximum(j - 1, 0)
    return pl.pallas_call(
        _swa_kernel,
        grid=(batch, nb),
        in_specs=[
            pl.BlockSpec(memory_space=pltpu.SMEM),
            pl.BlockSpec((WINDOW, A_Q_W), lambda b, j: (cur(b, j), q_blk)),
            pl.BlockSpec((WINDOW, A_KV_W), lambda b, j: (cur(b, j), k_blk)),
            pl.BlockSpec((WINDOW, A_KV_W), lambda b, j: (prev(b, j), k_blk)),
            pl.BlockSpec((WINDOW, A_KV_W), lambda b, j: (cur(b, j), k_blk + 1)),
            pl.BlockSpec((WINDOW, A_KV_W), lambda b, j: (prev(b, j), k_blk + 1)),
        ],
        out_specs=pl.BlockSpec((WINDOW, A_Q_W), lambda b, j: (cur(b, j), 0)),
        out_shape=jax.ShapeDtypeStruct((batch * seq, A_Q_W), BF16),
        scratch_shapes=[pltpu.VMEM((A_HEADS * WINDOW, A_KV_W), BF16), pltpu.VMEM((A_HEADS * WINDOW, KV_SPAN), BF16)],
        compiler_params=_params("parallel", "arbitrary"),
        name="swa_prompt",
    )(sinks, zb, zf, zf, zf, zf)


SWA_STEP_TOKENS = 4


def _swa_step_kernel(qx_ref, kn_ref, vn_ref, bk_ref, bv_ref, sink_ref, o_ref, ko_ref, vo_ref):
    in_window = lax.broadcasted_iota(jnp.int32, (1, WINDOW), 1) >= 1
    scale = A_HEAD_DIM ** -0.5
    sink = sink_ref[...]
    for t in range(qx_ref.shape[0]):
        qx = qx_ref[t]
        bk = bk_ref[t]
        bv = bv_ref[t]
        kn = kn_ref[t]
        vn = vn_ref[t]
        s = lax.dot_general(qx, bk, NT_DIMS, precision=HI, preferred_element_type=F32) * scale
        s = jnp.where(in_window, s, -jnp.inf)
        s_new = jnp.sum(qx * kn, axis=1, keepdims=True) * scale
        mx = jnp.maximum(jnp.maximum(jnp.max(s, axis=1, keepdims=True), s_new), sink)
        e = jnp.exp(s - mx)
        e_new = jnp.exp(s_new - mx)
        den = jnp.sum(e, axis=1, keepdims=True) + e_new + jnp.exp(sink - mx)
        o_ref[t] = jnp.dot(e / den, bv, precision=HI, preferred_element_type=F32) + (e_new / den) * vn
        ko_ref[t, 0:WINDOW - 1, :] = bk[1:WINDOW, :]
        ko_ref[t, WINDOW - 1:WINDOW, :] = kn
        vo_ref[t, 0:WINDOW - 1, :] = bv[1:WINDOW, :]
        vo_ref[t, WINDOW - 1:WINDOW, :] = vn


def _swa_sample(qx, kn3, vn3, buf_k, buf_v, sink_col):
    nb = qx.shape[0]
    nt = SWA_STEP_TOKENS
    tok = lambda b: (b, 0, 0)
    return pl.pallas_call(
        _swa_step_kernel,
        grid=(nb // nt,),
        in_specs=[
            pl.BlockSpec((nt, A_HEADS, A_KV_W), tok),
            pl.BlockSpec((nt, 1, A_KV_W), tok),
            pl.BlockSpec((nt, 1, A_KV_W), tok),
            pl.BlockSpec((nt, WINDOW, A_KV_W), tok),
            pl.BlockSpec((nt, WINDOW, A_KV_W), tok),
            pl.BlockSpec((A_HEADS, 1), lambda b: (0, 0)),
        ],
        out_specs=[
            pl.BlockSpec((nt, A_HEADS, A_KV_W), tok),
            pl.BlockSpec((nt, WINDOW, A_KV_W), tok),
            pl.BlockSpec((nt, WINDOW, A_KV_W), tok),
        ],
        out_shape=[
            jax.ShapeDtypeStruct((nb, A_HEADS, A_KV_W), F32),
            jax.ShapeDtypeStruct((nb, WINDOW, A_KV_W), F32),
            jax.ShapeDtypeStruct((nb, WINDOW, A_KV_W), F32),
        ],
        compiler_params=_params("parallel"),
        name="swa_sample",
    )(qx, kn3, vn3, buf_k, buf_v, sink_col)


GROUP_LANE0 = 0
EXPERT_LANE0 = N_GROUPS


def _outproj_route_kernel(x_ref, om_ref, oa_ref, wm_ref, wa_ref, g_ref, wr_ref, br_ref, cin_ref,
                          y_ref, info_ref, rows_ref, cnt_ref, carry, *, precision):
    @pl.when(pl.program_id(0) == 0)
    def _():
        carry[...] = cin_ref[...]

    y = (x_ref[...]
         + jnp.dot(om_ref[...], wm_ref[...], precision=precision, preferred_element_type=F32)
         + jnp.dot(oa_ref[...], wa_ref[...], precision=precision, preferred_element_type=F32))
    y_ref[...] = y
    _route_tile(y, g_ref, wr_ref, br_ref, info_ref, rows_ref, cnt_ref, carry)


def _route_tile(x1, g_ref, wr_ref, br_ref, info_ref, rows_ref, cnt_ref, carry):
    tm = x1.shape[0]
    hn = _rms(x1, g_ref[...])
    logits = jnp.dot(hn, wr_ref[...], precision=HI, preferred_element_type=F32) + br_ref[...]
    lane = lax.broadcasted_iota(jnp.int32, (tm, LANES), 1)
    neg = -jnp.inf
    gl = jnp.where(lane < N_GROUPS, logits, neg)
    g_max = jnp.max(gl, axis=1, keepdims=True)
    g_idx = jnp.min(jnp.where(gl == g_max, lane, LANES), axis=1, keepdims=True)
    g_w = 1.0 / jnp.sum(jnp.exp(gl - g_max), axis=1, keepdims=True)
    lo = EXPERT_LANE0 + EXPERTS_PER_GROUP * g_idx
    el = jnp.where(jnp.logical_and(lane >= lo, lane < lo + EXPERTS_PER_GROUP), logits, neg)
    l1 = jnp.max(el, axis=1, keepdims=True)
    i1 = jnp.min(jnp.where(el == l1, lane, LANES), axis=1, keepdims=True)
    el2 = jnp.where(lane == i1, neg, el)
    l2 = jnp.max(el2, axis=1, keepdims=True)
    i2 = jnp.min(jnp.where(el2 == l2, lane, LANES), axis=1, keepdims=True)
    e21 = jnp.exp(l2 - l1)
    w1 = g_w * (1.0 / (1.0 + e21))
    w2 = g_w * (e21 / (1.0 + e21))
    hit1 = lane == i1
    hit2 = lane == i2
    onehot = jnp.logical_or(hit1, hit2).astype(F32)
    r = lax.broadcasted_iota(jnp.int32, (tm, tm), 0)
    c = lax.broadcasted_iota(jnp.int32, (tm, tm), 1)
    before = (c < r).astype(BF16)
    cum = jnp.dot(before, onehot.astype(BF16), preferred_element_type=F32) + carry[0:1, :]
    rank1 = jnp.sum(jnp.where(hit1, cum, 0.0), axis=1, keepdims=True)
    rank2 = jnp.sum(jnp.where(hit2, cum, 0.0), axis=1, keepdims=True)
    info = jnp.where(lane == 0, (i1 - EXPERT_LANE0).astype(F32), 0.0)
    info = jnp.where(lane == 1, (i2 - EXPERT_LANE0).astype(F32), info)
    info = jnp.where(lane == 2, w1, info)
    info = jnp.where(lane == 3, w2, info)
    info = jnp.where(lane == 4, rank1, info)
    info = jnp.where(lane == 5, rank2, info)
    info_ref[...] = info
    rows_ref[...] = info.T[0:8, :]
    total = carry[0:1, :] + jnp.sum(onehot, axis=0, keepdims=True)
    carry[...] = jnp.broadcast_to(total, carry.shape)
    cnt_ref[...] = jnp.broadcast_to(total, cnt_ref.shape)


def _outproj_route(x, o_m, o_a, w, g_ffn, w_r, b_r, count_in, tm, precision=None):
    n = x.shape[0]
    const = lambda i: (0, 0)
    return pl.pallas_call(
        functools.partial(_outproj_route_kernel, precision=precision),
        grid=(n // tm,),
        in_specs=[
            pl.BlockSpec((tm, D_MODEL), lambda i: (i, 0)),
            pl.BlockSpec((tm, M_W), lambda i: (i, 0)),
            pl.BlockSpec((tm, A_Q_W), lambda i: (i, 0)),
            pl.BlockSpec((M_W, D_MODEL), lambda i: (0, 0)),
            pl.BlockSpec((A_Q_W, D_MODEL), lambda i: (1, 0)),
            pl.BlockSpec((1, D_MODEL), const),
            pl.BlockSpec((D_MODEL, LANES), const),
            pl.BlockSpec((1, LANES), const),
            pl.BlockSpec((8, LANES), const),
        ],
        out_specs=[
            pl.BlockSpec((tm, D_MODEL), lambda i: (i, 0)),
            pl.BlockSpec((tm, LANES), lambda i: (i, 0)),
            pl.BlockSpec((8, tm), lambda i: (0, i)),
            pl.BlockSpec((8, LANES), const),
        ],
        out_shape=[
            jax.ShapeDtypeStruct((n, D_MODEL), F32),
            jax.ShapeDtypeStruct((n, LANES), F32),
            jax.ShapeDtypeStruct((8, n), F32),
            jax.ShapeDtypeStruct((8, LANES), F32),
        ],
        scratch_shapes=[pltpu.VMEM((8, LANES), F32)],
        compiler_params=_params("arbitrary"),
        name="outproj_route",
    )(x, o_m, o_a, w, w, g_ffn, w_r, b_r, count_in)


DMA_UNROLL = 8


def _scatter_rows(d0_ref, d1_ref, x_ref, pad_out, sem, tile):
    def row_copy(r, dst_row):
        return pltpu.make_async_copy(x_ref.at[pl.ds(r, 1), :], pad_out.at[pl.ds(dst_row, 1), :], sem)

    def start(r, carry):
        row_copy(r, d0_ref[0, 0, r]).start()
        row_copy(r, d1_ref[0, 0, r]).start()
        return carry

    def wait(r, carry):
        row_copy(0, 0).wait()
        row_copy(0, 0).wait()
        return carry

    lax.fori_loop(0, tile, start, 0, unroll=DMA_UNROLL)
    lax.fori_loop(0, tile, wait, 0, unroll=DMA_UNROLL)


def _dispatch_kernel(zero_ref, d0p_ref, d1p_ref, d0s_ref, d1s_ref, xp_ref, xs_ref, pad_out, zero_scr, sem, zsem,
                     *, tile, n_tiles):
    i = pl.program_id(0)

    @pl.when(i == 0)
    def _():
        zero_scr[...] = jnp.zeros_like(zero_scr)

        def zero_copy(e):
            row0 = pl.multiple_of(zero_ref[e], EXPERT_ROWS)
            return pltpu.make_async_copy(zero_scr, pad_out.at[pl.ds(row0, EXPERT_ROWS), :], zsem)

        for e in range(zero_ref.shape[0]):
            @pl.when(zero_ref[e] >= 0)
            def _():
                zero_copy(e).start()
        for e in range(zero_ref.shape[0]):
            @pl.when(zero_ref[e] >= 0)
            def _():
                zero_copy(e).wait()

    @pl.when(i < n_tiles)
    def _():
        _scatter_rows(d0p_ref, d1p_ref, xp_ref, pad_out, sem, tile)

    @pl.when(i == n_tiles)
    def _():
        _scatter_rows(d0s_ref, d1s_ref, xs_ref, pad_out, sem, xs_ref.shape[0])


def _dispatch(zero_rows, dest_p, dest_s, xp, xs, n_rows, tile):
    n_tiles = xp.shape[0] // tile
    n_s = xs.shape[0]
    last = n_tiles - 1
    tile_spec = pl.BlockSpec((1, 1, tile), lambda i: (jnp.minimum(i, last), 0, 0), memory_space=pltpu.SMEM)
    return pl.pallas_call(
        functools.partial(_dispatch_kernel, tile=tile, n_tiles=n_tiles),
        grid=(n_tiles + 1,),
        in_specs=[
            pl.BlockSpec(memory_space=pltpu.SMEM),
            tile_spec, tile_spec,
            pl.BlockSpec(memory_space=pltpu.SMEM),
            pl.BlockSpec(memory_space=pltpu.SMEM),
            pl.BlockSpec((tile, D_MODEL), lambda i: (jnp.minimum(i, last), 0)),
            pl.BlockSpec((n_s, D_MODEL), lambda i: (0, 0)),
        ],
        out_specs=pl.BlockSpec(memory_space=pl.ANY),
        out_shape=jax.ShapeDtypeStruct((n_rows, D_MODEL), F32),
        scratch_shapes=[pltpu.VMEM((EXPERT_ROWS, D_MODEL), F32), pltpu.SemaphoreType.DMA(()),
                        pltpu.SemaphoreType.DMA(())],
        compiler_params=_params("arbitrary"),
        name="dispatch",
    )(zero_rows, dest_p[0].reshape(n_tiles, 1, tile), dest_p[1].reshape(n_tiles, 1, tile),
      dest_s[0].reshape(1, 1, n_s), dest_s[1].reshape(1, 1, n_s), xp, xs)


def _expert_kernel(be_ref, nused_ref, x_ref, g_ref, wg_ref, wu_ref, wd_ref, y_ref, wg_s, wu_s, wd_s):
    i = pl.program_id(0)
    prev = be_ref[jnp.maximum(i - 1, 0)]
    active = i < nused_ref[0]

    @pl.when(jnp.logical_and(active, jnp.logical_or(i == 0, be_ref[i] != prev)))
    def _():
        wg_s[...] = wg_ref[...].astype(BF16)
        wu_s[...] = wu_ref[...].astype(BF16)
        wd_s[...] = wd_ref[...].astype(BF16)

    @pl.when(active)
    def _():
        hn = _rms(x_ref[...], g_ref[...]).astype(BF16)
        gate = jnp.dot(hn, wg_s[...], preferred_element_type=F32)
        up = jnp.dot(hn, wu_s[...], preferred_element_type=F32)
        act = (gate * jax.nn.sigmoid(gate)) * up
        y_ref[...] = jnp.dot(act.astype(BF16), wd_s[...], preferred_element_type=F32)

    @pl.when(jnp.logical_not(active))
    def _():
        y_ref[...] = jnp.zeros_like(y_ref)


def _experts(block_expert, n_used, x_pad, g_ffn, w_gate, w_up, w_down):
    n_rows = x_pad.shape[0]
    nb = n_rows // EXPERT_ROWS
    grid_spec = pltpu.PrefetchScalarGridSpec(
        num_scalar_prefetch=2,
        grid=(nb,),
        in_specs=[
            pl.BlockSpec((EXPERT_ROWS, D_MODEL), lambda i, be, nu: (i, 0)),
            pl.BlockSpec((1, D_MODEL), lambda i, be, nu: (0, 0)),
            pl.BlockSpec((None, None, D_MODEL, D_EXPERT), lambda i, be, nu: (0, be[i], 0, 0)),
            pl.BlockSpec((None, None, D_MODEL, D_EXPERT), lambda i, be, nu: (0, be[i], 0, 0)),
            pl.BlockSpec((None, None, D_EXPERT, D_MODEL), lambda i, be, nu: (0, be[i], 0, 0)),
        ],
        out_specs=pl.BlockSpec((EXPERT_ROWS, D_MODEL), lambda i, be, nu: (i, 0)),
        scratch_shapes=[
            pltpu.VMEM((D_MODEL, D_EXPERT), BF16),
            pltpu.VMEM((D_MODEL, D_EXPERT), BF16),
            pltpu.VMEM((D_EXPERT, D_MODEL), BF16),
        ],
    )
    return pl.pallas_call(
        _expert_kernel,
        grid_spec=grid_spec,
        out_shape=jax.ShapeDtypeStruct((n_rows, D_MODEL), F32),
        compiler_params=_params("arbitrary"),
        name="experts",
    )(block_expert, n_used, x_pad, g_ffn, w_gate, w_up, w_down)


def _final_kernel(d0_ref, d1_ref, d0n_ref, d1n_ref, x_ref, info_ref, p_ref, y_hbm, wpg_ref, wpp_ref, gple_ref,
                  gfin_ref, out_ref, ybuf, sem):
    tm = x_ref.shape[0]
    i = pl.program_id(0)
    buf = i % 2

    def row_copy(src_row, b, k, r):
        return pltpu.make_async_copy(y_hbm.at[pl.ds(src_row, 1), :], ybuf.at[b, k, pl.ds(r, 1), :], sem.at[b])

    def gather(da_ref, db_ref, b):
        def start(r, carry):
            row_copy(da_ref[0, 0, r], b, 0, r).start()
            row_copy(db_ref[0, 0, r], b, 1, r).start()
            return carry
        lax.fori_loop(0, tm, start, 0, unroll=DMA_UNROLL)

    @pl.when(i == 0)
    def _():
        gather(d0_ref, d1_ref, 0)

    @pl.when(i + 1 < pl.num_programs(0))
    def _():
        gather(d0n_ref, d1n_ref, 1 - buf)

    def wait(r, carry):
        row_copy(0, buf, 0, 0).wait()
        row_copy(0, buf, 1, 0).wait()
        return carry

    lax.fori_loop(0, tm, wait, 0, unroll=DMA_UNROLL)

    info = info_ref[...]
    x2 = x_ref[...] + (ybuf[buf, 0] * info[:, 2:3] + ybuf[buf, 1] * info[:, 3:4])
    gate = jax.nn.sigmoid(jnp.dot(_rms(x2, gple_ref[...]).astype(BF16), wpg_ref[...], preferred_element_type=F32))
    proj = jnp.dot(p_ref[...].astype(BF16), wpp_ref[...], preferred_element_type=F32)
    out_ref[...] = _rms(x2 + gate * proj, gfin_ref[...])


def _final(dest, x1, info, p, y_pad, w_pg, w_pp, g_ple, g_final, tm):
    n = x1.shape[0]
    d0 = dest[0].reshape(n // tm, 1, tm)
    d1 = dest[1].reshape(n // tm, 1, tm)
    const = lambda i: (0, 0)
    last = n // tm - 1
    cur = pl.BlockSpec((1, 1, tm), lambda i: (i, 0, 0), memory_space=pltpu.SMEM)
    nxt = pl.BlockSpec((1, 1, tm), lambda i: (jnp.minimum(i + 1, last), 0, 0), memory_space=pltpu.SMEM)
    return pl.pallas_call(
        _final_kernel,
        grid=(n // tm,),
        in_specs=[
            cur, cur, nxt, nxt,
            pl.BlockSpec((tm, D_MODEL), lambda i: (i, 0)),
            pl.BlockSpec((tm, LANES), lambda i: (i, 0)),
            pl.BlockSpec((tm, PLE_DIM), lambda i: (i, 0)),
            pl.BlockSpec(memory_space=pl.ANY),
            pl.BlockSpec((D_MODEL, D_MODEL), const),
            pl.BlockSpec((PLE_DIM, D_MODEL), const),
            pl.BlockSpec((1, D_MODEL), const),
            pl.BlockSpec((1, D_MODEL), const),
        ],
        out_specs=pl.BlockSpec((tm, D_MODEL), lambda i: (i, 0)),
        out_shape=jax.ShapeDtypeStruct((n, D_MODEL), F32),
        scratch_shapes=[pltpu.VMEM((2, 2, tm, D_MODEL), F32), pltpu.SemaphoreType.DMA((2,))],
        compiler_params=_params("arbitrary"),
        name="final",
    )(d0, d1, d0, d1, x1, info, p, y_pad, w_pg, w_pp, g_ple, g_final)


def _pad_lanes(a):
    return jnp.pad(a, ((0, 0), (0, LANES - a.shape[1])))


def kernel(x_prompt, x_sample, p_prompt, p_sample, state_mlstm_C, state_mlstm_n, state_mlstm_m, state_swa_k, state_swa_v, g_mix, w_in, b_igate, b_fgate, g_mh, attn_sinks, w_out, g_ffn, w_router_g, b_router_g, w_router_e, b_router_e, w_gate, w_up, w_down, g_ple, w_ple_gate, w_ple_proj, g_final):
    batch, seq, d = x_prompt.shape
    dec = x_sample.shape[0]
    n_p = batch * seq
    xp = x_prompt.reshape(n_p, d)
    xs = x_sample.reshape(dec, d)

    w_in0 = w_in[0]
    head_order = jnp.array(SWA_HEAD_ORDER, jnp.int32)
    w_aq = w_in0[:, ATT_OFF:ATT_OFF + A_Q_W].reshape(d, A_HEADS, A_HEAD_DIM)[:, head_order].reshape(d, A_Q_W)
    zf_pad = ZF_W - (ZF_GATE + LANES)
    w_cat = jnp.concatenate(
        [w_in0[:, :3 * M_W], w_aq,
         w_in0[:, 3 * M_W:GATE_OFF], w_in0[:, ATT_OFF + A_Q_W:],
         _pad_lanes(w_in0[:, GATE_OFF:ATT_OFF]), jnp.zeros((d, zf_pad), F32)],
        axis=1).astype(BF16)
    gate_bias8 = jnp.concatenate([b_igate, b_fgate], axis=1)
    gate_bias = _pad_lanes(gate_bias8)
    w_oa = w_out[0, M_W:].reshape(A_HEADS, A_HEAD_DIM, d)[head_order].reshape(A_Q_W, d)
    w_o = jnp.concatenate([w_out[0, :M_W], w_oa], axis=0).astype(BF16)
    kv_of_head = jnp.arange(A_HEADS, dtype=jnp.int32) // A_GROUP
    head_mask = (kv_of_head[:, None] == jnp.arange(A_KV_HEADS, dtype=jnp.int32)[None, :]).astype(F32)
    w_r = _pad_lanes(jnp.concatenate([w_router_g[0], w_router_e[0]], axis=1))
    b_r = _pad_lanes(jnp.concatenate([b_router_g, b_router_e], axis=1))
    w_pg = w_ple_gate[0].astype(BF16)
    w_pp = w_ple_proj[0].astype(BF16)
    sink_col = attn_sinks.reshape(A_HEADS, 1)

    zb, zf = _inproj_split(xp, g_mix, w_cat, tm=1024, tn=1024)
    om_p, c_p, n_p8, m_p8 = _mlstm_prompt(zb, zf, gate_bias, g_mh, batch, seq)
    oa_p = _swa_prompt(zb, zf, attn_sinks, batch, seq)
    x1p, info_p, rows_p, cnt_p = _outproj_route(
        xp, om_p, oa_p, w_o, g_ffn, w_r, b_r, jnp.zeros((8, LANES), F32), tm=512)

    zs = _inproj(xs, g_mix, w_in0, tm=dec, tn=512, precision=HI)
    om_s, c_s, n_s, m_s = _mlstm_sample(
        zs.reshape(dec, 1, zs.shape[1]), gate_bias8,
        state_mlstm_C[0], state_mlstm_n[0].reshape(dec, 1, M_W), state_mlstm_m[0].reshape(dec, 1, M_HEADS), g_mh)
    ks_off = ATT_OFF + A_Q_W
    vs_off = ks_off + A_KV_W
    q_heads = zs[:, ATT_OFF:ks_off].reshape(dec, A_HEADS, 1, A_HEAD_DIM)
    qx = (q_heads * head_mask[None, :, :, None]).reshape(dec, A_HEADS, A_KV_W)
    ox_s, k_s, v_s = _swa_sample(
        qx,
        zs[:, ks_off:vs_off].reshape(dec, 1, A_KV_W),
        zs[:, vs_off:].reshape(dec, 1, A_KV_W),
        state_swa_k[0].reshape(dec, WINDOW, A_KV_W), state_swa_v[0].reshape(dec, WINDOW, A_KV_W), sink_col)
    oa_s = jnp.sum(ox_s.reshape(dec, A_HEADS, A_KV_HEADS, A_HEAD_DIM) * head_mask[None, :, :, None], axis=2)
    x1s, info_s, rows_s, cnt = _outproj_route(
        xs, om_s.reshape(dec, M_W), oa_s.reshape(dec, A_Q_W), w_out[0], g_ffn, w_r, b_r, cnt_p, tm=dec, precision=HI)
    counts = cnt[0, EXPERT_LANE0:EXPERT_LANE0 + N_EXPERTS].astype(jnp.int32)
    padded = ((counts + EXPERT_ROWS - 1) // EXPERT_ROWS) * EXPERT_ROWS
    pad_end = jnp.cumsum(padded)
    pad_start = pad_end - padded
    n_assign = 2 * (n_p + dec)
    n_blocks = -(-n_assign // EXPERT_ROWS) + N_EXPERTS
    block_row0 = jnp.arange(n_blocks, dtype=jnp.int32) * EXPERT_ROWS
    block_expert = jnp.minimum(
        jnp.sum((pad_end[None, :] <= block_row0[:, None]).astype(jnp.int32), axis=1), N_EXPERTS - 1)
    n_used = (pad_end[-1:] // EXPERT_ROWS).astype(jnp.int32)
    spare = n_used + jnp.arange(N_EXPERTS, dtype=jnp.int32)
    zero_rows = jnp.concatenate([
        jnp.where(padded > 0, pad_end - EXPERT_ROWS, -1),
        jnp.where(spare < n_blocks, spare * EXPERT_ROWS, -1)]).astype(jnp.int32)

    def dest_of(rows):
        e = rows[0:2].astype(jnp.int32)
        hit = e[:, :, None] == jnp.arange(N_EXPERTS, dtype=jnp.int32)
        return jnp.sum(jnp.where(hit, pad_start, 0), axis=2) + rows[4:6].astype(jnp.int32)

    dest_p = dest_of(rows_p)
    dest_s = dest_of(rows_s)

    x_pad = _dispatch(zero_rows, dest_p, dest_s, x1p, x1s, n_blocks * EXPERT_ROWS, tile=512)
    y_pad = _experts(block_expert, n_used, x_pad, g_ffn, w_gate, w_up, w_down)
    y_p = _final(dest_p, x1p, info_p, p_prompt[0].reshape(n_p, PLE_DIM), y_pad, w_pg, w_pp, g_ple, g_final.reshape(1, d), tm=256)
    y_s = _final(dest_s, x1s, info_s, p_sample[0].reshape(dec, PLE_DIM), y_pad, w_pg, w_pp, g_ple, g_final.reshape(1, d), tm=dec)

    zf3 = zf.reshape(batch, seq, ZF_W)
    swa_k_p = zf3[:, seq - WINDOW:, ZF_K:ZF_V].reshape(1, batch, WINDOW, A_KV_HEADS, A_HEAD_DIM)
    swa_v_p = zf3[:, seq - WINDOW:, ZF_V:ZF_GATE].reshape(1, batch, WINDOW, A_KV_HEADS, A_HEAD_DIM)
    return (
        y_p.reshape(batch, seq, d),
        y_s.reshape(dec, 1, d),
        c_p[None],
        n_p8[None, :, :M_HEADS, :],
        m_p8[None, :, :M_HEADS, 0],
        swa_k_p,
        swa_v_p,
        c_s[None],
        n_s.reshape(1, dec, M_HEADS, M_DK),
        m_s[None, :, 0, :M_HEADS],
        k_s.reshape(1, dec, WINDOW, A_KV_HEADS, A_HEAD_DIM),
        v_s.reshape(1, dec, WINDOW, A_KV_HEADS, A_HEAD_DIM),
    )
```

```python
import functools

import jax
import jax.numpy as jnp
from jax import lax
from jax.experimental import pallas as pl
from jax.experimental.pallas import tpu as pltpu

F32 = jnp.float32
BF16 = jnp.bfloat16
HI = lax.Precision.HIGHEST

D_MODEL = 2048
M_HEADS = 4
M_DK = 256
M_DV = 256
A_HEADS = 16
A_KV_HEADS = 4
A_HEAD_DIM = 64
A_GROUP = A_HEADS // A_KV_HEADS
WINDOW = 128
M_W = M_HEADS * M_DK
A_Q_W = A_HEADS * A_HEAD_DIM
A_KV_W = A_KV_HEADS * A_HEAD_DIM
GATE_OFF = 4 * M_W
ATT_OFF = GATE_OFF + 2 * M_HEADS
ZB_W = 3 * M_W + A_Q_W
ZF_MO, ZF_K, ZF_V, ZF_GATE = 0, M_W, M_W + A_KV_W, M_W + 2 * A_KV_W
ZF_W = 2 * M_W
N_GROUPS = 4
EXPERTS_PER_GROUP = 8
N_EXPERTS = N_GROUPS * EXPERTS_PER_GROUP
D_EXPERT = 512
PLE_DIM = 256
EPS = 1e-6

LANES = 128
MLSTM_CHUNK = 256
EXPERT_ROWS = 256
VMEM_LIMIT = 56 * 1024 * 1024
NT_DIMS = (((1,), (1,)), ((), ()))
TN_DIMS = (((0,), (0,)), ((), ()))


def _params(*sem):
    return pltpu.CompilerParams(dimension_semantics=sem, vmem_limit_bytes=VMEM_LIMIT)


def _rms(x, g):
    r = lax.rsqrt(jnp.mean(x * x, axis=-1, keepdims=True) + EPS)
    return (x * r) * g


def _log_sigmoid(x):
    return jnp.minimum(x, 0.0) - jnp.log1p(jnp.exp(-jnp.abs(x)))


def _inproj_kernel(x_ref, g_ref, w_ref, z_ref, hn_scr, *, precision):
    @pl.when(pl.program_id(1) == 0)
    def _():
        hn_scr[...] = _rms(x_ref[...], g_ref[...]).astype(hn_scr.dtype)

    z_ref[...] = jnp.dot(hn_scr[...], w_ref[...], precision=precision, preferred_element_type=F32)


def _inproj(x, g_mix, w, tm, tn, precision=None):
    n = x.shape[0]
    width = w.shape[1]
    return pl.pallas_call(
        functools.partial(_inproj_kernel, precision=precision),
        grid=(n // tm, pl.cdiv(width, tn)),
        in_specs=[
            pl.BlockSpec((tm, D_MODEL), lambda i, j: (i, 0)),
            pl.BlockSpec((1, D_MODEL), lambda i, j: (0, 0)),
            pl.BlockSpec((D_MODEL, tn), lambda i, j: (0, j)),
        ],
        out_specs=pl.BlockSpec((tm, tn), lambda i, j: (i, j)),
        out_shape=jax.ShapeDtypeStruct((n, width), F32),
        scratch_shapes=[pltpu.VMEM((tm, D_MODEL), w.dtype)],
        compiler_params=_params("parallel", "arbitrary"),
        name="inproj",
    )(x, g_mix, w)


def _inproj_split_kernel(x_ref, g_ref, w_ref, zb_ref, zf_ref, hn_scr, *, n_bf16_blocks):
    j = pl.program_id(1)

    @pl.when(j == 0)
    def _():
        hn_scr[...] = _rms(x_ref[...], g_ref[...]).astype(BF16)

    acc = jnp.dot(hn_scr[...], w_ref[...], preferred_element_type=F32)

    @pl.when(j < n_bf16_blocks)
    def _():
        zb_ref[...] = acc.astype(BF16)

    @pl.when(j >= n_bf16_blocks)
    def _():
        zf_ref[...] = acc


def _inproj_split(x, g_mix, w, tm, tn):
    n = x.shape[0]
    nb = ZB_W // tn
    nf = ZF_W // tn
    return pl.pallas_call(
        functools.partial(_inproj_split_kernel, n_bf16_blocks=nb),
        grid=(n // tm, nb + nf),
        in_specs=[
            pl.BlockSpec((tm, D_MODEL), lambda i, j: (i, 0)),
            pl.BlockSpec((1, D_MODEL), lambda i, j: (0, 0)),
            pl.BlockSpec((D_MODEL, tn), lambda i, j: (0, j)),
        ],
        out_specs=[
            pl.BlockSpec((tm, tn), lambda i, j: (i, jnp.minimum(j, nb - 1))),
            pl.BlockSpec((tm, tn), lambda i, j: (i, jnp.maximum(j - nb, 0))),
        ],
        out_shape=[
            jax.ShapeDtypeStruct((n, ZB_W), BF16),
            jax.ShapeDtypeStruct((n, ZF_W), F32),
        ],
        scratch_shapes=[pltpu.VMEM((tm, D_MODEL), BF16)],
        compiler_params=_params("parallel", "arbitrary"),
        name="inproj_prompt",
    )(x, g_mix, w)


def _mlstm_out(h, gmh, mo):
    hn = h * lax.rsqrt(jnp.mean(h * h, axis=-1, keepdims=True) + EPS)
    return (hn * gmh) * jax.nn.sigmoid(mo)


def _gates(g_raw, bias):
    g = g_raw + bias
    lane = lax.broadcasted_iota(jnp.int32, g.shape, g.ndim - 1)
    return jnp.where(lane < M_HEADS, g, _log_sigmoid(g))


def _mlstm_kernel(q_ref, k_ref, v_ref, mo_ref, g_ref, bias_ref, gmh_ref,
                  o_ref, c_out, n_out, m_out, c_scr, n_scr, m_scr, *, chunk, n_chunks):
    c_idx = pl.program_id(1)

    @pl.when(c_idx == 0)
    def _():
        c_scr[...] = jnp.zeros_like(c_scr)
        n_scr[...] = jnp.zeros_like(n_scr)
        m_scr[...] = jnp.zeros_like(m_scr)

    L = chunk
    row = lax.broadcasted_iota(jnp.int32, (L, L), 0)
    col = lax.broadcasted_iota(jnp.int32, (L, L), 1)
    causal = row >= col
    ltri = causal.astype(F32)
    utri = (row <= col).astype(F32)
    g_c = _gates(g_ref[...], bias_ref[...])
    f_c = jnp.dot(ltri, g_c, precision=HI, preferred_element_type=F32)
    g_r = g_c.T[0:2 * M_HEADS, :]
    f_r = jnp.dot(g_r, utri, precision=HI, preferred_element_type=F32)

    for h in range(M_HEADS):
        hs = slice(h * M_DK, (h + 1) * M_DK)
        hf = M_HEADS + h
        a_row = g_r[h:h + 1, :] - f_r[hf:hf + 1, :]
        f_col = f_c[:, hf:hf + 1]
        a_col = g_c[:, h:h + 1] - f_col
        m0 = m_scr[h:h + 1, 0:1]
        a_max = jnp.max(jnp.where(causal, a_row, -jnp.inf), axis=1, keepdims=True)
        m_t = f_col + jnp.maximum(m0, a_max)
        dec0 = jnp.exp(f_col + m0 - m_t)
        d = jnp.exp(jnp.where(causal, (f_col - m_t) + a_row, -jnp.inf))
        qb = q_ref[:, hs]
        kb = k_ref[:, hs] * (M_DK ** -0.5)
        vb = v_ref[:, hs]
        q = qb.astype(F32)
        k = kb.astype(F32)
        w = lax.dot_general(qb, kb, NT_DIMS, preferred_element_type=F32) * d
        c0 = c_scr[h]
        n0 = n_scr[h:h + 1, :]
        num = jnp.dot(w.astype(BF16), vb, preferred_element_type=F32) + dec0 * lax.dot_general(
            qb, c0.astype(BF16), NT_DIMS, preferred_element_type=F32)
        den = jnp.sum(w, axis=1, keepdims=True) + dec0 * jnp.sum(q * n0, axis=1, keepdims=True)
        hh = num / jnp.maximum(jnp.abs(den), jnp.exp(-m_t))
        m_end = m_t[L - 1:L, :]
        f_end = f_col[L - 1:L, :]
        w_end = jnp.exp(f_end + a_col - m_end)
        dec_end = jnp.exp(f_end + m0 - m_end)
        kw = k * w_end
        c_scr[h] = dec_end * c0 + lax.dot_general(vb, kw.astype(BF16), TN_DIMS, preferred_element_type=F32)
        n_scr[h:h + 1, :] = dec_end * n0 + jnp.sum(kw, axis=0, keepdims=True)
        m_scr[h:h + 1, :] = jnp.broadcast_to(m_end, (1, LANES))
        o_ref[:, hs] = _mlstm_out(hh, gmh_ref[:, hs], mo_ref[:, hs]).astype(o_ref.dtype)

    @pl.when(c_idx == n_chunks - 1)
    def _():
        c_out[0] = c_scr[...]
        n_out[0] = n_scr[...]
        m_out[0] = m_scr[...]


def _mlstm_prompt(zb, zf, gate_bias, g_mh, batch, seq):
    L = MLSTM_CHUNK
    nc = seq // L
    rows = lambda b, c: b * nc + c
    gate_blk = ZF_GATE // LANES
    return pl.pallas_call(
        functools.partial(_mlstm_kernel, chunk=L, n_chunks=nc),
        grid=(batch, nc),
        in_specs=[
            pl.BlockSpec((L, M_W), lambda b, c: (rows(b, c), 0)),
            pl.BlockSpec((L, M_W), lambda b, c: (rows(b, c), 1)),
            pl.BlockSpec((L, M_W), lambda b, c: (rows(b, c), 2)),
            pl.BlockSpec((L, M_W), lambda b, c: (rows(b, c), ZF_MO // M_W)),
            pl.BlockSpec((L, LANES), lambda b, c: (rows(b, c), gate_blk)),
            pl.BlockSpec((1, LANES), lambda b, c: (0, 0)),
            pl.BlockSpec((1, M_W), lambda b, c: (0, 0)),
        ],
        out_specs=[
            pl.BlockSpec((L, M_W), lambda b, c: (rows(b, c), 0)),
            pl.BlockSpec((1, M_HEADS, M_DV, M_DK), lambda b, c: (b, 0, 0, 0)),
            pl.BlockSpec((1, 8, M_DK), lambda b, c: (b, 0, 0)),
            pl.BlockSpec((1, 8, LANES), lambda b, c: (b, 0, 0)),
        ],
        out_shape=[
            jax.ShapeDtypeStruct((batch * seq, M_W), BF16),
            jax.ShapeDtypeStruct((batch, M_HEADS, M_DV, M_DK), F32),
            jax.ShapeDtypeStruct((batch, 8, M_DK), F32),
            jax.ShapeDtypeStruct((batch, 8, LANES), F32),
        ],
        scratch_shapes=[
            pltpu.VMEM((M_HEADS, M_DV, M_DK), F32),
            pltpu.VMEM((8, M_DK), F32),
            pltpu.VMEM((8, LANES), F32),
        ],
        compiler_params=_params("parallel", "arbitrary"),
        name="mlstm_prompt",
    )(zb, zb, zb, zf, zf, gate_bias, g_mh)


def _mlstm_step_kernel(z_ref, bias_ref, c_ref, n_ref, m_ref, gmh_ref, o_ref, c_out, n_out, m_out):
    z = z_ref[0]
    gates = _gates(z[:, GATE_OFF:ATT_OFF], bias_ref[...])
    m_in = m_ref[0]
    row = lax.broadcasted_iota(jnp.int32, (M_DV, M_DV), 0)
    col = lax.broadcasted_iota(jnp.int32, (M_DV, M_DV), 1)
    eye = row == col
    lane = lax.broadcasted_iota(jnp.int32, (1, LANES), 1)
    m_new = jnp.zeros((1, LANES), F32)
    for h in range(M_HEADS):
        hs = slice(h * M_DK, (h + 1) * M_DK)
        q = z[:, h * M_DK:(h + 1) * M_DK]
        k = z[:, M_W + h * M_DK:M_W + (h + 1) * M_DK] * (M_DK ** -0.5)
        v = z[:, 2 * M_W + h * M_DV:2 * M_W + (h + 1) * M_DV]
        mo = z[:, 3 * M_W + h * M_DV:3 * M_W + (h + 1) * M_DV]
        ip = gates[:, h:h + 1]
        lf = gates[:, M_HEADS + h:M_HEADS + h + 1]
        m0 = m_in[:, h:h + 1]
        m_t = jnp.maximum(lf + m0, ip)
        dec0 = jnp.exp(lf + m0 - m_t)
        d = jnp.exp(ip - m_t)
        c0 = c_ref[0, h]
        n0 = n_ref[0, :, hs]
        w = jnp.sum(q * k, axis=1, keepdims=True) * d
        cq_col = jnp.sum(c0 * q, axis=1, keepdims=True)
        cq = jnp.sum(jnp.where(eye, cq_col, 0.0), axis=0, keepdims=True)
        num = w * v + dec0 * cq
        den = w + dec0 * jnp.sum(n0 * q, axis=1, keepdims=True)
        hh = num / jnp.maximum(jnp.abs(den), jnp.exp(-m_t))
        v_col = jnp.sum(jnp.where(eye, v, 0.0), axis=1, keepdims=True)
        c_out[0, h] = dec0 * c0 + (d * v_col) * k
        n_out[0, :, hs] = dec0 * n0 + d * k
        m_new = jnp.where(lane == h, m_t, m_new)
        o_ref[0, :, hs] = _mlstm_out(hh, gmh_ref[:, hs], mo).astype(o_ref.dtype)
    m_out[0] = m_new


def _mlstm_sample(z3, gate_bias, c0, n0, m0, g_mh):
    nb = z3.shape[0]
    tok = lambda b: (b, 0, 0)
    return pl.pallas_call(
        _mlstm_step_kernel,
        grid=(nb,),
        in_specs=[
            pl.BlockSpec((1, 1, z3.shape[2]), tok),
            pl.BlockSpec((1, 2 * M_HEADS), lambda b: (0, 0)),
            pl.BlockSpec((1, M_HEADS, M_DV, M_DK), lambda b: (b, 0, 0, 0)),
            pl.BlockSpec((1, 1, M_W), tok),
            pl.BlockSpec((1, 1, M_HEADS), tok),
            pl.BlockSpec((1, M_W), lambda b: (0, 0)),
        ],
        out_specs=[
            pl.BlockSpec((1, 1, M_W), tok),
            pl.BlockSpec((1, M_HEADS, M_DV, M_DK), lambda b: (b, 0, 0, 0)),
            pl.BlockSpec((1, 1, M_W), tok),
            pl.BlockSpec((1, 1, LANES), tok),
        ],
        out_shape=[
            jax.ShapeDtypeStruct((nb, 1, M_W), F32),
            jax.ShapeDtypeStruct((nb, M_HEADS, M_DV, M_DK), F32),
            jax.ShapeDtypeStruct((nb, 1, M_W), F32),
            jax.ShapeDtypeStruct((nb, 1, LANES), F32),
        ],
        compiler_params=_params("parallel"),
        name="mlstm_sample",
    )(z3, gate_bias, c0, n0, m0, g_mh)


_EVEN_KV = [h for h in range(A_HEADS) if (h // A_GROUP) % 2 == 0]
_ODD_KV = [h for h in range(A_HEADS) if (h // A_GROUP) % 2 == 1]
SWA_HEAD_ORDER = tuple(h for pair in zip(_EVEN_KV, _ODD_KV) for h in pair)
HALF = LANES // 2
KV_SPAN = 2 * WINDOW


def _swa_kernel(sink_ref, q_ref, kc_ref, kp_ref, vc_ref, vp_ref, o_ref, qx_scr, p_scr):
    j = pl.program_id(1)
    lo_half = lax.broadcasted_iota(jnp.int32, (WINDOW, LANES), 1) < HALF
    zero_tile = jnp.zeros((WINDOW, LANES), BF16)
    for slot, head in enumerate(SWA_HEAD_ORDER):
        kv = head // A_GROUP
        tile = q_ref[:, (slot // 2) * LANES:(slot // 2 + 1) * LANES]
        keep = lo_half if slot % 2 == 0 else jnp.logical_not(lo_half)
        qm = jnp.where(keep, tile, jnp.zeros_like(tile))
        parts = [qm, zero_tile] if kv // 2 == 0 else [zero_tile, qm]
        qx_scr[slot * WINDOW:(slot + 1) * WINDOW, :] = jnp.concatenate(parts, axis=1)

    k2 = jnp.concatenate([kp_ref[...], kc_ref[...]], axis=0).astype(BF16)
    v2 = jnp.concatenate([vp_ref[...], vc_ref[...]], axis=0).astype(BF16)
    s_all = lax.dot_general(qx_scr[...], k2, NT_DIMS, preferred_element_type=F32) * (A_HEAD_DIM ** -0.5)

    t = lax.broadcasted_iota(jnp.int32, (WINDOW, KV_SPAN), 0)
    kk = lax.broadcasted_iota(jnp.int32, (WINDOW, KV_SPAN), 1)
    in_prev = jnp.logical_and(jnp.logical_and(kk < WINDOW, kk > t), j > 0)
    in_cur = jnp.logical_and(kk >= WINDOW, kk - WINDOW <= t)
    allowed = jnp.logical_or(in_prev, in_cur)
    for slot, head in enumerate(SWA_HEAD_ORDER):
        rows = slice(slot * WINDOW, (slot + 1) * WINDOW)
        s = jnp.where(allowed, s_all[rows, :], -jnp.inf)
        sink = sink_ref[0, head]
        mx = jnp.maximum(jnp.max(s, axis=1, keepdims=True), sink)
        e = jnp.exp(s - mx)
        den = jnp.sum(e, axis=1, keepdims=True) + jnp.exp(sink - mx)
        p_scr[rows, :] = (e * (1.0 / den)).astype(BF16)

    o_all = jnp.dot(p_scr[...], v2, preferred_element_type=F32)
    for c in range(A_HEADS // 2):
        tiles = []
        for slot in (2 * c, 2 * c + 1):
            kv = SWA_HEAD_ORDER[slot] // A_GROUP
            tiles.append(o_all[slot * WINDOW:(slot + 1) * WINDOW, (kv // 2) * LANES:(kv // 2 + 1) * LANES])
        o_ref[:, c * LANES:(c + 1) * LANES] = jnp.where(lo_half, tiles[0], tiles[1]).astype(o_ref.dtype)


def _swa_prompt(zb, zf, sinks, batch, seq):
    nb = seq // WINDOW
    q_blk = (3 * M_W) // A_Q_W
    k_blk = ZF_K // A_KV_W
    cur = lambda b, j: b * nb + j
    prev = lambda b, j: b * nb + jnp.maximum(j - 1, 0)
    return pl.pallas_call(
        _swa_kernel,
        grid=(batch, nb),
        in_specs=[
            pl.BlockSpec(memory_space=pltpu.SMEM),
            pl.BlockSpec((WINDOW, A_Q_W), lambda b, j: (cur(b, j), q_blk)),
            pl.BlockSpec((WINDOW, A_KV_W), lambda b, j: (cur(b, j), k_blk)),
            pl.BlockSpec((WINDOW, A_KV_W), lambda b, j: (prev(b, j), k_blk)),
            pl.BlockSpec((WINDOW, A_KV_W), lambda b, j: (cur(b, j), k_blk + 1)),
            pl.BlockSpec((WINDOW, A_KV_W), lambda b, j: (prev(b, j), k_blk + 1)),
        ],
        out_specs=pl.BlockSpec((WINDOW, A_Q_W), lambda b, j: (cur(b, j), 0)),
        out_shape=jax.ShapeDtypeStruct((batch * seq, A_Q_W), BF16),
        scratch_shapes=[pltpu.VMEM((A_HEADS * WINDOW, A_KV_W), BF16), pltpu.VMEM((A_HEADS * WINDOW, KV_SPAN), BF16)],
        compiler_params=_params("parallel", "arbitrary"),
        name="swa_prompt",
    )(sinks, zb, zf, zf, zf, zf)


SWA_STEP_TOKENS = 4


def _swa_step_kernel(qx_ref, kn_ref, vn_ref, bk_ref, bv_ref, sink_ref, o_ref, ko_ref, vo_ref):
    in_window = lax.broadcasted_iota(jnp.int32, (1, WINDOW), 1) >= 1
    scale = A_HEAD_DIM ** -0.5
    sink = sink_ref[...]
    for t in range(qx_ref.shape[0]):
        qx = qx_ref[t]
        bk = bk_ref[t]
        bv = bv_ref[t]
        kn = kn_ref[t]
        vn = vn_ref[t]
        s = lax.dot_general(qx, bk, NT_DIMS, precision=HI, preferred_element_type=F32) * scale
        s = jnp.where(in_window, s, -jnp.inf)
        s_new = jnp.sum(qx * kn, axis=1, keepdims=True) * scale
        mx = jnp.maximum(jnp.maximum(jnp.max(s, axis=1, keepdims=True), s_new), sink)
        e = jnp.exp(s - mx)
        e_new = jnp.exp(s_new - mx)
        den = jnp.sum(e, axis=1, keepdims=True) + e_new + jnp.exp(sink - mx)
        o_ref[t] = jnp.dot(e / den, bv, precision=HI, preferred_element_type=F32) + (e_new / den) * vn
        ko_ref[t, 0:WINDOW - 1, :] = bk[1:WINDOW, :]
        ko_ref[t, WINDOW - 1:WINDOW, :] = kn
        vo_ref[t, 0:WINDOW - 1, :] = bv[1:WINDOW, :]
        vo_ref[t, WINDOW - 1:WINDOW, :] = vn


def _swa_sample(qx, kn3, vn3, buf_k, buf_v, sink_col):
    nb = qx.shape[0]
    nt = SWA_STEP_TOKENS
    tok = lambda b: (b, 0, 0)
    return pl.pallas_call(
        _swa_step_kernel,
        grid=(nb // nt,),
        in_specs=[
            pl.BlockSpec((nt, A_HEADS, A_KV_W), tok),
            pl.BlockSpec((nt, 1, A_KV_W), tok),
            pl.BlockSpec((nt, 1, A_KV_W), tok),
            pl.BlockSpec((nt, WINDOW, A_KV_W), tok),
            pl.BlockSpec((nt, WINDOW, A_KV_W), tok),
            pl.BlockSpec((A_HEADS, 1), lambda b: (0, 0)),
        ],
        out_specs=[
            pl.BlockSpec((nt, A_HEADS, A_KV_W), tok),
            pl.BlockSpec((nt, WINDOW, A_KV_W), tok),
            pl.BlockSpec((nt, WINDOW, A_KV_W), tok),
        ],
        out_shape=[
            jax.ShapeDtypeStruct((nb, A_HEADS, A_KV_W), F32),
            jax.ShapeDtypeStruct((nb, WINDOW, A_KV_W), F32),
            jax.ShapeDtypeStruct((nb, WINDOW, A_KV_W), F32),
        ],
        compiler_params=_params("parallel"),
        name="swa_sample",
    )(qx, kn3, vn3, buf_k, buf_v, sink_col)


GROUP_LANE0 = 0
EXPERT_LANE0 = 8


def _outproj_kernel(x_ref, om_ref, oa_ref, wm_ref, wa_ref, y_ref, *, precision):
    y_ref[...] = (x_ref[...]
                  + jnp.dot(om_ref[...], wm_ref[...], precision=precision, preferred_element_type=F32)
                  + jnp.dot(oa_ref[...], wa_ref[...], precision=precision, preferred_element_type=F32))


def _outproj(x, o_m, o_a, w, tm, precision=None):
    n = x.shape[0]
    return pl.pallas_call(
        functools.partial(_outproj_kernel, precision=precision),
        grid=(n // tm,),
        in_specs=[
            pl.BlockSpec((tm, D_MODEL), lambda i: (i, 0)),
            pl.BlockSpec((tm, M_W), lambda i: (i, 0)),
            pl.BlockSpec((tm, A_Q_W), lambda i: (i, 0)),
            pl.BlockSpec((M_W, D_MODEL), lambda i: (0, 0)),
            pl.BlockSpec((A_Q_W, D_MODEL), lambda i: (1, 0)),
        ],
        out_specs=pl.BlockSpec((tm, D_MODEL), lambda i: (i, 0)),
        out_shape=jax.ShapeDtypeStruct((n, D_MODEL), F32),
        compiler_params=_params("parallel"),
        name="outproj",
    )(x, o_m, o_a, w, w)


def _route_kernel(x_ref, g_ref, wr_ref, br_ref, cin_ref, info_ref, rows_ref, cnt_ref, carry):
    @pl.when(pl.program_id(0) == 0)
    def _():
        carry[...] = cin_ref[...]

    tm = x_ref.shape[0]
    hn = _rms(x_ref[...], g_ref[...])
    w_r = wr_ref[...]
    hn_hi = hn.astype(BF16)
    hn_lo = (hn - hn_hi.astype(F32)).astype(BF16)
    w_hi = w_r.astype(BF16)
    w_lo = (w_r - w_hi.astype(F32)).astype(BF16)
    logits = (jnp.dot(hn_hi, w_hi, preferred_element_type=F32)
              + (jnp.dot(hn_lo, w_hi, preferred_element_type=F32)
                 + jnp.dot(hn_hi, w_lo, preferred_element_type=F32))) + br_ref[...]
    lt = logits.T
    neg = -jnp.inf
    grp = lax.broadcasted_iota(jnp.int32, (8, tm), 0)
    gl = jnp.where(grp < N_GROUPS, lt[GROUP_LANE0:GROUP_LANE0 + 8, :], neg)
    g_max = jnp.max(gl, axis=0, keepdims=True)
    g_idx = jnp.min(jnp.where(gl == g_max, grp, 8), axis=0, keepdims=True)
    g_w = 1.0 / jnp.sum(jnp.exp(gl - g_max), axis=0, keepdims=True)
    eid = lax.broadcasted_iota(jnp.int32, (N_EXPERTS, tm), 0)
    first = g_idx * EXPERTS_PER_GROUP
    in_group = jnp.logical_and(eid >= first, eid < first + EXPERTS_PER_GROUP)
    el = jnp.where(in_group, lt[EXPERT_LANE0:EXPERT_LANE0 + N_EXPERTS, :], neg)
    l1 = jnp.max(el, axis=0, keepdims=True)
    i1 = jnp.min(jnp.where(el == l1, eid, N_EXPERTS), axis=0, keepdims=True)
    el2 = jnp.where(eid == i1, neg, el)
    l2 = jnp.max(el2, axis=0, keepdims=True)
    i2 = jnp.min(jnp.where(el2 == l2, eid, N_EXPERTS), axis=0, keepdims=True)
    e21 = jnp.exp(l2 - l1)
    w1 = g_w * (1.0 / (1.0 + e21))
    w2 = g_w * (e21 / (1.0 + e21))
    hit1 = eid == i1
    hit2 = eid == i2
    onehot = jnp.logical_or(hit1, hit2).astype(F32)
    r = lax.broadcasted_iota(jnp.int32, (tm, tm), 0)
    c = lax.broadcasted_iota(jnp.int32, (tm, tm), 1)
    before = (r < c).astype(BF16)
    cum = jnp.dot(onehot.astype(BF16), before, preferred_element_type=F32) + carry[:, 0:1]
    rank1 = jnp.sum(jnp.where(hit1, cum, 0.0), axis=0, keepdims=True)
    rank2 = jnp.sum(jnp.where(hit2, cum, 0.0), axis=0, keepdims=True)
    row = lax.broadcasted_iota(jnp.int32, (LANES, tm), 0)
    rows = jnp.where(row == 0, i1.astype(F32), 0.0)
    rows = jnp.where(row == 1, i2.astype(F32), rows)
    rows = jnp.where(row == 2, w1, rows)
    rows = jnp.where(row == 3, w2, rows)
    rows = jnp.where(row == 4, rank1, rows)
    rows = jnp.where(row == 5, rank2, rows)
    rows_ref[...] = rows[0:8, :]
    info_ref[...] = rows.T
    total = carry[:, 0:1] + jnp.sum(onehot, axis=1, keepdims=True)
    carry[...] = jnp.broadcast_to(total, carry.shape)
    cnt_ref[...] = jnp.broadcast_to(total, cnt_ref.shape)


def _route(x, g_ffn, w_r, b_r, count_in, tm):
    n = x.shape[0]
    const = lambda i: (0, 0)
    return pl.pallas_call(
        _route_kernel,
        grid=(n // tm,),
        in_specs=[
            pl.BlockSpec((tm, D_MODEL), lambda i: (i, 0)),
            pl.BlockSpec((1, D_MODEL), const),
            pl.BlockSpec((D_MODEL, LANES), const),
            pl.BlockSpec((1, LANES), const),
            pl.BlockSpec((N_EXPERTS, LANES), const),
        ],
        out_specs=[
            pl.BlockSpec((tm, LANES), lambda i: (i, 0)),
            pl.BlockSpec((8, tm), lambda i: (0, i)),
            pl.BlockSpec((N_EXPERTS, LANES), const),
        ],
        out_shape=[
            jax.ShapeDtypeStruct((n, LANES), F32),
            jax.ShapeDtypeStruct((8, n), F32),
            jax.ShapeDtypeStruct((N_EXPERTS, LANES), F32),
        ],
        scratch_shapes=[pltpu.VMEM((N_EXPERTS, LANES), F32)],
        compiler_params=_params("arbitrary"),
        name="route",
    )(x, g_ffn, w_r, b_r, count_in)


DMA_UNROLL = 8


def _scatter_rows(d0_ref, d1_ref, x_ref, pad_out, sem, tile):
    def row_copy(r, dst_row):
        return pltpu.make_async_copy(x_ref.at[pl.ds(r, 1), :], pad_out.at[pl.ds(dst_row, 1), :], sem)

    def start(r, carry):
        row_copy(r, d0_ref[0, 0, r]).start(priority=0)
        row_copy(r, d1_ref[0, 0, r]).start(priority=1)
        return carry

    def wait(r, carry):
        row_copy(0, 0).wait()
        row_copy(0, 0).wait()
        return carry

    lax.fori_loop(0, tile, start, 0, unroll=DMA_UNROLL)
    lax.fori_loop(0, tile, wait, 0, unroll=DMA_UNROLL)


def _dispatch_kernel(zero_ref, d0p_ref, d1p_ref, d0s_ref, d1s_ref, xp_ref, xs_ref, pad_out, zero_scr, sem, zsem,
                     *, tile, n_tiles):
    i = pl.program_id(0)

    @pl.when(i == 0)
    def _():
        zero_scr[...] = jnp.zeros_like(zero_scr)

        def zero_copy(e):
            row0 = pl.multiple_of(zero_ref[e], EXPERT_ROWS)
            return pltpu.make_async_copy(zero_scr, pad_out.at[pl.ds(row0, EXPERT_ROWS), :], zsem)

        for e in range(zero_ref.shape[0]):
            @pl.when(zero_ref[e] >= 0)
            def _():
                zero_copy(e).start()
        for e in range(zero_ref.shape[0]):
            @pl.when(zero_ref[e] >= 0)
            def _():
                zero_copy(e).wait()

    @pl.when(i < n_tiles)
    def _():
        _scatter_rows(d0p_ref, d1p_ref, xp_ref, pad_out, sem, tile)

    @pl.when(i == n_tiles)
    def _():
        _scatter_rows(d0s_ref, d1s_ref, xs_ref, pad_out, sem, xs_ref.shape[0])


def _dispatch(zero_rows, dest_p, dest_s, xp, xs, n_rows, tile):
    n_tiles = xp.shape[0] // tile
    n_s = xs.shape[0]
    last = n_tiles - 1
    tile_spec = pl.BlockSpec((1, 1, tile), lambda i: (jnp.minimum(i, last), 0, 0), memory_space=pltpu.SMEM)
    return pl.pallas_call(
        functools.partial(_dispatch_kernel, tile=tile, n_tiles=n_tiles),
        grid=(n_tiles + 1,),
        in_specs=[
            pl.BlockSpec(memory_space=pltpu.SMEM),
            tile_spec, tile_spec,
            pl.BlockSpec(memory_space=pltpu.SMEM),
            pl.BlockSpec(memory_space=pltpu.SMEM),
            pl.BlockSpec((tile, D_MODEL), lambda i: (jnp.minimum(i, last), 0)),
            pl.BlockSpec((n_s, D_MODEL), lambda i: (0, 0)),
        ],
        out_specs=pl.BlockSpec(memory_space=pl.ANY),
        out_shape=jax.ShapeDtypeStruct((n_rows, D_MODEL), F32),
        scratch_shapes=[pltpu.VMEM((EXPERT_ROWS, D_MODEL), F32), pltpu.SemaphoreType.DMA(()),
                        pltpu.SemaphoreType.DMA(())],
        compiler_params=_params("arbitrary"),
        name="dispatch",
    )(zero_rows, dest_p[0].reshape(n_tiles, 1, tile), dest_p[1].reshape(n_tiles, 1, tile),
      dest_s[0].reshape(1, 1, n_s), dest_s[1].reshape(1, 1, n_s), xp, xs)


def _expert_kernel(be_ref, nused_ref, x_ref, g_ref, wg_ref, wu_ref, wd_ref, y_ref, wg_s, wu_s, wd_s):
    i = pl.program_id(0)
    prev = be_ref[jnp.maximum(i - 1, 0)]
    active = i < nused_ref[0]

    @pl.when(jnp.logical_and(active, jnp.logical_or(i == 0, be_ref[i] != prev)))
    def _():
        wg_s[...] = wg_ref[...].astype(BF16)
        wu_s[...] = wu_ref[...].astype(BF16)
        wd_s[...] = wd_ref[...].astype(BF16)

    @pl.when(active)
    def _():
        hn = _rms(x_ref[...], g_ref[...]).astype(BF16)
        gate = jnp.dot(hn, wg_s[...], preferred_element_type=F32)
        up = jnp.dot(hn, wu_s[...], preferred_element_type=F32)
        act = (gate * jax.nn.sigmoid(gate)) * up
        y_ref[...] = jnp.dot(act.astype(BF16), wd_s[...], preferred_element_type=F32)

    @pl.when(jnp.logical_not(active))
    def _():
        y_ref[...] = jnp.zeros_like(y_ref)


def _experts(block_expert, n_used, x_pad, g_ffn, w_gate, w_up, w_down):
    n_rows = x_pad.shape[0]
    nb = n_rows // EXPERT_ROWS
    grid_spec = pltpu.PrefetchScalarGridSpec(
        num_scalar_prefetch=2,
        grid=(nb,),
        in_specs=[
            pl.BlockSpec((EXPERT_ROWS, D_MODEL), lambda i, be, nu: (i, 0)),
            pl.BlockSpec((1, D_MODEL), lambda i, be, nu: (0, 0)),
            pl.BlockSpec((None, None, D_MODEL, D_EXPERT), lambda i, be, nu: (0, be[i], 0, 0)),
            pl.BlockSpec((None, None, D_MODEL, D_EXPERT), lambda i, be, nu: (0, be[i], 0, 0)),
            pl.BlockSpec((None, None, D_EXPERT, D_MODEL), lambda i, be, nu: (0, be[i], 0, 0)),
        ],
        out_specs=pl.BlockSpec((EXPERT_ROWS, D_MODEL), lambda i, be, nu: (i, 0)),
        scratch_shapes=[
            pltpu.VMEM((D_MODEL, D_EXPERT), BF16),
            pltpu.VMEM((D_MODEL, D_EXPERT), BF16),
            pltpu.VMEM((D_EXPERT, D_MODEL), BF16),
        ],
    )
    return pl.pallas_call(
        _expert_kernel,
        grid_spec=grid_spec,
        out_shape=jax.ShapeDtypeStruct((n_rows, D_MODEL), F32),
        compiler_params=_params("arbitrary"),
        name="experts",
    )(block_expert, n_used, x_pad, g_ffn, w_gate, w_up, w_down)


def _final_kernel(d0_ref, d1_ref, d0n_ref, d1n_ref, x_ref, info_ref, p_ref, y_hbm, wpg_ref, wpp_ref, gple_ref,
                  gfin_ref, out_ref, ybuf, sem):
    tm = x_ref.shape[0]
    i = pl.program_id(0)
    buf = i % 2

    def row_copy(src_row, b, k, r):
        return pltpu.make_async_copy(y_hbm.at[pl.ds(src_row, 1), :], ybuf.at[b, k, pl.ds(r, 1), :], sem.at[b])

    def gather(da_ref, db_ref, b):
        def start(r, carry):
            row_copy(da_ref[0, 0, r], b, 0, r).start(priority=0)
            row_copy(db_ref[0, 0, r], b, 1, r).start(priority=1)
            return carry
        lax.fori_loop(0, tm, start, 0, unroll=DMA_UNROLL)

    @pl.when(i == 0)
    def _():
        gather(d0_ref, d1_ref, 0)

    @pl.when(i + 1 < pl.num_programs(0))
    def _():
        gather(d0n_ref, d1n_ref, 1 - buf)

    def wait(r, carry):
        row_copy(0, buf, 0, 0).wait()
        row_copy(0, buf, 1, 0).wait()
        return carry

    lax.fori_loop(0, tm, wait, 0, unroll=DMA_UNROLL)

    info = info_ref[...]
    x2 = x_ref[...] + (ybuf[buf, 0] * info[:, 2:3] + ybuf[buf, 1] * info[:, 3:4])
    gate = jax.nn.sigmoid(jnp.dot(_rms(x2, gple_ref[...]).astype(BF16), wpg_ref[...], preferred_element_type=F32))
    proj = jnp.dot(p_ref[...].astype(BF16), wpp_ref[...], preferred_element_type=F32)
    out_ref[...] = _rms(x2 + gate * proj, gfin_ref[...])


def _final(dest, x1, info, p, y_pad, w_pg, w_pp, g_ple, g_final, tm):
    n = x1.shape[0]
    d0 = dest[0].reshape(n // tm, 1, tm)
    d1 = dest[1].reshape(n // tm, 1, tm)
    const = lambda i: (0, 0)
    last = n // tm - 1
    cur = pl.BlockSpec((1, 1, tm), lambda i: (i, 0, 0), memory_space=pltpu.SMEM)
    nxt = pl.BlockSpec((1, 1, tm), lambda i: (jnp.minimum(i + 1, last), 0, 0), memory_space=pltpu.SMEM)
    return pl.pallas_call(
        _final_kernel,
        grid=(n // tm,),
        in_specs=[
            cur, cur, nxt, nxt,
            pl.BlockSpec((tm, D_MODEL), lambda i: (i, 0)),
            pl.BlockSpec((tm, LANES), lambda i: (i, 0)),
            pl.BlockSpec((tm, PLE_DIM), lambda i: (i, 0)),
            pl.BlockSpec(memory_space=pl.ANY),
            pl.BlockSpec((D_MODEL, D_MODEL), const),
            pl.BlockSpec((PLE_DIM, D_MODEL), const),
            pl.BlockSpec((1, D_MODEL), const),
            pl.BlockSpec((1, D_MODEL), const),
        ],
        out_specs=pl.BlockSpec((tm, D_MODEL), lambda i: (i, 0)),
        out_shape=jax.ShapeDtypeStruct((n, D_MODEL), F32),
        scratch_shapes=[pltpu.VMEM((2, 2, tm, D_MODEL), F32), pltpu.SemaphoreType.DMA((2,))],
        compiler_params=_params("arbitrary"),
        name="final",
    )(d0, d1, d0, d1, x1, info, p, y_pad, w_pg, w_pp, g_ple, g_final)


def _pad_lanes(a):
    return jnp.pad(a, ((0, 0), (0, LANES - a.shape[1])))


def kernel(x_prompt, x_sample, p_prompt, p_sample, state_mlstm_C, state_mlstm_n, state_mlstm_m, state_swa_k, state_swa_v, g_mix, w_in, b_igate, b_fgate, g_mh, attn_sinks, w_out, g_ffn, w_router_g, b_router_g, w_router_e, b_router_e, w_gate, w_up, w_down, g_ple, w_ple_gate, w_ple_proj, g_final):
    batch, seq, d = x_prompt.shape
    dec = x_sample.shape[0]
    n_p = batch * seq
    xp = x_prompt.reshape(n_p, d)
    xs = x_sample.reshape(dec, d)

    w_in0 = w_in[0]
    head_order = jnp.array(SWA_HEAD_ORDER, jnp.int32)
    w_aq = w_in0[:, ATT_OFF:ATT_OFF + A_Q_W].reshape(d, A_HEADS, A_HEAD_DIM)[:, head_order].reshape(d, A_Q_W)
    zf_pad = ZF_W - (ZF_GATE + LANES)
    w_cat = jnp.concatenate(
        [w_in0[:, :3 * M_W], w_aq,
         w_in0[:, 3 * M_W:GATE_OFF], w_in0[:, ATT_OFF + A_Q_W:],
         _pad_lanes(w_in0[:, GATE_OFF:ATT_OFF]), jnp.zeros((d, zf_pad), F32)],
        axis=1).astype(BF16)
    gate_bias8 = jnp.concatenate([b_igate, b_fgate], axis=1)
    gate_bias = _pad_lanes(gate_bias8)
    w_oa = w_out[0, M_W:].reshape(A_HEADS, A_HEAD_DIM, d)[head_order].reshape(A_Q_W, d)
    w_o = jnp.concatenate([w_out[0, :M_W], w_oa], axis=0).astype(BF16)
    kv_of_head = jnp.arange(A_HEADS, dtype=jnp.int32) // A_GROUP
    head_mask = (kv_of_head[:, None] == jnp.arange(A_KV_HEADS, dtype=jnp.int32)[None, :]).astype(F32)
    gap = EXPERT_LANE0 - N_GROUPS
    w_r = _pad_lanes(jnp.concatenate([w_router_g[0], jnp.zeros((d, gap), F32), w_router_e[0]], axis=1))
    b_r = _pad_lanes(jnp.concatenate([b_router_g, jnp.zeros((1, gap), F32), b_router_e], axis=1))
    w_pg = w_ple_gate[0].astype(BF16)
    w_pp = w_ple_proj[0].astype(BF16)
    sink_col = attn_sinks.reshape(A_HEADS, 1)

    zb, zf = _inproj_split(xp, g_mix, w_cat, tm=1024, tn=1024)
    om_p, c_p, n_p8, m_p8 = _mlstm_prompt(zb, zf, gate_bias, g_mh, batch, seq)
    oa_p = _swa_prompt(zb, zf, attn_sinks, batch, seq)
    x1p = _outproj(xp, om_p, oa_p, w_o, tm=512)

    zs = _inproj(xs, g_mix, w_in0, tm=dec, tn=512, precision=HI)
    om_s, c_s, n_s, m_s = _mlstm_sample(
        zs.reshape(dec, 1, zs.shape[1]), gate_bias8,
        state_mlstm_C[0], state_mlstm_n[0].reshape(dec, 1, M_W), state_mlstm_m[0].reshape(dec, 1, M_HEADS), g_mh)
    ks_off = ATT_OFF + A_Q_W
    vs_off = ks_off + A_KV_W
    q_heads = zs[:, ATT_OFF:ks_off].reshape(dec, A_HEADS, 1, A_HEAD_DIM)
    qx = (q_heads * head_mask[None, :, :, None]).reshape(dec, A_HEADS, A_KV_W)
    ox_s, k_s, v_s = _swa_sample(
        qx,
        zs[:, ks_off:vs_off].reshape(dec, 1, A_KV_W),
        zs[:, vs_off:].reshape(dec, 1, A_KV_W),
        state_swa_k[0].reshape(dec, WINDOW, A_KV_W), state_swa_v[0].reshape(dec, WINDOW, A_KV_W), sink_col)
    oa_s = jnp.sum(ox_s.reshape(dec, A_HEADS, A_KV_HEADS, A_HEAD_DIM) * head_mask[None, :, :, None], axis=2)
    x1s = _outproj(xs, om_s.reshape(dec, M_W), oa_s.reshape(dec, A_Q_W), w_out[0], tm=dec, precision=HI)

    info_p, rows_p, cnt_p = _route(x1p, g_ffn, w_r, b_r, jnp.zeros((N_EXPERTS, LANES), F32), tm=512)
    info_s, rows_s, cnt = _route(x1s, g_ffn, w_r, b_r, cnt_p, tm=dec)
    counts = cnt[:, 0].astype(jnp.int32)
    padded = ((counts + EXPERT_ROWS - 1) // EXPERT_ROWS) * EXPERT_ROWS
    pad_end = jnp.cumsum(padded)
    pad_start = pad_end - padded
    n_assign = 2 * (n_p + dec)
    n_blocks = -(-n_assign // EXPERT_ROWS) + N_EXPERTS
    block_row0 = jnp.arange(n_blocks, dtype=jnp.int32) * EXPERT_ROWS
    block_expert = jnp.minimum(
        jnp.sum((pad_end[None, :] <= block_row0[:, None]).astype(jnp.int32), axis=1), N_EXPERTS - 1)
    n_used = (pad_end[-1:] // EXPERT_ROWS).astype(jnp.int32)
    spare = n_used + jnp.arange(N_EXPERTS, dtype=jnp.int32)
    zero_rows = jnp.concatenate([
        jnp.where(padded > 0, pad_end - EXPERT_ROWS, -1),
        jnp.where(spare < n_blocks, spare * EXPERT_ROWS, -1)]).astype(jnp.int32)

    def dest_of(rows):
        e = rows[0:2].astype(jnp.int32)
        hit = e[:, :, None] == jnp.arange(N_EXPERTS, dtype=jnp.int32)
        return jnp.sum(jnp.where(hit, pad_start, 0), axis=2) + rows[4:6].astype(jnp.int32)

    dest_p = dest_of(rows_p)
    dest_s = dest_of(rows_s)

    x_pad = _dispatch(zero_rows, dest_p, dest_s, x1p, x1s, n_blocks * EXPERT_ROWS, tile=512)
    y_pad = _experts(block_expert, n_used, x_pad, g_ffn, w_gate, w_up, w_down)
    y_p = _final(dest_p, x1p, info_p, p_prompt[0].reshape(n_p, PLE_DIM), y_pad, w_pg, w_pp, g_ple, g_final.reshape(1, d), tm=256)
    y_s = _final(dest_s, x1s, info_s, p_sample[0].reshape(dec, PLE_DIM), y_pad, w_pg, w_pp, g_ple, g_final.reshape(1, d), tm=dec)

    zf3 = zf.reshape(batch, seq, ZF_W)
    swa_k_p = zf3[:, seq - WINDOW:, ZF_K:ZF_V].reshape(1, batch, WINDOW, A_KV_HEADS, A_HEAD_DIM)
    swa_v_p = zf3[:, seq - WINDOW:, ZF_V:ZF_GATE].reshape(1, batch, WINDOW, A_KV_HEADS, A_HEAD_DIM)
    return (
        y_p.reshape(batch, seq, d),
        y_s.reshape(dec, 1, d),
        c_p[None],
        n_p8[None, :, :M_HEADS, :],
        m_p8[None, :, :M_HEADS, 0],
        swa_k_p,
        swa_v_p,
        c_s[None],
        n_s.reshape(1, dec, M_HEADS, M_DK),
        m_s[None, :, 0, :M_HEADS],
        k_s.reshape(1, dec, WINDOW, A_KV_HEADS, A_HEAD_DIM),
        v_s.reshape(1, dec, WINDOW, A_KV_HEADS, A_HEAD_DIM),
    )
```

```python
import functools

import jax
import jax.numpy as jnp
from jax import lax
from jax.experimental import pallas as pl
from jax.experimental.pallas import tpu as pltpu

F32 = jnp.float32
BF16 = jnp.bfloat16
HI = lax.Precision.HIGHEST

D_MODEL = 2048
M_HEADS = 4
M_DK = 256
M_DV = 256
A_HEADS = 16
A_KV_HEADS = 4
A_HEAD_DIM = 64
A_GROUP = A_HEADS // A_KV_HEADS
WINDOW = 128
M_W = M_HEADS * M_DK
A_Q_W = A_HEADS * A_HEAD_DIM
A_KV_W = A_KV_HEADS * A_HEAD_DIM
GATE_OFF = 4 * M_W
ATT_OFF = GATE_OFF + 2 * M_HEADS
ZB_W = 3 * M_W + A_Q_W
ZF_MO, ZF_K, ZF_V, ZF_GATE = 0, M_W, M_W + A_KV_W, M_W + 2 * A_KV_W
ZF_W = 2 * M_W
N_GROUPS = 4
EXPERTS_PER_GROUP = 8
N_EXPERTS = N_GROUPS * EXPERTS_PER_GROUP
D_EXPERT = 512
PLE_DIM = 256
EPS = 1e-6

LANES = 128
MLSTM_CHUNK = 256
EXPERT_ROWS = 512
VMEM_LIMIT = 56 * 1024 * 1024
NT_DIMS = (((1,), (1,)), ((), ()))
TN_DIMS = (((0,), (0,)), ((), ()))


def _params(*sem):
    return pltpu.CompilerParams(dimension_semantics=sem, vmem_limit_bytes=VMEM_LIMIT)


def _rms(x, g):
    r = lax.rsqrt(jnp.mean(x * x, axis=-1, keepdims=True) + EPS)
    return (x * r) * g


def _log_sigmoid(x):
    return jnp.minimum(x, 0.0) - jnp.log1p(jnp.exp(-jnp.abs(x)))


def _inproj_kernel(x_ref, g_ref, w_ref, z_ref, hn_scr, *, precision):
    @pl.when(pl.program_id(1) == 0)
    def _():
        hn_scr[...] = _rms(x_ref[...], g_ref[...]).astype(hn_scr.dtype)

    z_ref[...] = jnp.dot(hn_scr[...], w_ref[...], precision=precision, preferred_element_type=F32)


def _inproj(x, g_mix, w, tm, tn, precision=None):
    n = x.shape[0]
    width = w.shape[1]
    return pl.pallas_call(
        functools.partial(_inproj_kernel, precision=precision),
        grid=(n // tm, pl.cdiv(width, tn)),
        in_specs=[
            pl.BlockSpec((tm, D_MODEL), lambda i, j: (i, 0)),
            pl.BlockSpec((1, D_MODEL), lambda i, j: (0, 0)),
            pl.BlockSpec((D_MODEL, tn), lambda i, j: (0, j)),
        ],
        out_specs=pl.BlockSpec((tm, tn), lambda i, j: (i, j)),
        out_shape=jax.ShapeDtypeStruct((n, width), F32),
        scratch_shapes=[pltpu.VMEM((tm, D_MODEL), w.dtype)],
        compiler_params=_params("parallel", "arbitrary"),
        name="inproj",
    )(x, g_mix, w)


def _inproj_split_kernel(x_ref, g_ref, w_ref, zb_ref, zf_ref, hn_scr, *, n_bf16_blocks):
    j = pl.program_id(1)

    @pl.when(j == 0)
    def _():
        hn_scr[...] = _rms(x_ref[...], g_ref[...]).astype(BF16)

    acc = jnp.dot(hn_scr[...], w_ref[...], preferred_element_type=F32)

    @pl.when(j < n_bf16_blocks)
    def _():
        zb_ref[...] = acc.astype(BF16)

    @pl.when(j >= n_bf16_blocks)
    def _():
        zf_ref[...] = acc


def _inproj_split(x, g_mix, w, tm, tn):
    n = x.shape[0]
    nb = ZB_W // tn
    nf = ZF_W // tn
    return pl.pallas_call(
        functools.partial(_inproj_split_kernel, n_bf16_blocks=nb),
        grid=(n // tm, nb + nf),
        in_specs=[
            pl.BlockSpec((tm, D_MODEL), lambda i, j: (i, 0)),
            pl.BlockSpec((1, D_MODEL), lambda i, j: (0, 0)),
            pl.BlockSpec((D_MODEL, tn), lambda i, j: (0, j)),
        ],
        out_specs=[
            pl.BlockSpec((tm, tn), lambda i, j: (i, jnp.minimum(j, nb - 1))),
            pl.BlockSpec((tm, tn), lambda i, j: (i, jnp.maximum(j - nb, 0))),
        ],
        out_shape=[
            jax.ShapeDtypeStruct((n, ZB_W), BF16),
            jax.ShapeDtypeStruct((n, ZF_W), F32),
        ],
        scratch_shapes=[pltpu.VMEM((tm, D_MODEL), BF16)],
        compiler_params=_params("parallel", "arbitrary"),
        name="inproj_prompt",
    )(x, g_mix, w)


def _mlstm_out(h, gmh, mo):
    hn = h * lax.rsqrt(jnp.mean(h * h, axis=-1, keepdims=True) + EPS)
    return (hn * gmh) * jax.nn.sigmoid(mo)


def _gates(g_raw, bias):
    g = g_raw + bias
    lane = lax.broadcasted_iota(jnp.int32, g.shape, g.ndim - 1)
    return jnp.where(lane < M_HEADS, g, _log_sigmoid(g))


def _mlstm_kernel(q_ref, k_ref, v_ref, mo_ref, g_ref, bias_ref, gmh_ref,
                  o_ref, c_out, n_out, m_out, c_scr, n_scr, m_scr, *, chunk, n_chunks):
    c_idx = pl.program_id(1)

    @pl.when(c_idx == 0)
    def _():
        c_scr[...] = jnp.zeros_like(c_scr)
        n_scr[...] = jnp.zeros_like(n_scr)
        m_scr[...] = jnp.zeros_like(m_scr)

    L = chunk
    row = lax.broadcasted_iota(jnp.int32, (L, L), 0)
    col = lax.broadcasted_iota(jnp.int32, (L, L), 1)
    causal = row >= col
    ltri = causal.astype(F32)
    utri = (row <= col).astype(F32)
    g_c = _gates(g_ref[...], bias_ref[...])
    f_c = jnp.dot(ltri, g_c, precision=HI, preferred_element_type=F32)
    g_r = g_c.T[0:2 * M_HEADS, :]
    f_r = jnp.dot(g_r, utri, precision=HI, preferred_element_type=F32)

    for h in range(M_HEADS):
        hs = slice(h * M_DK, (h + 1) * M_DK)
        hf = M_HEADS + h
        a_row = g_r[h:h + 1, :] - f_r[hf:hf + 1, :]
        f_col = f_c[:, hf:hf + 1]
        a_col = g_c[:, h:h + 1] - f_col
        m0 = m_scr[h:h + 1, 0:1]
        a_max = jnp.max(jnp.where(causal, a_row, -jnp.inf), axis=1, keepdims=True)
        m_t = f_col + jnp.maximum(m0, a_max)
        dec0 = jnp.exp(f_col + m0 - m_t)
        d = jnp.exp(jnp.where(causal, (f_col - m_t) + a_row, -jnp.inf))
        qb = q_ref[:, hs]
        kb = k_ref[:, hs] * (M_DK ** -0.5)
        vb = v_ref[:, hs]
        q = qb.astype(F32)
        k = kb.astype(F32)
        w = lax.dot_general(qb, kb, NT_DIMS, preferred_element_type=F32) * d
        c0 = c_scr[h]
        n0 = n_scr[h:h + 1, :]
        num = jnp.dot(w.astype(BF16), vb, preferred_element_type=F32) + dec0 * lax.dot_general(
            qb, c0.astype(BF16), NT_DIMS, preferred_element_type=F32)
        den = jnp.sum(w, axis=1, keepdims=True) + dec0 * jnp.sum(q * n0, axis=1, keepdims=True)
        hh = num / jnp.maximum(jnp.abs(den), jnp.exp(-m_t))
        m_end = m_t[L - 1:L, :]
        f_end = f_col[L - 1:L, :]
        w_end = jnp.exp(f_end + a_col - m_end)
        dec_end = jnp.exp(f_end + m0 - m_end)
        kw = k * w_end
        c_scr[h] = dec_end * c0 + lax.dot_general(vb, kw.astype(BF16), TN_DIMS, preferred_element_type=F32)
        n_scr[h:h + 1, :] = dec_end * n0 + jnp.sum(kw, axis=0, keepdims=True)
        m_scr[h:h + 1, :] = jnp.broadcast_to(m_end, (1, LANES))
        o_ref[:, hs] = _mlstm_out(hh, gmh_ref[:, hs], mo_ref[:, hs]).astype(o_ref.dtype)

    @pl.when(c_idx == n_chunks - 1)
    def _():
        c_out[0] = c_scr[...]
        n_out[0] = n_scr[...]
        m_out[0] = m_scr[...]


def _mlstm_prompt(zb, zf, gate_bias, g_mh, batch, seq):
    L = MLSTM_CHUNK
    nc = seq // L
    rows = lambda b, c: b * nc + c
    gate_blk = ZF_GATE // LANES
    return pl.pallas_call(
        functools.partial(_mlstm_kernel, chunk=L, n_chunks=nc),
        grid=(batch, nc),
        in_specs=[
            pl.BlockSpec((L, M_W), lambda b, c: (rows(b, c), 0)),
            pl.BlockSpec((L, M_W), lambda b, c: (rows(b, c), 1)),
            pl.BlockSpec((L, M_W), lambda b, c: (rows(b, c), 2)),
            pl.BlockSpec((L, M_W), lambda b, c: (rows(b, c), ZF_MO // M_W)),
            pl.BlockSpec((L, LANES), lambda b, c: (rows(b, c), gate_blk)),
            pl.BlockSpec((1, LANES), lambda b, c: (0, 0)),
            pl.BlockSpec((1, M_W), lambda b, c: (0, 0)),
        ],
        out_specs=[
            pl.BlockSpec((L, M_W), lambda b, c: (rows(b, c), 0)),
            pl.BlockSpec((1, M_HEADS, M_DV, M_DK), lambda b, c: (b, 0, 0, 0)),
            pl.BlockSpec((1, 8, M_DK), lambda b, c: (b, 0, 0)),
            pl.BlockSpec((1, 8, LANES), lambda b, c: (b, 0, 0)),
        ],
        out_shape=[
            jax.ShapeDtypeStruct((batch * seq, M_W), BF16),
            jax.ShapeDtypeStruct((batch, M_HEADS, M_DV, M_DK), F32),
            jax.ShapeDtypeStruct((batch, 8, M_DK), F32),
            jax.ShapeDtypeStruct((batch, 8, LANES), F32),
        ],
        scratch_shapes=[
            pltpu.VMEM((M_HEADS, M_DV, M_DK), F32),
            pltpu.VMEM((8, M_DK), F32),
            pltpu.VMEM((8, LANES), F32),
        ],
        compiler_params=_params("parallel", "arbitrary"),
        name="mlstm_prompt",
    )(zb, zb, zb, zf, zf, gate_bias, g_mh)


MLSTM_STEP_TOKENS = 2


def _mlstm_step_kernel(z_ref, bias_ref, c_ref, n_ref, m_ref, gmh_ref, o_ref, c_out, n_out, m_out):
    row = lax.broadcasted_iota(jnp.int32, (M_DV, M_DV), 0)
    col = lax.broadcasted_iota(jnp.int32, (M_DV, M_DV), 1)
    eye = row == col
    for t in range(z_ref.shape[0]):
        _mlstm_step_token(t, eye, z_ref, bias_ref, c_ref, n_ref, m_ref, gmh_ref, o_ref, c_out, n_out, m_out)


def _mlstm_step_token(t, eye, z_ref, bias_ref, c_ref, n_ref, m_ref, gmh_ref, o_ref, c_out, n_out, m_out):
    z = z_ref[t]
    gates = _gates(z[:, GATE_OFF:ATT_OFF], bias_ref[...])
    m_in = m_ref[t]
    lane = lax.broadcasted_iota(jnp.int32, (1, LANES), 1)
    m_new = jnp.zeros((1, LANES), F32)
    for h in range(M_HEADS):
        hs = slice(h * M_DK, (h + 1) * M_DK)
        q = z[:, h * M_DK:(h + 1) * M_DK]
        k = z[:, M_W + h * M_DK:M_W + (h + 1) * M_DK] * (M_DK ** -0.5)
        v = z[:, 2 * M_W + h * M_DV:2 * M_W + (h + 1) * M_DV]
        mo = z[:, 3 * M_W + h * M_DV:3 * M_W + (h + 1) * M_DV]
        ip = gates[:, h:h + 1]
        lf = gates[:, M_HEADS + h:M_HEADS + h + 1]
        m0 = m_in[:, h:h + 1]
        m_t = jnp.maximum(lf + m0, ip)
        dec0 = jnp.exp(lf + m0 - m_t)
        d = jnp.exp(ip - m_t)
        c0 = c_ref[t, h]
        n0 = n_ref[t, :, hs]
        w = jnp.sum(q * k, axis=1, keepdims=True) * d
        cq_col = jnp.sum(c0 * q, axis=1, keepdims=True)
        cq = jnp.sum(jnp.where(eye, cq_col, 0.0), axis=0, keepdims=True)
        num = w * v + dec0 * cq
        den = w + dec0 * jnp.sum(n0 * q, axis=1, keepdims=True)
        hh = num / jnp.maximum(jnp.abs(den), jnp.exp(-m_t))
        v_col = jnp.sum(jnp.where(eye, v, 0.0), axis=1, keepdims=True)
        c_out[t, h] = dec0 * c0 + (d * v_col) * k
        n_out[t, :, hs] = dec0 * n0 + d * k
        m_new = jnp.where(lane == h, m_t, m_new)
        o_ref[t, :, hs] = _mlstm_out(hh, gmh_ref[:, hs], mo).astype(o_ref.dtype)
    m_out[t] = m_new


def _mlstm_sample(z3, gate_bias, c0, n0, m0, g_mh):
    nb = z3.shape[0]
    nt = MLSTM_STEP_TOKENS
    tok = lambda b: (b, 0, 0)
    return pl.pallas_call(
        _mlstm_step_kernel,
        grid=(nb // nt,),
        in_specs=[
            pl.BlockSpec((nt, 1, z3.shape[2]), tok),
            pl.BlockSpec((1, 2 * M_HEADS), lambda b: (0, 0)),
            pl.BlockSpec((nt, M_HEADS, M_DV, M_DK), lambda b: (b, 0, 0, 0)),
            pl.BlockSpec((nt, 1, M_W), tok),
            pl.BlockSpec((nt, 1, M_HEADS), tok),
            pl.BlockSpec((1, M_W), lambda b: (0, 0)),
        ],
        out_specs=[
            pl.BlockSpec((nt, 1, M_W), tok),
            pl.BlockSpec((nt, M_HEADS, M_DV, M_DK), lambda b: (b, 0, 0, 0)),
            pl.BlockSpec((nt, 1, M_W), tok),
            pl.BlockSpec((nt, 1, LANES), tok),
        ],
        out_shape=[
            jax.ShapeDtypeStruct((nb, 1, M_W), F32),
            jax.ShapeDtypeStruct((nb, M_HEADS, M_DV, M_DK), F32),
            jax.ShapeDtypeStruct((nb, 1, M_W), F32),
            jax.ShapeDtypeStruct((nb, 1, LANES), F32),
        ],
        compiler_params=_params("parallel"),
        name="mlstm_sample",
    )(z3, gate_bias, c0, n0, m0, g_mh)


_EVEN_KV = [h for h in range(A_HEADS) if (h // A_GROUP) % 2 == 0]
_ODD_KV = [h for h in range(A_HEADS) if (h // A_GROUP) % 2 == 1]
SWA_HEAD_ORDER = tuple(h for pair in zip(_EVEN_KV, _ODD_KV) for h in pair)
HALF = LANES // 2
KV_SPAN = 2 * WINDOW


def _swa_kernel(sink_ref, q_ref, kc_ref, kp_ref, vc_ref, vp_ref, o_ref, qx_scr, p_scr):
    j = pl.program_id(1)
    lo_half = lax.broadcasted_iota(jnp.int32, (WINDOW, LANES), 1) < HALF
    zero_tile = jnp.zeros((WINDOW, LANES), BF16)
    for slot, head in enumerate(SWA_HEAD_ORDER):
        kv = head // A_GROUP
        tile = q_ref[:, (slot // 2) * LANES:(slot // 2 + 1) * LANES]
        keep = lo_half if slot % 2 == 0 else jnp.logical_not(lo_half)
        qm = jnp.where(keep, tile, jnp.zeros_like(tile))
        parts = [qm, zero_tile] if kv // 2 == 0 else [zero_tile, qm]
        qx_scr[slot * WINDOW:(slot + 1) * WINDOW, :] = jnp.concatenate(parts, axis=1)

    k2 = jnp.concatenate([kp_ref[...], kc_ref[...]], axis=0).astype(BF16)
    v2 = jnp.concatenate([vp_ref[...], vc_ref[...]], axis=0).astype(BF16)
    s_all = lax.dot_general(qx_scr[...], k2, NT_DIMS, preferred_element_type=F32) * (A_HEAD_DIM ** -0.5)

    t = lax.broadcasted_iota(jnp.int32, (WINDOW, KV_SPAN), 0)
    kk = lax.broadcasted_iota(jnp.int32, (WINDOW, KV_SPAN), 1)
    in_prev = jnp.logical_and(jnp.logical_and(kk < WINDOW, kk > t), j > 0)
    in_cur = jnp.logical_and(kk >= WINDOW, kk - WINDOW <= t)
    allowed = jnp.logical_or(in_prev, in_cur)
    for slot, head in enumerate(SWA_HEAD_ORDER):
        rows = slice(slot * WINDOW, (slot + 1) * WINDOW)
        s = jnp.where(allowed, s_all[rows, :], -jnp.inf)
        sink = sink_ref[0, head]
        mx = jnp.maximum(jnp.max(s, axis=1, keepdims=True), sink)
        e = jnp.exp(s - mx)
        den = jnp.sum(e, axis=1, keepdims=True) + jnp.exp(sink - mx)
        p_scr[rows, :] = (e * (1.0 / den)).astype(BF16)

    o_all = jnp.dot(p_scr[...], v2, preferred_element_type=F32)
    for c in range(A_HEADS // 2):
        tiles = []
        for slot in (2 * c, 2 * c + 1):
            kv = SWA_HEAD_ORDER[slot] // A_GROUP
            tiles.append(o_all[slot * WINDOW:(slot + 1) * WINDOW, (kv // 2) * LANES:(kv // 2 + 1) * LANES])
        o_ref[:, c * LANES:(c + 1) * LANES] = jnp.where(lo_half, tiles[0], tiles[1]).astype(o_ref.dtype)


def _swa_prompt(zb, zf, sinks, batch, seq):
    nb = seq // WINDOW
    q_blk = (3 * M_W) // A_Q_W
    k_blk = ZF_K // A_KV_W
    cur = lambda b, j: b * nb + j
    prev = lambda b, j: b * nb + jnp.maximum(j - 1, 0)
    return pl.pallas_call(
        _swa_kernel,
        grid=(batch, nb),
        in_specs=[
            pl.BlockSpec(memory_space=pltpu.SMEM),
            pl.BlockSpec((WINDOW, A_Q_W), lambda b, j: (cur(b, j), q_blk)),
            pl.BlockSpec((WINDOW, A_KV_W), lambda b, j: (cur(b, j), k_blk)),
            pl.BlockSpec((WINDOW, A_KV_W), lambda b, j: (prev(b, j), k_blk)),
            pl.BlockSpec((WINDOW, A_KV_W), lambda b, j: (cur(b, j), k_blk + 1)),
            pl.BlockSpec((WINDOW, A_KV_W), lambda b, j: (prev(b, j), k_blk + 1)),
        ],
        out_specs=pl.BlockSpec((WINDOW, A_Q_W), lambda b, j: (cur(b, j), 0)),
        out_shape=jax.ShapeDtypeStruct((batch * seq, A_Q_W), BF16),
        scratch_shapes=[pltpu.VMEM((A_HEADS * WINDOW, A_KV_W), BF16), pltpu.VMEM((A_HEADS * WINDOW, KV_SPAN), BF16)],
        compiler_params=_params("parallel", "arbitrary"),
        name="swa_prompt",
    )(sinks, zb, zf, zf, zf, zf)


SWA_STEP_TOKENS = 4


def _swa_step_kernel(qx_ref, kn_ref, vn_ref, bk_ref, bv_ref, sink_ref, o_ref, ko_ref, vo_ref):
    in_window = lax.broadcasted_iota(jnp.int32, (1, WINDOW), 1) >= 1
    scale = A_HEAD_DIM ** -0.5
    sink = sink_ref[...]
    for t in range(qx_ref.shape[0]):
        qx = qx_ref[t]
        bk = bk_ref[t]
        bv = bv_ref[t]
        kn = kn_ref[t]
        vn = vn_ref[t]
        s = lax.dot_general(qx, bk, NT_DIMS, precision=HI, preferred_element_type=F32) * scale
        s = jnp.where(in_window, s, -jnp.inf)
        s_new = jnp.sum(qx * kn, axis=1, keepdims=True) * scale
        mx = jnp.maximum(jnp.maximum(jnp.max(s, axis=1, keepdims=True), s_new), sink)
        e = jnp.exp(s - mx)
        e_new = jnp.exp(s_new - mx)
        den = jnp.sum(e, axis=1, keepdims=True) + e_new + jnp.exp(sink - mx)
        o_ref[t] = jnp.dot(e / den, bv, precision=HI, preferred_element_type=F32) + (e_new / den) * vn
        ko_ref[t, 0:WINDOW - 1, :] = bk[1:WINDOW, :]
        ko_ref[t, WINDOW - 1:WINDOW, :] = kn
        vo_ref[t, 0:WINDOW - 1, :] = bv[1:WINDOW, :]
        vo_ref[t, WINDOW - 1:WINDOW, :] = vn


def _swa_sample(qx, kn3, vn3, buf_k, buf_v, sink_col):
    nb = qx.shape[0]
    nt = SWA_STEP_TOKENS
    tok = lambda b: (b, 0, 0)
    return pl.pallas_call(
        _swa_step_kernel,
        grid=(nb // nt,),
        in_specs=[
            pl.BlockSpec((nt, A_HEADS, A_KV_W), tok),
            pl.BlockSpec((nt, 1, A_KV_W), tok),
            pl.BlockSpec((nt, 1, A_KV_W), tok),
            pl.BlockSpec((nt, WINDOW, A_KV_W), tok),
            pl.BlockSpec((nt, WINDOW, A_KV_W), tok),
            pl.BlockSpec((A_HEADS, 1), lambda b: (0, 0)),
        ],
        out_specs=[
            pl.BlockSpec((nt, A_HEADS, A_KV_W), tok),
            pl.BlockSpec((nt, WINDOW, A_KV_W), tok),
            pl.BlockSpec((nt, WINDOW, A_KV_W), tok),
        ],
        out_shape=[
            jax.ShapeDtypeStruct((nb, A_HEADS, A_KV_W), F32),
            jax.ShapeDtypeStruct((nb, WINDOW, A_KV_W), F32),
            jax.ShapeDtypeStruct((nb, WINDOW, A_KV_W), F32),
        ],
        compiler_params=_params("parallel"),
        name="swa_sample",
    )(qx, kn3, vn3, buf_k, buf_v, sink_col)


GROUP_LANE0 = 0
EXPERT_LANE0 = 8


def _outproj_kernel(x_ref, om_ref, oa_ref, wm_ref, wa_ref, y_ref, *, precision):
    y_ref[...] = (x_ref[...]
                  + jnp.dot(om_ref[...], wm_ref[...], precision=precision, preferred_element_type=F32)
                  + jnp.dot(oa_ref[...], wa_ref[...], precision=precision, preferred_element_type=F32))


def _outproj(x, o_m, o_a, w, tm, precision=None):
    n = x.shape[0]
    return pl.pallas_call(
        functools.partial(_outproj_kernel, precision=precision),
        grid=(n // tm,),
        in_specs=[
            pl.BlockSpec((tm, D_MODEL), lambda i: (i, 0)),
            pl.BlockSpec((tm, M_W), lambda i: (i, 0)),
            pl.BlockSpec((tm, A_Q_W), lambda i: (i, 0)),
            pl.BlockSpec((M_W, D_MODEL), lambda i: (0, 0)),
            pl.BlockSpec((A_Q_W, D_MODEL), lambda i: (1, 0)),
        ],
        out_specs=pl.BlockSpec((tm, D_MODEL), lambda i: (i, 0)),
        out_shape=jax.ShapeDtypeStruct((n, D_MODEL), F32),
        compiler_params=_params("parallel"),
        name="outproj",
    )(x, o_m, o_a, w, w)


def _route_kernel(x_ref, g_ref, wr_ref, br_ref, cin_ref, info_ref, rows_ref, cnt_ref, carry):
    @pl.when(pl.program_id(0) == 0)
    def _():
        carry[...] = cin_ref[...]

    tm = x_ref.shape[0]
    hn = _rms(x_ref[...], g_ref[...])
    w_r = wr_ref[...]
    hn_hi = hn.astype(BF16)
    hn_lo = (hn - hn_hi.astype(F32)).astype(BF16)
    w_hi = w_r.astype(BF16)
    w_lo = (w_r - w_hi.astype(F32)).astype(BF16)
    logits = (jnp.dot(hn_hi, w_hi, preferred_element_type=F32)
              + (jnp.dot(hn_lo, w_hi, preferred_element_type=F32)
                 + jnp.dot(hn_hi, w_lo, preferred_element_type=F32))) + br_ref[...]
    lt = logits.T
    neg = -jnp.inf
    grp = lax.broadcasted_iota(jnp.int32, (8, tm), 0)
    gl = jnp.where(grp < N_GROUPS, lt[GROUP_LANE0:GROUP_LANE0 + 8, :], neg)
    g_max = jnp.max(gl, axis=0, keepdims=True)
    g_idx = jnp.min(jnp.where(gl == g_max, grp, 8), axis=0, keepdims=True)
    g_w = 1.0 / jnp.sum(jnp.exp(gl - g_max), axis=0, keepdims=True)
    eid = lax.broadcasted_iota(jnp.int32, (N_EXPERTS, tm), 0)
    first = g_idx * EXPERTS_PER_GROUP
    in_group = jnp.logical_and(eid >= first, eid < first + EXPERTS_PER_GROUP)
    el = jnp.where(in_group, lt[EXPERT_LANE0:EXPERT_LANE0 + N_EXPERTS, :], neg)
    l1 = jnp.max(el, axis=0, keepdims=True)
    i1 = jnp.min(jnp.where(el == l1, eid, N_EXPERTS), axis=0, keepdims=True)
    el2 = jnp.where(eid == i1, neg, el)
    l2 = jnp.max(el2, axis=0, keepdims=True)
    i2 = jnp.min(jnp.where(el2 == l2, eid, N_EXPERTS), axis=0, keepdims=True)
    e21 = jnp.exp(l2 - l1)
    w1 = g_w * (1.0 / (1.0 + e21))
    w2 = g_w * (e21 / (1.0 + e21))
    hit1 = eid == i1
    hit2 = eid == i2
    onehot = jnp.logical_or(hit1, hit2).astype(F32)
    r = lax.broadcasted_iota(jnp.int32, (tm, tm), 0)
    c = lax.broadcasted_iota(jnp.int32, (tm, tm), 1)
    before = (r < c).astype(BF16)
    cum = jnp.dot(onehot.astype(BF16), before, preferred_element_type=F32) + carry[:, 0:1]
    rank1 = jnp.sum(jnp.where(hit1, cum, 0.0), axis=0, keepdims=True)
    rank2 = jnp.sum(jnp.where(hit2, cum, 0.0), axis=0, keepdims=True)
    row = lax.broadcasted_iota(jnp.int32, (LANES, tm), 0)
    rows = jnp.where(row == 0, i1.astype(F32), 0.0)
    rows = jnp.where(row == 1, i2.astype(F32), rows)
    rows = jnp.where(row == 2, w1, rows)
    rows = jnp.where(row == 3, w2, rows)
    rows = jnp.where(row == 4, rank1, rows)
    rows = jnp.where(row == 5, rank2, rows)
    rows_ref[...] = rows[0:8, :]
    info_ref[...] = rows.T
    total = carry[:, 0:1] + jnp.sum(onehot, axis=1, keepdims=True)
    carry[...] = jnp.broadcast_to(total, carry.shape)
    cnt_ref[...] = jnp.broadcast_to(total, cnt_ref.shape)


def _route(x, g_ffn, w_r, b_r, count_in, tm):
    n = x.shape[0]
    const = lambda i: (0, 0)
    return pl.pallas_call(
        _route_kernel,
        grid=(n // tm,),
        in_specs=[
            pl.BlockSpec((tm, D_MODEL), lambda i: (i, 0)),
            pl.BlockSpec((1, D_MODEL), const),
            pl.BlockSpec((D_MODEL, LANES), const),
            pl.BlockSpec((1, LANES), const),
            pl.BlockSpec((N_EXPERTS, LANES), const),
        ],
        out_specs=[
            pl.BlockSpec((tm, LANES), lambda i: (i, 0)),
            pl.BlockSpec((8, tm), lambda i: (0, i)),
            pl.BlockSpec((N_EXPERTS, LANES), const),
        ],
        out_shape=[
            jax.ShapeDtypeStruct((n, LANES), F32),
            jax.ShapeDtypeStruct((8, n), F32),
            jax.ShapeDtypeStruct((N_EXPERTS, LANES), F32),
        ],
        scratch_shapes=[pltpu.VMEM((N_EXPERTS, LANES), F32)],
        compiler_params=_params("arbitrary"),
        name="route",
    )(x, g_ffn, w_r, b_r, count_in)


DMA_UNROLL = 8


def _scatter_rows(d0_ref, d1_ref, x_ref, pad_out, sem, tile):
    def row_copy(r, dst_row):
        return pltpu.make_async_copy(x_ref.at[pl.ds(r, 1), :], pad_out.at[pl.ds(dst_row, 1), :], sem)

    def start(r, carry):
        row_copy(r, d0_ref[0, 0, r]).start(priority=0)
        row_copy(r, d1_ref[0, 0, r]).start(priority=1)
        return carry

    def wait(r, carry):
        row_copy(0, 0).wait()
        row_copy(0, 0).wait()
        return carry

    lax.fori_loop(0, tile, start, 0, unroll=DMA_UNROLL)
    lax.fori_loop(0, tile, wait, 0, unroll=DMA_UNROLL)


def _dispatch_kernel(zero_ref, d0p_ref, d1p_ref, d0s_ref, d1s_ref, xp_ref, xs_ref, pad_out, zero_scr, sem, zsem,
                     *, tile, n_tiles):
    i = pl.program_id(0)

    @pl.when(i == 0)
    def _():
        zero_scr[...] = jnp.zeros_like(zero_scr)

        def zero_copy(e):
            row0 = pl.multiple_of(zero_ref[e], EXPERT_ROWS)
            return pltpu.make_async_copy(zero_scr, pad_out.at[pl.ds(row0, EXPERT_ROWS), :], zsem)

        for e in range(zero_ref.shape[0]):
            @pl.when(zero_ref[e] >= 0)
            def _():
                zero_copy(e).start()
        for e in range(zero_ref.shape[0]):
            @pl.when(zero_ref[e] >= 0)
            def _():
                zero_copy(e).wait()

    @pl.when(i < n_tiles)
    def _():
        _scatter_rows(d0p_ref, d1p_ref, xp_ref, pad_out, sem, tile)

    @pl.when(i == n_tiles)
    def _():
        _scatter_rows(d0s_ref, d1s_ref, xs_ref, pad_out, sem, xs_ref.shape[0])


def _dispatch(zero_rows, dest_p, dest_s, xp, xs, n_rows, tile):
    n_tiles = xp.shape[0] // tile
    n_s = xs.shape[0]
    last = n_tiles - 1
    tile_spec = pl.BlockSpec((1, 1, tile), lambda i: (jnp.minimum(i, last), 0, 0), memory_space=pltpu.SMEM)
    return pl.pallas_call(
        functools.partial(_dispatch_kernel, tile=tile, n_tiles=n_tiles),
        grid=(n_tiles + 1,),
        in_specs=[
            pl.BlockSpec(memory_space=pltpu.SMEM),
            tile_spec, tile_spec,
            pl.BlockSpec(memory_space=pltpu.SMEM),
            pl.BlockSpec(memory_space=pltpu.SMEM),
            pl.BlockSpec((tile, D_MODEL), lambda i: (jnp.minimum(i, last), 0)),
            pl.BlockSpec((n_s, D_MODEL), lambda i: (0, 0)),
        ],
        out_specs=pl.BlockSpec(memory_space=pl.ANY),
        out_shape=jax.ShapeDtypeStruct((n_rows, D_MODEL), F32),
        scratch_shapes=[pltpu.VMEM((EXPERT_ROWS, D_MODEL), F32), pltpu.SemaphoreType.DMA(()),
                        pltpu.SemaphoreType.DMA(())],
        compiler_params=_params("arbitrary"),
        name="dispatch",
    )(zero_rows, dest_p[0].reshape(n_tiles, 1, tile), dest_p[1].reshape(n_tiles, 1, tile),
      dest_s[0].reshape(1, 1, n_s), dest_s[1].reshape(1, 1, n_s), xp, xs)


def _expert_kernel(be_ref, nused_ref, x_ref, g_ref, wg_ref, wu_ref, wd_ref, y_ref, wg_s, wu_s, wd_s):
    i = pl.program_id(0)
    prev = be_ref[jnp.maximum(i - 1, 0)]
    active = i < nused_ref[0]

    @pl.when(jnp.logical_and(active, jnp.logical_or(i == 0, be_ref[i] != prev)))
    def _():
        wg_s[...] = wg_ref[...].astype(BF16)
        wu_s[...] = wu_ref[...].astype(BF16)
        wd_s[...] = wd_ref[...].astype(BF16)

    @pl.when(active)
    def _():
        hn = _rms(x_ref[...], g_ref[...]).astype(BF16)
        gate = jnp.dot(hn, wg_s[...], preferred_element_type=F32)
        up = jnp.dot(hn, wu_s[...], preferred_element_type=F32)
        act = (gate * jax.nn.sigmoid(gate)) * up
        y_ref[...] = jnp.dot(act.astype(BF16), wd_s[...], preferred_element_type=F32)

    @pl.when(jnp.logical_not(active))
    def _():
        y_ref[...] = jnp.zeros_like(y_ref)


def _experts(block_expert, n_used, x_pad, g_ffn, w_gate, w_up, w_down):
    n_rows = x_pad.shape[0]
    nb = n_rows // EXPERT_ROWS
    grid_spec = pltpu.PrefetchScalarGridSpec(
        num_scalar_prefetch=2,
        grid=(nb,),
        in_specs=[
            pl.BlockSpec((EXPERT_ROWS, D_MODEL), lambda i, be, nu: (i, 0)),
            pl.BlockSpec((1, D_MODEL), lambda i, be, nu: (0, 0)),
            pl.BlockSpec((None, None, D_MODEL, D_EXPERT), lambda i, be, nu: (0, be[i], 0, 0)),
            pl.BlockSpec((None, None, D_MODEL, D_EXPERT), lambda i, be, nu: (0, be[i], 0, 0)),
            pl.BlockSpec((None, None, D_EXPERT, D_MODEL), lambda i, be, nu: (0, be[i], 0, 0)),
        ],
        out_specs=pl.BlockSpec((EXPERT_ROWS, D_MODEL), lambda i, be, nu: (i, 0)),
        scratch_shapes=[
            pltpu.VMEM((D_MODEL, D_EXPERT), BF16),
            pltpu.VMEM((D_MODEL, D_EXPERT), BF16),
            pltpu.VMEM((D_EXPERT, D_MODEL), BF16),
        ],
    )
    return pl.pallas_call(
        _expert_kernel,
        grid_spec=grid_spec,
        out_shape=jax.ShapeDtypeStruct((n_rows, D_MODEL), F32),
        compiler_params=_params("arbitrary"),
        name="experts",
    )(block_expert, n_used, x_pad, g_ffn, w_gate, w_up, w_down)


def _final_kernel(d0_ref, d1_ref, d0n_ref, d1n_ref, x_ref, info_ref, p_ref, y_hbm, wpg_ref, wpp_ref, gple_ref,
                  gfin_ref, out_ref, ybuf, sem):
    tm = x_ref.shape[0]
    i = pl.program_id(0)
    buf = i % 2

    def row_copy(src_row, b, k, r):
        return pltpu.make_async_copy(y_hbm.at[pl.ds(src_row, 1), :], ybuf.at[b, k, pl.ds(r, 1), :], sem.at[b])

    def gather(da_ref, db_ref, b):
        def start(r, carry):
            row_copy(da_ref[0, 0, r], b, 0, r).start(priority=0)
            row_copy(db_ref[0, 0, r], b, 1, r).start(priority=1)
            return carry
        lax.fori_loop(0, tm, start, 0, unroll=DMA_UNROLL)

    def wait_all(b):
        def wait(r, carry):
            row_copy(0, b, 0, 0).wait()
            row_copy(0, b, 1, 0).wait()
            return carry
        lax.fori_loop(0, tm, wait, 0, unroll=DMA_UNROLL)

    @pl.when(i == 0)
    def _():
        gather(d0_ref, d1_ref, 0)

    wait_all(buf)
    for r in range(tm):
        row_copy(d0n_ref[0, 0, r], 1 - buf, 0, r).start()
        row_copy(d1n_ref[0, 0, r], 1 - buf, 1, r).start()

    info = info_ref[...]
    x2 = x_ref[...] + (ybuf[buf, 0] * info[:, 2:3] + ybuf[buf, 1] * info[:, 3:4])
    gate = jax.nn.sigmoid(jnp.dot(_rms(x2, gple_ref[...]).astype(BF16), wpg_ref[...], preferred_element_type=F32))
    proj = jnp.dot(p_ref[...].astype(BF16), wpp_ref[...], preferred_element_type=F32)
    out_ref[...] = _rms(x2 + gate * proj, gfin_ref[...])

    @pl.when(i == pl.num_programs(0) - 1)
    def _():
        wait_all(1 - buf)


def _final(dest, x1, info, p, y_pad, w_pg, w_pp, g_ple, g_final, tm):
    n = x1.shape[0]
    d0 = dest[0].reshape(n // tm, 1, tm)
    d1 = dest[1].reshape(n // tm, 1, tm)
    const = lambda i: (0, 0)
    last = n // tm - 1
    cur = pl.BlockSpec((1, 1, tm), lambda i: (i, 0, 0), memory_space=pltpu.SMEM)
    nxt = pl.BlockSpec((1, 1, tm), lambda i: (jnp.minimum(i + 1, last), 0, 0), memory_space=pltpu.SMEM)
    return pl.pallas_call(
        _final_kernel,
        grid=(n // tm,),
        in_specs=[
            cur, cur, nxt, nxt,
            pl.BlockSpec((tm, D_MODEL), lambda i: (i, 0)),
            pl.BlockSpec((tm, LANES), lambda i: (i, 0)),
            pl.BlockSpec((tm, PLE_DIM), lambda i: (i, 0)),
            pl.BlockSpec(memory_space=pl.ANY),
            pl.BlockSpec((D_MODEL, D_MODEL), const),
            pl.BlockSpec((PLE_DIM, D_MODEL), const),
            pl.BlockSpec((1, D_MODEL), const),
            pl.BlockSpec((1, D_MODEL), const),
        ],
        out_specs=pl.BlockSpec((tm, D_MODEL), lambda i: (i, 0)),
        out_shape=jax.ShapeDtypeStruct((n, D_MODEL), F32),
        scratch_shapes=[pltpu.VMEM((2, 2, tm, D_MODEL), F32), pltpu.SemaphoreType.DMA((2,))],
        compiler_params=_params("arbitrary"),
        name="final",
    )(d0, d1, d0, d1, x1, info, p, y_pad, w_pg, w_pp, g_ple, g_final)


def _pad_lanes(a):
    return jnp.pad(a, ((0, 0), (0, LANES - a.shape[1])))


def kernel(x_prompt, x_sample, p_prompt, p_sample, state_mlstm_C, state_mlstm_n, state_mlstm_m, state_swa_k, state_swa_v, g_mix, w_in, b_igate, b_fgate, g_mh, attn_sinks, w_out, g_ffn, w_router_g, b_router_g, w_router_e, b_router_e, w_gate, w_up, w_down, g_ple, w_ple_gate, w_ple_proj, g_final):
    batch, seq, d = x_prompt.shape
    dec = x_sample.shape[0]
    n_p = batch * seq
    xp = x_prompt.reshape(n_p, d)
    xs = x_sample.reshape(dec, d)

    w_in0 = w_in[0]
    head_order = jnp.array(SWA_HEAD_ORDER, jnp.int32)
    w_aq = w_in0[:, ATT_OFF:ATT_OFF + A_Q_W].reshape(d, A_HEADS, A_HEAD_DIM)[:, head_order].reshape(d, A_Q_W)
    zf_pad = ZF_W - (ZF_GATE + LANES)
    w_cat = jnp.concatenate(
        [w_in0[:, :3 * M_W], w_aq,
         w_in0[:, 3 * M_W:GATE_OFF], w_in0[:, ATT_OFF + A_Q_W:],
         _pad_lanes(w_in0[:, GATE_OFF:ATT_OFF]), jnp.zeros((d, zf_pad), F32)],
        axis=1).astype(BF16)
    gate_bias8 = jnp.concatenate([b_igate, b_fgate], axis=1)
    gate_bias = _pad_lanes(gate_bias8)
    w_oa = w_out[0, M_W:].reshape(A_HEADS, A_HEAD_DIM, d)[head_order].reshape(A_Q_W, d)
    w_o = jnp.concatenate([w_out[0, :M_W], w_oa], axis=0).astype(BF16)
    kv_of_head = jnp.arange(A_HEADS, dtype=jnp.int32) // A_GROUP
    head_mask = (kv_of_head[:, None] == jnp.arange(A_KV_HEADS, dtype=jnp.int32)[None, :]).astype(F32)
    gap = EXPERT_LANE0 - N_GROUPS
    w_r = _pad_lanes(jnp.concatenate([w_router_g[0], jnp.zeros((d, gap), F32), w_router_e[0]], axis=1))
    b_r = _pad_lanes(jnp.concatenate([b_router_g, jnp.zeros((1, gap), F32), b_router_e], axis=1))
    w_pg = w_ple_gate[0].astype(BF16)
    w_pp = w_ple_proj[0].astype(BF16)
    sink_col = attn_sinks.reshape(A_HEADS, 1)

    zb, zf = _inproj_split(xp, g_mix, w_cat, tm=1024, tn=1024)
    om_p, c_p, n_p8, m_p8 = _mlstm_prompt(zb, zf, gate_bias, g_mh, batch, seq)
    oa_p = _swa_prompt(zb, zf, attn_sinks, batch, seq)
    x1p = _outproj(xp, om_p, oa_p, w_o, tm=512)

    zs = _inproj(xs, g_mix, w_in0, tm=dec, tn=512, precision=HI)
    om_s, c_s, n_s, m_s = _mlstm_sample(
        zs.reshape(dec, 1, zs.shape[1]), gate_bias8,
        state_mlstm_C[0], state_mlstm_n[0].reshape(dec, 1, M_W), state_mlstm_m[0].reshape(dec, 1, M_HEADS), g_mh)
    ks_off = ATT_OFF + A_Q_W
    vs_off = ks_off + A_KV_W
    q_heads = zs[:, ATT_OFF:ks_off].reshape(dec, A_HEADS, 1, A_HEAD_DIM)
    qx = (q_heads * head_mask[None, :, :, None]).reshape(dec, A_HEADS, A_KV_W)
    ox_s, k_s, v_s = _swa_sample(
        qx,
        zs[:, ks_off:vs_off].reshape(dec, 1, A_KV_W),
        zs[:, vs_off:].reshape(dec, 1, A_KV_W),
        state_swa_k[0].reshape(dec, WINDOW, A_KV_W), state_swa_v[0].reshape(dec, WINDOW, A_KV_W), sink_col)
    oa_s = jnp.sum(ox_s.reshape(dec, A_HEADS, A_KV_HEADS, A_HEAD_DIM) * head_mask[None, :, :, None], axis=2)
    x1s = _outproj(xs, om_s.reshape(dec, M_W), oa_s.reshape(dec, A_Q_W), w_out[0], tm=dec, precision=HI)

    info_p, rows_p, cnt_p = _route(x1p, g_ffn, w_r, b_r, jnp.zeros((N_EXPERTS, LANES), F32), tm=512)
    info_s, rows_s, cnt = _route(x1s, g_ffn, w_r, b_r, cnt_p, tm=dec)
    counts = cnt[:, 0].astype(jnp.int32)
    padded = ((counts + EXPERT_ROWS - 1) // EXPERT_ROWS) * EXPERT_ROWS
    pad_end = jnp.cumsum(padded)
    pad_start = pad_end - padded
    n_assign = 2 * (n_p + dec)
    n_blocks = -(-n_assign // EXPERT_ROWS) + N_EXPERTS
    block_row0 = jnp.arange(n_blocks, dtype=jnp.int32) * EXPERT_ROWS
    block_expert = jnp.minimum(
        jnp.sum((pad_end[None, :] <= block_row0[:, None]).astype(jnp.int32), axis=1), N_EXPERTS - 1)
    n_used = (pad_end[-1:] // EXPERT_ROWS).astype(jnp.int32)
    spare = n_used + jnp.arange(N_EXPERTS, dtype=jnp.int32)
    zero_rows = jnp.concatenate([
        jnp.where(padded > 0, pad_end - EXPERT_ROWS, -1),
        jnp.where(spare < n_blocks, spare * EXPERT_ROWS, -1)]).astype(jnp.int32)

    def dest_of(rows):
        e = rows[0:2].astype(jnp.int32)
        hit = e[:, :, None] == jnp.arange(N_EXPERTS, dtype=jnp.int32)
        return jnp.sum(jnp.where(hit, pad_start, 0), axis=2) + rows[4:6].astype(jnp.int32)

    dest_p = dest_of(rows_p)
    dest_s = dest_of(rows_s)

    x_pad = _dispatch(zero_rows, dest_p, dest_s, x1p, x1s, n_blocks * EXPERT_ROWS, tile=512)
    y_pad = _experts(block_expert, n_used, x_pad, g_ffn, w_gate, w_up, w_down)
    y_p = _final(dest_p, x1p, info_p, p_prompt[0].reshape(n_p, PLE_DIM), y_pad, w_pg, w_pp, g_ple, g_final.reshape(1, d), tm=256)
    y_s = _final(dest_s, x1s, info_s, p_sample[0].reshape(dec, PLE_DIM), y_pad, w_pg, w_pp, g_ple, g_final.reshape(1, d), tm=dec)

    zf3 = zf.reshape(batch, seq, ZF_W)
    swa_k_p = zf3[:, seq - WINDOW:, ZF_K:ZF_V].reshape(1, batch, WINDOW, A_KV_HEADS, A_HEAD_DIM)
    swa_v_p = zf3[:, seq - WINDOW:, ZF_V:ZF_GATE].reshape(1, batch, WINDOW, A_KV_HEADS, A_HEAD_DIM)
    return (
        y_p.reshape(batch, seq, d),
        y_s.reshape(dec, 1, d),
        c_p[None],
        n_p8[None, :, :M_HEADS, :],
        m_p8[None, :, :M_HEADS, 0],
        swa_k_p,
        swa_v_p,
        c_s[None],
        n_s.reshape(1, dec, M_HEADS, M_DK),
        m_s[None, :, 0, :M_HEADS],
        k_s.reshape(1, dec, WINDOW, A_KV_HEADS, A_HEAD_DIM),
        v_s.reshape(1, dec, WINDOW, A_KV_HEADS, A_HEAD_DIM),
    )
```

```python
import functools

import jax
import jax.numpy as jnp
from jax import lax
from jax.experimental import pallas as pl
from jax.experimental.pallas import tpu as pltpu

F32 = jnp.float32
BF16 = jnp.bfloat16
HI = lax.Precision.HIGHEST

D_MODEL = 2048
M_HEADS = 4
M_DK = 256
M_DV = 256
A_HEADS = 16
A_KV_HEADS = 4
A_HEAD_DIM = 64
A_GROUP = A_HEADS // A_KV_HEADS
WINDOW = 128
M_W = M_HEADS * M_DK
A_Q_W = A_HEADS * A_HEAD_DIM
A_KV_W = A_KV_HEADS * A_HEAD_DIM
GATE_OFF = 4 * M_W
ATT_OFF = GATE_OFF + 2 * M_HEADS
ZB_W = 3 * M_W + A_Q_W
ZF_MO, ZF_K, ZF_V, ZF_GATE = 0, M_W, M_W + A_KV_W, M_W + 2 * A_KV_W
ZF_W = 2 * M_W
N_GROUPS = 4
EXPERTS_PER_GROUP = 8
N_EXPERTS = N_GROUPS * EXPERTS_PER_GROUP
D_EXPERT = 512
PLE_DIM = 256
EPS = 1e-6

LANES = 128
MLSTM_CHUNK = 256
EXPERT_ROWS = 256
VMEM_LIMIT = 56 * 1024 * 1024
NT_DIMS = (((1,), (1,)), ((), ()))
TN_DIMS = (((0,), (0,)), ((), ()))


def _params(*sem):
    return pltpu.CompilerParams(dimension_semantics=sem, vmem_limit_bytes=VMEM_LIMIT)


def _rms(x, g):
    r = lax.rsqrt(jnp.mean(x * x, axis=-1, keepdims=True) + EPS)
    return (x * r) * g


def _log_sigmoid(x):
    return jnp.minimum(x, 0.0) - jnp.log1p(jnp.exp(-jnp.abs(x)))


def _inproj_kernel(x_ref, g_ref, w_ref, z_ref, hn_scr, *, precision):
    @pl.when(pl.program_id(1) == 0)
    def _():
        hn_scr[...] = _rms(x_ref[...], g_ref[...]).astype(hn_scr.dtype)

    z_ref[...] = jnp.dot(hn_scr[...], w_ref[...], precision=precision, preferred_element_type=F32)


def _inproj(x, g_mix, w, tm, tn, precision=None):
    n = x.shape[0]
    width = w.shape[1]
    return pl.pallas_call(
        functools.partial(_inproj_kernel, precision=precision),
        grid=(n // tm, pl.cdiv(width, tn)),
        in_specs=[
            pl.BlockSpec((tm, D_MODEL), lambda i, j: (i, 0)),
            pl.BlockSpec((1, D_MODEL), lambda i, j: (0, 0)),
            pl.BlockSpec((D_MODEL, tn), lambda i, j: (0, j)),
        ],
        out_specs=pl.BlockSpec((tm, tn), lambda i, j: (i, j)),
        out_shape=jax.ShapeDtypeStruct((n, width), F32),
        scratch_shapes=[pltpu.VMEM((tm, D_MODEL), w.dtype)],
        compiler_params=_params("parallel", "arbitrary"),
        name="inproj",
    )(x, g_mix, w)


def _inproj_split_kernel(x_ref, g_ref, w_ref, zb_ref, zf_ref, hn_scr, *, n_bf16_blocks):
    j = pl.program_id(1)

    @pl.when(j == 0)
    def _():
        hn_scr[...] = _rms(x_ref[...], g_ref[...]).astype(BF16)

    acc = jnp.dot(hn_scr[...], w_ref[...], preferred_element_type=F32)

    @pl.when(j < n_bf16_blocks)
    def _():
        zb_ref[...] = acc.astype(BF16)

    @pl.when(j >= n_bf16_blocks)
    def _():
        zf_ref[...] = acc


def _inproj_split(x, g_mix, w, tm, tn):
    n = x.shape[0]
    nb = ZB_W // tn
    nf = ZF_W // tn
    return pl.pallas_call(
        functools.partial(_inproj_split_kernel, n_bf16_blocks=nb),
        grid=(n // tm, nb + nf),
        in_specs=[
            pl.BlockSpec((tm, D_MODEL), lambda i, j: (i, 0)),
            pl.BlockSpec((1, D_MODEL), lambda i, j: (0, 0)),
            pl.BlockSpec((D_MODEL, tn), lambda i, j: (0, j)),
        ],
        out_specs=[
            pl.BlockSpec((tm, tn), lambda i, j: (i, jnp.minimum(j, nb - 1))),
            pl.BlockSpec((tm, tn), lambda i, j: (i, jnp.maximum(j - nb, 0))),
        ],
        out_shape=[
            jax.ShapeDtypeStruct((n, ZB_W), BF16),
            jax.ShapeDtypeStruct((n, ZF_W), F32),
        ],
        scratch_shapes=[pltpu.VMEM((tm, D_MODEL), BF16)],
        compiler_params=_params("parallel", "arbitrary"),
        name="inproj_prompt",
    )(x, g_mix, w)


def _mlstm_out(h, gmh, mo):
    hn = h * lax.rsqrt(jnp.mean(h * h, axis=-1, keepdims=True) + EPS)
    return (hn * gmh) * jax.nn.sigmoid(mo)


def _gates(g_raw, bias):
    g = g_raw + bias
    lane = lax.broadcasted_iota(jnp.int32, g.shape, g.ndim - 1)
    return jnp.where(lane < M_HEADS, g, _log_sigmoid(g))


def _mlstm_kernel(q_ref, k_ref, v_ref, mo_ref, g_ref, bias_ref, gmh_ref,
                  o_ref, c_out, n_out, m_out, c_scr, n_scr, m_scr, *, chunk, n_chunks):
    c_idx = pl.program_id(1)

    @pl.when(c_idx == 0)
    def _():
        c_scr[...] = jnp.zeros_like(c_scr)
        n_scr[...] = jnp.zeros_like(n_scr)
        m_scr[...] = jnp.zeros_like(m_scr)

    L = chunk
    row = lax.broadcasted_iota(jnp.int32, (L, L), 0)
    col = lax.broadcasted_iota(jnp.int32, (L, L), 1)
    causal = row >= col
    ltri = causal.astype(F32)
    utri = (row <= col).astype(F32)
    g_c = _gates(g_ref[...], bias_ref[...])
    f_c = jnp.dot(ltri, g_c, precision=HI, preferred_element_type=F32)
    g_r = g_c.T[0:2 * M_HEADS, :]
    f_r = jnp.dot(g_r, utri, precision=HI, preferred_element_type=F32)

    for h in range(M_HEADS):
        hs = slice(h * M_DK, (h + 1) * M_DK)
        hf = M_HEADS + h
        a_row = g_r[h:h + 1, :] - f_r[hf:hf + 1, :]
        f_col = f_c[:, hf:hf + 1]
        a_col = g_c[:, h:h + 1] - f_col
        m0 = m_scr[h:h + 1, 0:1]
        a_max = jnp.max(jnp.where(causal, a_row, -jnp.inf), axis=1, keepdims=True)
        m_t = f_col + jnp.maximum(m0, a_max)
        dec0 = jnp.exp(f_col + m0 - m_t)
        d = jnp.exp(jnp.where(causal, (f_col - m_t) + a_row, -jnp.inf))
        qb = q_ref[:, hs]
        kb = k_ref[:, hs] * (M_DK ** -0.5)
        vb = v_ref[:, hs]
        q = qb.astype(F32)
        k = kb.astype(F32)
        w = lax.dot_general(qb, kb, NT_DIMS, preferred_element_type=F32) * d
        c0 = c_scr[h]
        n0 = n_scr[h:h + 1, :]
        num = jnp.dot(w.astype(BF16), vb, preferred_element_type=F32) + dec0 * lax.dot_general(
            qb, c0.astype(BF16), NT_DIMS, preferred_element_type=F32)
        den = jnp.sum(w, axis=1, keepdims=True) + dec0 * jnp.sum(q * n0, axis=1, keepdims=True)
        hh = num / jnp.maximum(jnp.abs(den), jnp.exp(-m_t))
        m_end = m_t[L - 1:L, :]
        f_end = f_col[L - 1:L, :]
        w_end = jnp.exp(f_end + a_col - m_end)
        dec_end = jnp.exp(f_end + m0 - m_end)
        kw = k * w_end
        c_scr[h] = dec_end * c0 + lax.dot_general(vb, kw.astype(BF16), TN_DIMS, preferred_element_type=F32)
        n_scr[h:h + 1, :] = dec_end * n0 + jnp.sum(kw, axis=0, keepdims=True)
        m_scr[h:h + 1, :] = jnp.broadcast_to(m_end, (1, LANES))
        o_ref[:, hs] = _mlstm_out(hh, gmh_ref[:, hs], mo_ref[:, hs]).astype(o_ref.dtype)

    @pl.when(c_idx == n_chunks - 1)
    def _():
        c_out[0] = c_scr[...]
        n_out[0] = n_scr[...]
        m_out[0] = m_scr[...]


def _mlstm_prompt(zb, zf, gate_bias, g_mh, batch, seq):
    L = MLSTM_CHUNK
    nc = seq // L
    rows = lambda b, c: b * nc + c
    gate_blk = ZF_GATE // LANES
    return pl.pallas_call(
        functools.partial(_mlstm_kernel, chunk=L, n_chunks=nc),
        grid=(batch, nc),
        in_specs=[
            pl.BlockSpec((L, M_W), lambda b, c: (rows(b, c), 0)),
            pl.BlockSpec((L, M_W), lambda b, c: (rows(b, c), 1)),
            pl.BlockSpec((L, M_W), lambda b, c: (rows(b, c), 2)),
            pl.BlockSpec((L, M_W), lambda b, c: (rows(b, c), ZF_MO // M_W)),
            pl.BlockSpec((L, LANES), lambda b, c: (rows(b, c), gate_blk)),
            pl.BlockSpec((1, LANES), lambda b, c: (0, 0)),
            pl.BlockSpec((1, M_W), lambda b, c: (0, 0)),
        ],
        out_specs=[
            pl.BlockSpec((L, M_W), lambda b, c: (rows(b, c), 0)),
            pl.BlockSpec((1, M_HEADS, M_DV, M_DK), lambda b, c: (b, 0, 0, 0)),
            pl.BlockSpec((1, 8, M_DK), lambda b, c: (b, 0, 0)),
            pl.BlockSpec((1, 8, LANES), lambda b, c: (b, 0, 0)),
        ],
        out_shape=[
            jax.ShapeDtypeStruct((batch * seq, M_W), BF16),
            jax.ShapeDtypeStruct((batch, M_HEADS, M_DV, M_DK), F32),
            jax.ShapeDtypeStruct((batch, 8, M_DK), F32),
            jax.ShapeDtypeStruct((batch, 8, LANES), F32),
        ],
        scratch_shapes=[
            pltpu.VMEM((M_HEADS, M_DV, M_DK), F32),
            pltpu.VMEM((8, M_DK), F32),
            pltpu.VMEM((8, LANES), F32),
        ],
        compiler_params=_params("parallel", "arbitrary"),
        name="mlstm_prompt",
    )(zb, zb, zb, zf, zf, gate_bias, g_mh)


MLSTM_STEP_TOKENS = 2


def _mlstm_step_kernel(z_ref, bias_ref, c_ref, n_ref, m_ref, gmh_ref, o_ref, c_out, n_out, m_out):
    row = lax.broadcasted_iota(jnp.int32, (M_DV, M_DV), 0)
    col = lax.broadcasted_iota(jnp.int32, (M_DV, M_DV), 1)
    eye = row == col
    for t in range(z_ref.shape[0]):
        _mlstm_step_token(t, eye, z_ref, bias_ref, c_ref, n_ref, m_ref, gmh_ref, o_ref, c_out, n_out, m_out)


def _mlstm_step_token(t, eye, z_ref, bias_ref, c_ref, n_ref, m_ref, gmh_ref, o_ref, c_out, n_out, m_out):
    z = z_ref[t]
    gates = _gates(z[:, GATE_OFF:ATT_OFF], bias_ref[...])
    m_in = m_ref[t]
    lane = lax.broadcasted_iota(jnp.int32, (1, LANES), 1)
    m_new = jnp.zeros((1, LANES), F32)
    for h in range(M_HEADS):
        hs = slice(h * M_DK, (h + 1) * M_DK)
        q = z[:, h * M_DK:(h + 1) * M_DK]
        k = z[:, M_W + h * M_DK:M_W + (h + 1) * M_DK] * (M_DK ** -0.5)
        v = z[:, 2 * M_W + h * M_DV:2 * M_W + (h + 1) * M_DV]
        mo = z[:, 3 * M_W + h * M_DV:3 * M_W + (h + 1) * M_DV]
        ip = gates[:, h:h + 1]
        lf = gates[:, M_HEADS + h:M_HEADS + h + 1]
        m0 = m_in[:, h:h + 1]
        m_t = jnp.maximum(lf + m0, ip)
        dec0 = jnp.exp(lf + m0 - m_t)
        d = jnp.exp(ip - m_t)
        c0 = c_ref[t, h]
        n0 = n_ref[t, :, hs]
        w = jnp.sum(q * k, axis=1, keepdims=True) * d
        cq_col = jnp.sum(c0 * q, axis=1, keepdims=True)
        cq = jnp.sum(jnp.where(eye, cq_col, 0.0), axis=0, keepdims=True)
        num = w * v + dec0 * cq
        den = w + dec0 * jnp.sum(n0 * q, axis=1, keepdims=True)
        hh = num / jnp.maximum(jnp.abs(den), jnp.exp(-m_t))
        v_col = jnp.sum(jnp.where(eye, v, 0.0), axis=1, keepdims=True)
        c_out[t, h] = dec0 * c0 + (d * v_col) * k
        n_out[t, :, hs] = dec0 * n0 + d * k
        m_new = jnp.where(lane == h, m_t, m_new)
        o_ref[t, :, hs] = _mlstm_out(hh, gmh_ref[:, hs], mo).astype(o_ref.dtype)
    m_out[t] = m_new


def _mlstm_sample(z3, gate_bias, c0, n0, m0, g_mh):
    nb = z3.shape[0]
    nt = MLSTM_STEP_TOKENS
    tok = lambda b: (b, 0, 0)
    return pl.pallas_call(
        _mlstm_step_kernel,
        grid=(nb // nt,),
        in_specs=[
            pl.BlockSpec((nt, 1, z3.shape[2]), tok),
            pl.BlockSpec((1, 2 * M_HEADS), lambda b: (0, 0)),
            pl.BlockSpec((nt, M_HEADS, M_DV, M_DK), lambda b: (b, 0, 0, 0)),
            pl.BlockSpec((nt, 1, M_W), tok),
            pl.BlockSpec((nt, 1, M_HEADS), tok),
            pl.BlockSpec((1, M_W), lambda b: (0, 0)),
        ],
        out_specs=[
            pl.BlockSpec((nt, 1, M_W), tok),
            pl.BlockSpec((nt, M_HEADS, M_DV, M_DK), lambda b: (b, 0, 0, 0)),
            pl.BlockSpec((nt, 1, M_W), tok),
            pl.BlockSpec((nt, 1, LANES), tok),
        ],
        out_shape=[
            jax.ShapeDtypeStruct((nb, 1, M_W), F32),
            jax.ShapeDtypeStruct((nb, M_HEADS, M_DV, M_DK), F32),
            jax.ShapeDtypeStruct((nb, 1, M_W), F32),
            jax.ShapeDtypeStruct((nb, 1, LANES), F32),
        ],
        compiler_params=_params("parallel"),
        name="mlstm_sample",
    )(z3, gate_bias, c0, n0, m0, g_mh)


_EVEN_KV = [h for h in range(A_HEADS) if (h // A_GROUP) % 2 == 0]
_ODD_KV = [h for h in range(A_HEADS) if (h // A_GROUP) % 2 == 1]
SWA_HEAD_ORDER = tuple(h for pair in zip(_EVEN_KV, _ODD_KV) for h in pair)
HALF = LANES // 2
KV_SPAN = 2 * WINDOW


def _swa_kernel(sink_ref, q_ref, kc_ref, kp_ref, vc_ref, vp_ref, o_ref, qx_scr, p_scr):
    j = pl.program_id(1)
    lo_half = lax.broadcasted_iota(jnp.int32, (WINDOW, LANES), 1) < HALF
    zero_tile = jnp.zeros((WINDOW, LANES), BF16)
    for slot, head in enumerate(SWA_HEAD_ORDER):
        kv = head // A_GROUP
        tile = q_ref[:, (slot // 2) * LANES:(slot // 2 + 1) * LANES]
        keep = lo_half if slot % 2 == 0 else jnp.logical_not(lo_half)
        qm = jnp.where(keep, tile * (A_HEAD_DIM ** -0.5), jnp.zeros_like(tile))
        parts = [qm, zero_tile] if kv // 2 == 0 else [zero_tile, qm]
        qx_scr[slot * WINDOW:(slot + 1) * WINDOW, :] = jnp.concatenate(parts, axis=1)

    k2 = jnp.concatenate([kp_ref[...], kc_ref[...]], axis=0).astype(BF16)
    v2 = jnp.concatenate([vp_ref[...], vc_ref[...]], axis=0).astype(BF16)
    s_all = lax.dot_general(qx_scr[...], k2, NT_DIMS, preferred_element_type=F32)

    t = lax.broadcasted_iota(jnp.int32, (WINDOW, KV_SPAN), 0)
    kk = lax.broadcasted_iota(jnp.int32, (WINDOW, KV_SPAN), 1)
    in_prev = jnp.logical_and(jnp.logical_and(kk < WINDOW, kk > t), j > 0)
    in_cur = jnp.logical_and(kk >= WINDOW, kk - WINDOW <= t)
    allowed = jnp.logical_or(in_prev, in_cur)
    for slot, head in enumerate(SWA_HEAD_ORDER):
        rows = slice(slot * WINDOW, (slot + 1) * WINDOW)
        s = jnp.where(allowed, s_all[rows, :], -jnp.inf)
        sink = sink_ref[0, head]
        mx = jnp.maximum(jnp.max(s, axis=1, keepdims=True), sink)
        e = jnp.exp(s - mx)
        den = jnp.sum(e, axis=1, keepdims=True) + jnp.exp(sink - mx)
        p_scr[rows, :] = (e * (1.0 / den)).astype(BF16)

    o_all = jnp.dot(p_scr[...], v2, preferred_element_type=F32)
    for c in range(A_HEADS // 2):
        tiles = []
        for slot in (2 * c, 2 * c + 1):
            kv = SWA_HEAD_ORDER[slot] // A_GROUP
            tiles.append(o_all[slot * WINDOW:(slot + 1) * WINDOW, (kv // 2) * LANES:(kv // 2 + 1) * LANES])
        o_ref[:, c * LANES:(c + 1) * LANES] = jnp.where(lo_half, tiles[0], tiles[1]).astype(o_ref.dtype)


def _swa_prompt(zb, zf, sinks, batch, seq):
    nb = seq // WINDOW
    q_blk = (3 * M_W) // A_Q_W
    k_blk = ZF_K // A_KV_W
    cur = lambda b, j: b * nb + j
    prev = lambda b, j: b * nb + jnp.maximum(j - 1, 0)
    return pl.pallas_call(
        _swa_kernel,
        grid=(batch, nb),
        in_specs=[
            pl.BlockSpec(memory_space=pltpu.SMEM),
            pl.BlockSpec((WINDOW, A_Q_W), lambda b, j: (cur(b, j), q_blk)),
            pl.BlockSpec((WINDOW, A_KV_W), lambda b, j: (cur(b, j), k_blk)),
            pl.BlockSpec((WINDOW, A_KV_W), lambda b, j: (prev(b, j), k_blk)),
            pl.BlockSpec((WINDOW, A_KV_W), lambda b, j: (cur(b, j), k_blk + 1)),
            pl.BlockSpec((WINDOW, A_KV_W), lambda b, j: (prev(b, j), k_blk + 1)),
        ],
        out_specs=pl.BlockSpec((WINDOW, A_Q_W), lambda b, j: (cur(b, j), 0)),
        out_shape=jax.ShapeDtypeStruct((batch * seq, A_Q_W), BF16),
        scratch_shapes=[pltpu.VMEM((A_HEADS * WINDOW, A_KV_W), BF16), pltpu.VMEM((A_HEADS * WINDOW, KV_SPAN), BF16)],
        compiler_params=_params("parallel", "arbitrary"),
        name="swa_prompt",
    )(sinks, zb, zf, zf, zf, zf)


SWA_STEP_TOKENS = 4


def _swa_step_kernel(qx_ref, kn_ref, vn_ref, bk_ref, bv_ref, sink_ref, o_ref, ko_ref, vo_ref):
    in_window = lax.broadcasted_iota(jnp.int32, (1, WINDOW), 1) >= 1
    scale = A_HEAD_DIM ** -0.5
    sink = sink_ref[...]
    for t in range(qx_ref.shape[0]):
        qx = qx_ref[t]
        bk = bk_ref[t]
        bv = bv_ref[t]
        kn = kn_ref[t]
        vn = vn_ref[t]
        s = lax.dot_general(qx, bk, NT_DIMS, precision=HI, preferred_element_type=F32) * scale
        s = jnp.where(in_window, s, -jnp.inf)
        s_new = jnp.sum(qx * kn, axis=1, keepdims=True) * scale
        mx = jnp.maximum(jnp.maximum(jnp.max(s, axis=1, keepdims=True), s_new), sink)
        e = jnp.exp(s - mx)
        e_new = jnp.exp(s_new - mx)
        den = jnp.sum(e, axis=1, keepdims=True) + e_new + jnp.exp(sink - mx)
        o_ref[t] = jnp.dot(e / den, bv, precision=HI, preferred_element_type=F32) + (e_new / den) * vn
        ko_ref[t, 0:WINDOW - 1, :] = bk[1:WINDOW, :]
        ko_ref[t, WINDOW - 1:WINDOW, :] = kn
        vo_ref[t, 0:WINDOW - 1, :] = bv[1:WINDOW, :]
        vo_ref[t, WINDOW - 1:WINDOW, :] = vn


def _swa_sample(qx, kn3, vn3, buf_k, buf_v, sink_col):
    nb = qx.shape[0]
    nt = SWA_STEP_TOKENS
    tok = lambda b: (b, 0, 0)
    return pl.pallas_call(
        _swa_step_kernel,
        grid=(nb // nt,),
        in_specs=[
            pl.BlockSpec((nt, A_HEADS, A_KV_W), tok),
            pl.BlockSpec((nt, 1, A_KV_W), tok),
            pl.BlockSpec((nt, 1, A_KV_W), tok),
            pl.BlockSpec((nt, WINDOW, A_KV_W), tok),
            pl.BlockSpec((nt, WINDOW, A_KV_W), tok),
            pl.BlockSpec((A_HEADS, 1), lambda b: (0, 0)),
        ],
        out_specs=[
            pl.BlockSpec((nt, A_HEADS, A_KV_W), tok),
            pl.BlockSpec((nt, WINDOW, A_KV_W), tok),
            pl.BlockSpec((nt, WINDOW, A_KV_W), tok),
        ],
        out_shape=[
            jax.ShapeDtypeStruct((nb, A_HEADS, A_KV_W), F32),
            jax.ShapeDtypeStruct((nb, WINDOW, A_KV_W), F32),
            jax.ShapeDtypeStruct((nb, WINDOW, A_KV_W), F32),
        ],
        compiler_params=_params("parallel"),
        name="swa_sample",
    )(qx, kn3, vn3, buf_k, buf_v, sink_col)


GROUP_LANE0 = 0
EXPERT_LANE0 = 8


def _outproj_kernel(x_ref, om_ref, oa_ref, wm_ref, wa_ref, y_ref, *, precision):
    y_ref[...] = (x_ref[...]
                  + jnp.dot(om_ref[...], wm_ref[...], precision=precision, preferred_element_type=F32)
                  + jnp.dot(oa_ref[...], wa_ref[...], precision=precision, preferred_element_type=F32))


def _outproj(x, o_m, o_a, w, tm, precision=None):
    n = x.shape[0]
    return pl.pallas_call(
        functools.partial(_outproj_kernel, precision=precision),
        grid=(n // tm,),
        in_specs=[
            pl.BlockSpec((tm, D_MODEL), lambda i: (i, 0)),
            pl.BlockSpec((tm, M_W), lambda i: (i, 0)),
            pl.BlockSpec((tm, A_Q_W), lambda i: (i, 0)),
            pl.BlockSpec((M_W, D_MODEL), lambda i: (0, 0)),
            pl.BlockSpec((A_Q_W, D_MODEL), lambda i: (1, 0)),
        ],
        out_specs=pl.BlockSpec((tm, D_MODEL), lambda i: (i, 0)),
        out_shape=jax.ShapeDtypeStruct((n, D_MODEL), F32),
        compiler_params=_params("parallel"),
        name="outproj",
    )(x, o_m, o_a, w, w)


def _route_kernel(x_ref, g_ref, wr_ref, br_ref, cin_ref, info_ref, rows_ref, cnt_ref, carry):
    @pl.when(pl.program_id(0) == 0)
    def _():
        carry[...] = cin_ref[...]

    tm = x_ref.shape[0]
    hn = _rms(x_ref[...], g_ref[...])
    w_r = wr_ref[...]
    hn_hi = hn.astype(BF16)
    hn_lo = (hn - hn_hi.astype(F32)).astype(BF16)
    w_hi = w_r.astype(BF16)
    w_lo = (w_r - w_hi.astype(F32)).astype(BF16)
    logits = (jnp.dot(hn_hi, w_hi, preferred_element_type=F32)
              + (jnp.dot(hn_lo, w_hi, preferred_element_type=F32)
                 + jnp.dot(hn_hi, w_lo, preferred_element_type=F32))) + br_ref[...]
    lt = logits.T
    neg = -jnp.inf
    grp = lax.broadcasted_iota(jnp.int32, (8, tm), 0)
    gl = jnp.where(grp < N_GROUPS, lt[GROUP_LANE0:GROUP_LANE0 + 8, :], neg)
    g_max = jnp.max(gl, axis=0, keepdims=True)
    g_idx = jnp.min(jnp.where(gl == g_max, grp, 8), axis=0, keepdims=True)
    g_w = 1.0 / jnp.sum(jnp.exp(gl - g_max), axis=0, keepdims=True)
    eid = lax.broadcasted_iota(jnp.int32, (N_EXPERTS, tm), 0)
    first = g_idx * EXPERTS_PER_GROUP
    in_group = jnp.logical_and(eid >= first, eid < first + EXPERTS_PER_GROUP)
    el = jnp.where(in_group, lt[EXPERT_LANE0:EXPERT_LANE0 + N_EXPERTS, :], neg)
    l1 = jnp.max(el, axis=0, keepdims=True)
    i1 = jnp.min(jnp.where(el == l1, eid, N_EXPERTS), axis=0, keepdims=True)
    el2 = jnp.where(eid == i1, neg, el)
    l2 = jnp.max(el2, axis=0, keepdims=True)
    i2 = jnp.min(jnp.where(el2 == l2, eid, N_EXPERTS), axis=0, keepdims=True)
    e21 = jnp.exp(l2 - l1)
    w1 = g_w * (1.0 / (1.0 + e21))
    w2 = g_w * (e21 / (1.0 + e21))
    hit1 = eid == i1
    hit2 = eid == i2
    onehot = jnp.logical_or(hit1, hit2).astype(F32)
    r = lax.broadcasted_iota(jnp.int32, (tm, tm), 0)
    c = lax.broadcasted_iota(jnp.int32, (tm, tm), 1)
    before = (r < c).astype(BF16)
    cum = jnp.dot(onehot.astype(BF16), before, preferred_element_type=F32) + carry[:, 0:1]
    rank1 = jnp.sum(jnp.where(hit1, cum, 0.0), axis=0, keepdims=True)
    rank2 = jnp.sum(jnp.where(hit2, cum, 0.0), axis=0, keepdims=True)
    row = lax.broadcasted_iota(jnp.int32, (LANES, tm), 0)
    rows = jnp.where(row == 0, i1.astype(F32), 0.0)
    rows = jnp.where(row == 1, i2.astype(F32), rows)
    rows = jnp.where(row == 2, w1, rows)
    rows = jnp.where(row == 3, w2, rows)
    rows = jnp.where(row == 4, rank1, rows)
    rows = jnp.where(row == 5, rank2, rows)
    rows_ref[...] = rows[0:8, :]
    info_ref[...] = rows.T
    total = carry[:, 0:1] + jnp.sum(onehot, axis=1, keepdims=True)
    carry[...] = jnp.broadcast_to(total, carry.shape)
    cnt_ref[...] = jnp.broadcast_to(total, cnt_ref.shape)


def _route(x, g_ffn, w_r, b_r, count_in, tm):
    n = x.shape[0]
    const = lambda i: (0, 0)
    return pl.pallas_call(
        _route_kernel,
        grid=(n // tm,),
        in_specs=[
            pl.BlockSpec((tm, D_MODEL), lambda i: (i, 0)),
            pl.BlockSpec((1, D_MODEL), const),
            pl.BlockSpec((D_MODEL, LANES), const),
            pl.BlockSpec((1, LANES), const),
            pl.BlockSpec((N_EXPERTS, LANES), const),
        ],
        out_specs=[
            pl.BlockSpec((tm, LANES), lambda i: (i, 0)),
            pl.BlockSpec((8, tm), lambda i: (0, i)),
            pl.BlockSpec((N_EXPERTS, LANES), const),
        ],
        out_shape=[
            jax.ShapeDtypeStruct((n, LANES), F32),
            jax.ShapeDtypeStruct((8, n), F32),
            jax.ShapeDtypeStruct((N_EXPERTS, LANES), F32),
        ],
        scratch_shapes=[pltpu.VMEM((N_EXPERTS, LANES), F32)],
        compiler_params=_params("arbitrary"),
        name="route",
    )(x, g_ffn, w_r, b_r, count_in)


DMA_UNROLL = 8


def _scatter_rows(d0_ref, d1_ref, x_ref, pad_out, sem, tile):
    def row_copy(r, dst_row):
        return pltpu.make_async_copy(x_ref.at[pl.ds(r, 1), :], pad_out.at[pl.ds(dst_row, 1), :], sem)

    for r in range(tile):
        row_copy(r, d0_ref[0, 0, r]).start(priority=0)
        row_copy(r, d1_ref[0, 0, r]).start(priority=1)

    def wait(r, carry):
        row_copy(0, 0).wait()
        row_copy(0, 0).wait()
        return carry

    lax.fori_loop(0, tile, wait, 0, unroll=DMA_UNROLL)


def _dispatch_kernel(zero_ref, d0p_ref, d1p_ref, d0s_ref, d1s_ref, xp_ref, xs_ref, pad_out, zero_scr, sem, zsem,
                     *, tile, n_tiles):
    i = pl.program_id(0)

    @pl.when(i == 0)
    def _():
        zero_scr[...] = jnp.zeros_like(zero_scr)

        def zero_copy(e):
            row0 = pl.multiple_of(zero_ref[e], EXPERT_ROWS)
            return pltpu.make_async_copy(zero_scr, pad_out.at[pl.ds(row0, EXPERT_ROWS), :], zsem)

        for e in range(zero_ref.shape[0]):
            @pl.when(zero_ref[e] >= 0)
            def _():
                zero_copy(e).start()
        for e in range(zero_ref.shape[0]):
            @pl.when(zero_ref[e] >= 0)
            def _():
                zero_copy(e).wait()

    @pl.when(i < n_tiles)
    def _():
        _scatter_rows(d0p_ref, d1p_ref, xp_ref, pad_out, sem, tile)

    @pl.when(i == n_tiles)
    def _():
        _scatter_rows(d0s_ref, d1s_ref, xs_ref, pad_out, sem, xs_ref.shape[0])


def _dispatch(zero_rows, dest_p, dest_s, xp, xs, n_rows, tile):
    n_tiles = xp.shape[0] // tile
    n_s = xs.shape[0]
    last = n_tiles - 1
    tile_spec = pl.BlockSpec((1, 1, tile), lambda i: (jnp.minimum(i, last), 0, 0), memory_space=pltpu.SMEM)
    return pl.pallas_call(
        functools.partial(_dispatch_kernel, tile=tile, n_tiles=n_tiles),
        grid=(n_tiles + 1,),
        in_specs=[
            pl.BlockSpec(memory_space=pltpu.SMEM),
            tile_spec, tile_spec,
            pl.BlockSpec(memory_space=pltpu.SMEM),
            pl.BlockSpec(memory_space=pltpu.SMEM),
            pl.BlockSpec((tile, D_MODEL), lambda i: (jnp.minimum(i, last), 0)),
            pl.BlockSpec((n_s, D_MODEL), lambda i: (0, 0)),
        ],
        out_specs=pl.BlockSpec(memory_space=pl.ANY),
        out_shape=jax.ShapeDtypeStruct((n_rows, D_MODEL), F32),
        scratch_shapes=[pltpu.VMEM((EXPERT_ROWS, D_MODEL), F32), pltpu.SemaphoreType.DMA(()),
                        pltpu.SemaphoreType.DMA(())],
        compiler_params=_params("arbitrary"),
        name="dispatch",
    )(zero_rows, dest_p[0].reshape(n_tiles, 1, tile), dest_p[1].reshape(n_tiles, 1, tile),
      dest_s[0].reshape(1, 1, n_s), dest_s[1].reshape(1, 1, n_s), xp, xs)


def _expert_kernel(be_ref, nused_ref, x_ref, g_ref, wg_ref, wu_ref, wd_ref, y_ref, wg_s, wu_s, wd_s):
    i = pl.program_id(0)
    prev = be_ref[jnp.maximum(i - 1, 0)]
    active = i < nused_ref[0]

    @pl.when(jnp.logical_and(active, jnp.logical_or(i == 0, be_ref[i] != prev)))
    def _():
        wg_s[...] = wg_ref[...].astype(BF16)
        wu_s[...] = wu_ref[...].astype(BF16)
        wd_s[...] = wd_ref[...].astype(BF16)

    @pl.when(active)
    def _():
        hn = _rms(x_ref[...], g_ref[...]).astype(BF16)
        gate = jnp.dot(hn, wg_s[...], preferred_element_type=F32)
        up = jnp.dot(hn, wu_s[...], preferred_element_type=F32)
        act = (gate * jax.nn.sigmoid(gate)) * up
        y_ref[...] = jnp.dot(act.astype(BF16), wd_s[...], preferred_element_type=F32)

    @pl.when(jnp.logical_not(active))
    def _():
        y_ref[...] = jnp.zeros_like(y_ref)


def _experts(block_expert, n_used, x_pad, g_ffn, w_gate, w_up, w_down):
    n_rows = x_pad.shape[0]
    nb = n_rows // EXPERT_ROWS
    grid_spec = pltpu.PrefetchScalarGridSpec(
        num_scalar_prefetch=2,
        grid=(nb,),
        in_specs=[
            pl.BlockSpec((EXPERT_ROWS, D_MODEL), lambda i, be, nu: (i, 0)),
            pl.BlockSpec((1, D_MODEL), lambda i, be, nu: (0, 0)),
            pl.BlockSpec((None, None, D_MODEL, D_EXPERT), lambda i, be, nu: (0, be[i], 0, 0)),
            pl.BlockSpec((None, None, D_MODEL, D_EXPERT), lambda i, be, nu: (0, be[i], 0, 0)),
            pl.BlockSpec((None, None, D_EXPERT, D_MODEL), lambda i, be, nu: (0, be[i], 0, 0)),
        ],
        out_specs=pl.BlockSpec((EXPERT_ROWS, D_MODEL), lambda i, be, nu: (i, 0)),
        scratch_shapes=[
            pltpu.VMEM((D_MODEL, D_EXPERT), BF16),
            pltpu.VMEM((D_MODEL, D_EXPERT), BF16),
            pltpu.VMEM((D_EXPERT, D_MODEL), BF16),
        ],
    )
    return pl.pallas_call(
        _expert_kernel,
        grid_spec=grid_spec,
        out_shape=jax.ShapeDtypeStruct((n_rows, D_MODEL), F32),
        compiler_params=_params("arbitrary"),
        name="experts",
    )(block_expert, n_used, x_pad, g_ffn, w_gate, w_up, w_down)


def _final_kernel(d0_ref, d1_ref, d0n_ref, d1n_ref, x_ref, info_ref, p_ref, y_hbm, wpg_ref, wpp_ref, gple_ref,
                  gfin_ref, out_ref, ybuf, sem):
    tm = x_ref.shape[0]
    i = pl.program_id(0)
    buf = i % 2

    def row_copy(src_row, b, k, r):
        return pltpu.make_async_copy(y_hbm.at[pl.ds(src_row, 1), :], ybuf.at[b, k, pl.ds(r, 1), :], sem.at[b])

    def gather(da_ref, db_ref, b):
        def start(r, carry):
            row_copy(da_ref[0, 0, r], b, 0, r).start(priority=0)
            row_copy(db_ref[0, 0, r], b, 1, r).start(priority=1)
            return carry
        lax.fori_loop(0, tm, start, 0, unroll=DMA_UNROLL)

    def wait_all(b):
        def wait(r, carry):
            row_copy(0, b, 0, 0).wait()
            row_copy(0, b, 1, 0).wait()
            return carry
        lax.fori_loop(0, tm, wait, 0, unroll=DMA_UNROLL)

    @pl.when(i == 0)
    def _():
        gather(d0_ref, d1_ref, 0)

    wait_all(buf)
    for r in range(tm):
        row_copy(d0n_ref[0, 0, r], 1 - buf, 0, r).start()
        row_copy(d1n_ref[0, 0, r], 1 - buf, 1, r).start()

    info = info_ref[...]
    x2 = x_ref[...] + (ybuf[buf, 0] * info[:, 2:3] + ybuf[buf, 1] * info[:, 3:4])
    gate = jax.nn.sigmoid(jnp.dot(_rms(x2, gple_ref[...]).astype(BF16), wpg_ref[...], preferred_element_type=F32))
    proj = jnp.dot(p_ref[...].astype(BF16), wpp_ref[...], preferred_element_type=F32)
    out_ref[...] = _rms(x2 + gate * proj, gfin_ref[...])

    @pl.when(i == pl.num_programs(0) - 1)
    def _():
        wait_all(1 - buf)


def _final(dest, x1, info, p, y_pad, w_pg, w_pp, g_ple, g_final, tm):
    n = x1.shape[0]
    d0 = dest[0].reshape(n // tm, 1, tm)
    d1 = dest[1].reshape(n // tm, 1, tm)
    const = lambda i: (0, 0)
    last = n // tm - 1
    cur = pl.BlockSpec((1, 1, tm), lambda i: (i, 0, 0), memory_space=pltpu.SMEM)
    nxt = pl.BlockSpec((1, 1, tm), lambda i: (jnp.minimum(i + 1, last), 0, 0), memory_space=pltpu.SMEM)
    return pl.pallas_call(
        _final_kernel,
        grid=(n // tm,),
        in_specs=[
            cur, cur, nxt, nxt,
            pl.BlockSpec((tm, D_MODEL), lambda i: (i, 0)),
            pl.BlockSpec((tm, LANES), lambda i: (i, 0)),
            pl.BlockSpec((tm, PLE_DIM), lambda i: (i, 0)),
            pl.BlockSpec(memory_space=pl.ANY),
            pl.BlockSpec((D_MODEL, D_MODEL), const),
            pl.BlockSpec((PLE_DIM, D_MODEL), const),
            pl.BlockSpec((1, D_MODEL), const),
            pl.BlockSpec((1, D_MODEL), const),
        ],
        out_specs=pl.BlockSpec((tm, D_MODEL), lambda i: (i, 0)),
        out_shape=jax.ShapeDtypeStruct((n, D_MODEL), F32),
        scratch_shapes=[pltpu.VMEM((2, 2, tm, D_MODEL), F32), pltpu.SemaphoreType.DMA((2,))],
        compiler_params=_params("arbitrary"),
        name="final",
    )(d0, d1, d0, d1, x1, info, p, y_pad, w_pg, w_pp, g_ple, g_final)


def _pad_lanes(a):
    return jnp.pad(a, ((0, 0), (0, LANES - a.shape[1])))


def kernel(x_prompt, x_sample, p_prompt, p_sample, state_mlstm_C, state_mlstm_n, state_mlstm_m, state_swa_k, state_swa_v, g_mix, w_in, b_igate, b_fgate, g_mh, attn_sinks, w_out, g_ffn, w_router_g, b_router_g, w_router_e, b_router_e, w_gate, w_up, w_down, g_ple, w_ple_gate, w_ple_proj, g_final):
    batch, seq, d = x_prompt.shape
    dec = x_sample.shape[0]
    n_p = batch * seq
    xp = x_prompt.reshape(n_p, d)
    xs = x_sample.reshape(dec, d)

    w_in0 = w_in[0]
    head_order = jnp.array(SWA_HEAD_ORDER, jnp.int32)
    w_aq = w_in0[:, ATT_OFF:ATT_OFF + A_Q_W].reshape(d, A_HEADS, A_HEAD_DIM)[:, head_order].reshape(d, A_Q_W)
    zf_pad = ZF_W - (ZF_GATE + LANES)
    w_cat = jnp.concatenate(
        [w_in0[:, :3 * M_W], w_aq,
         w_in0[:, 3 * M_W:GATE_OFF], w_in0[:, ATT_OFF + A_Q_W:],
         _pad_lanes(w_in0[:, GATE_OFF:ATT_OFF]), jnp.zeros((d, zf_pad), F32)],
        axis=1).astype(BF16)
    gate_bias8 = jnp.concatenate([b_igate, b_fgate], axis=1)
    gate_bias = _pad_lanes(gate_bias8)
    w_oa = w_out[0, M_W:].reshape(A_HEADS, A_HEAD_DIM, d)[head_order].reshape(A_Q_W, d)
    w_o = jnp.concatenate([w_out[0, :M_W], w_oa], axis=0).astype(BF16)
    kv_of_head = jnp.arange(A_HEADS, dtype=jnp.int32) // A_GROUP
    head_mask = (kv_of_head[:, None] == jnp.arange(A_KV_HEADS, dtype=jnp.int32)[None, :]).astype(F32)
    gap = EXPERT_LANE0 - N_GROUPS
    w_r = _pad_lanes(jnp.concatenate([w_router_g[0], jnp.zeros((d, gap), F32), w_router_e[0]], axis=1))
    b_r = _pad_lanes(jnp.concatenate([b_router_g, jnp.zeros((1, gap), F32), b_router_e], axis=1))
    w_pg = w_ple_gate[0].astype(BF16)
    w_pp = w_ple_proj[0].astype(BF16)
    sink_col = attn_sinks.reshape(A_HEADS, 1)

    zb, zf = _inproj_split(xp, g_mix, w_cat, tm=1024, tn=1024)
    om_p, c_p, n_p8, m_p8 = _mlstm_prompt(zb, zf, gate_bias, g_mh, batch, seq)
    oa_p = _swa_prompt(zb, zf, attn_sinks, batch, seq)
    x1p = _outproj(xp, om_p, oa_p, w_o, tm=512)

    zs = _inproj(xs, g_mix, w_in0, tm=dec, tn=512, precision=HI)
    om_s, c_s, n_s, m_s = _mlstm_sample(
        zs.reshape(dec, 1, zs.shape[1]), gate_bias8,
        state_mlstm_C[0], state_mlstm_n[0].reshape(dec, 1, M_W), state_mlstm_m[0].reshape(dec, 1, M_HEADS), g_mh)
    ks_off = ATT_OFF + A_Q_W
    vs_off = ks_off + A_KV_W
    q_heads = zs[:, ATT_OFF:ks_off].reshape(dec, A_HEADS, 1, A_HEAD_DIM)
    qx = (q_heads * head_mask[None, :, :, None]).reshape(dec, A_HEADS, A_KV_W)
    ox_s, k_s, v_s = _swa_sample(
        qx,
        zs[:, ks_off:vs_off].reshape(dec, 1, A_KV_W),
        zs[:, vs_off:].reshape(dec, 1, A_KV_W),
        state_swa_k[0].reshape(dec, WINDOW, A_KV_W), state_swa_v[0].reshape(dec, WINDOW, A_KV_W), sink_col)
    oa_s = jnp.sum(ox_s.reshape(dec, A_HEADS, A_KV_HEADS, A_HEAD_DIM) * head_mask[None, :, :, None], axis=2)
    x1s = _outproj(xs, om_s.reshape(dec, M_W), oa_s.reshape(dec, A_Q_W), w_out[0], tm=dec, precision=HI)

    info_p, rows_p, cnt_p = _route(x1p, g_ffn, w_r, b_r, jnp.zeros((N_EXPERTS, LANES), F32), tm=512)
    info_s, rows_s, cnt = _route(x1s, g_ffn, w_r, b_r, cnt_p, tm=dec)
    counts = cnt[:, 0].astype(jnp.int32)
    padded = ((counts + EXPERT_ROWS - 1) // EXPERT_ROWS) * EXPERT_ROWS
    pad_end = jnp.cumsum(padded)
    pad_start = pad_end - padded
    n_assign = 2 * (n_p + dec)
    n_blocks = -(-n_assign // EXPERT_ROWS) + N_EXPERTS
    block_row0 = jnp.arange(n_blocks, dtype=jnp.int32) * EXPERT_ROWS
    block_expert = jnp.minimum(
        jnp.sum((pad_end[None, :] <= block_row0[:, None]).astype(jnp.int32), axis=1), N_EXPERTS - 1)
    n_used = (pad_end[-1:] // EXPERT_ROWS).astype(jnp.int32)
    spare = n_used + jnp.arange(N_EXPERTS, dtype=jnp.int32)
    zero_rows = jnp.concatenate([
        jnp.where(padded > 0, pad_end - EXPERT_ROWS, -1),
        jnp.where(spare < n_blocks, spare * EXPERT_ROWS, -1)]).astype(jnp.int32)

    def dest_of(rows):
        e = rows[0:2].astype(jnp.int32)
        hit = e[:, :, None] == jnp.arange(N_EXPERTS, dtype=jnp.int32)
        return jnp.sum(jnp.where(hit, pad_start, 0), axis=2) + rows[4:6].astype(jnp.int32)

    dest_p = dest_of(rows_p)
    dest_s = dest_of(rows_s)

    x_pad = _dispatch(zero_rows, dest_p, dest_s, x1p, x1s, n_blocks * EXPERT_ROWS, tile=512)
    y_pad = _experts(block_expert, n_used, x_pad, g_ffn, w_gate, w_up, w_down)
    y_p = _final(dest_p, x1p, info_p, p_prompt[0].reshape(n_p, PLE_DIM), y_pad, w_pg, w_pp, g_ple, g_final.reshape(1, d), tm=256)
    y_s = _final(dest_s, x1s, info_s, p_sample[0].reshape(dec, PLE_DIM), y_pad, w_pg, w_pp, g_ple, g_final.reshape(1, d), tm=dec)

    zf3 = zf.reshape(batch, seq, ZF_W)
    swa_k_p = zf3[:, seq - WINDOW:, ZF_K:ZF_V].reshape(1, batch, WINDOW, A_KV_HEADS, A_HEAD_DIM)
    swa_v_p = zf3[:, seq - WINDOW:, ZF_V:ZF_GATE].reshape(1, batch, WINDOW, A_KV_HEADS, A_HEAD_DIM)
    return (
        y_p.reshape(batch, seq, d),
        y_s.reshape(dec, 1, d),
        c_p[None],
        n_p8[None, :, :M_HEADS, :],
        m_p8[None, :, :M_HEADS, 0],
        swa_k_p,
        swa_v_p,
        c_s[None],
        n_s.reshape(1, dec, M_HEADS, M_DK),
        m_s[None, :, 0, :M_HEADS],
        k_s.reshape(1, dec, WINDOW, A_KV_HEADS, A_HEAD_DIM),
        v_s.reshape(1, dec, WINDOW, A_KV_HEADS, A_HEAD_DIM),
    )
```

```python
import functools

import jax
import jax.numpy as jnp
from jax import lax
from jax.experimental import pallas as pl
from jax.experimental.pallas import tpu as pltpu

F32 = jnp.float32
BF16 = jnp.bfloat16
HI = lax.Precision.HIGHEST

D_MODEL = 2048
M_HEADS = 4
M_DK = 256
M_DV = 256
A_HEADS = 16
A_KV_HEADS = 4
A_HEAD_DIM = 64
A_GROUP = A_HEADS // A_KV_HEADS
WINDOW = 128
M_W = M_HEADS * M_DK
A_Q_W = A_HEADS * A_HEAD_DIM
A_KV_W = A_KV_HEADS * A_HEAD_DIM
GATE_OFF = 4 * M_W
ATT_OFF = GATE_OFF + 2 * M_HEADS
ZB_W = 3 * M_W + A_Q_W
ZF_MO, ZF_K, ZF_V, ZF_GATE = 0, M_W, M_W + A_KV_W, M_W + 2 * A_KV_W
ZF_W = 2 * M_W
N_GROUPS = 4
EXPERTS_PER_GROUP = 8
N_EXPERTS = N_GROUPS * EXPERTS_PER_GROUP
D_EXPERT = 512
PLE_DIM = 256
EPS = 1e-6

LANES = 128
MLSTM_CHUNK = 256
EXPERT_ROWS = 256
VMEM_LIMIT = 56 * 1024 * 1024
NT_DIMS = (((1,), (1,)), ((), ()))
TN_DIMS = (((0,), (0,)), ((), ()))


def _params(*sem):
    return pltpu.CompilerParams(dimension_semantics=sem, vmem_limit_bytes=VMEM_LIMIT)


def _rms(x, g):
    r = lax.rsqrt(jnp.mean(x * x, axis=-1, keepdims=True) + EPS)
    return (x * r) * g


def _log_sigmoid(x):
    return jnp.minimum(x, 0.0) - jnp.log1p(jnp.exp(-jnp.abs(x)))


def _inproj_kernel(x_ref, g_ref, w_ref, z_ref, hn_scr, *, precision):
    @pl.when(pl.program_id(1) == 0)
    def _():
        hn_scr[...] = _rms(x_ref[...], g_ref[...]).astype(hn_scr.dtype)

    z_ref[...] = jnp.dot(hn_scr[...], w_ref[...], precision=precision, preferred_element_type=F32)


def _inproj(x, g_mix, w, tm, tn, precision=None):
    n = x.shape[0]
    width = w.shape[1]
    return pl.pallas_call(
        functools.partial(_inproj_kernel, precision=precision),
        grid=(n // tm, pl.cdiv(width, tn)),
        in_specs=[
            pl.BlockSpec((tm, D_MODEL), lambda i, j: (i, 0)),
            pl.BlockSpec((1, D_MODEL), lambda i, j: (0, 0)),
            pl.BlockSpec((D_MODEL, tn), lambda i, j: (0, j)),
        ],
        out_specs=pl.BlockSpec((tm, tn), lambda i, j: (i, j)),
        out_shape=jax.ShapeDtypeStruct((n, width), F32),
        scratch_shapes=[pltpu.VMEM((tm, D_MODEL), w.dtype)],
        compiler_params=_params("parallel", "arbitrary"),
        name="inproj",
    )(x, g_mix, w)


def _inproj_split_kernel(x_ref, g_ref, w_ref, zb_ref, zf_ref, hn_scr, *, n_bf16_blocks):
    j = pl.program_id(1)

    @pl.when(j == 0)
    def _():
        hn_scr[...] = _rms(x_ref[...], g_ref[...]).astype(BF16)

    acc = jnp.dot(hn_scr[...], w_ref[...], preferred_element_type=F32)

    @pl.when(j < n_bf16_blocks)
    def _():
        zb_ref[...] = acc.astype(BF16)

    @pl.when(j >= n_bf16_blocks)
    def _():
        zf_ref[...] = acc


def _inproj_split(x, g_mix, w, tm, tn):
    n = x.shape[0]
    nb = ZB_W // tn
    nf = ZF_W // tn
    return pl.pallas_call(
        functools.partial(_inproj_split_kernel, n_bf16_blocks=nb),
        grid=(n // tm, nb + nf),
        in_specs=[
            pl.BlockSpec((tm, D_MODEL), lambda i, j: (i, 0)),
            pl.BlockSpec((1, D_MODEL), lambda i, j: (0, 0)),
            pl.BlockSpec((D_MODEL, tn), lambda i, j: (0, j)),
        ],
        out_specs=[
            pl.BlockSpec((tm, tn), lambda i, j: (i, jnp.minimum(j, nb - 1))),
            pl.BlockSpec((tm, tn), lambda i, j: (i, jnp.maximum(j - nb, 0))),
        ],
        out_shape=[
            jax.ShapeDtypeStruct((n, ZB_W), BF16),
            jax.ShapeDtypeStruct((n, ZF_W), F32),
        ],
        scratch_shapes=[pltpu.VMEM((tm, D_MODEL), BF16)],
        compiler_params=_params("parallel", "arbitrary"),
        name="inproj_prompt",
    )(x, g_mix, w)


def _mlstm_out(h, gmh, mo):
    hn = h * lax.rsqrt(jnp.mean(h * h, axis=-1, keepdims=True) + EPS)
    return (hn * gmh) * jax.nn.sigmoid(mo)


def _gates(g_raw, bias):
    g = g_raw + bias
    lane = lax.broadcasted_iota(jnp.int32, g.shape, g.ndim - 1)
    return jnp.where(lane < M_HEADS, g, _log_sigmoid(g))


def _mlstm_kernel(q_ref, k_ref, v_ref, mo_ref, g_ref, bias_ref, gmh_ref,
                  o_ref, c_out, n_out, m_out, c_scr, n_scr, m_scr, *, chunk, n_chunks):
    c_idx = pl.program_id(1)

    @pl.when(c_idx == 0)
    def _():
        c_scr[...] = jnp.zeros_like(c_scr)
        n_scr[...] = jnp.zeros_like(n_scr)
        m_scr[...] = jnp.zeros_like(m_scr)

    L = chunk
    row = lax.broadcasted_iota(jnp.int32, (L, L), 0)
    col = lax.broadcasted_iota(jnp.int32, (L, L), 1)
    causal = row >= col
    ltri = causal.astype(F32)
    utri = (row <= col).astype(F32)
    g_c = _gates(g_ref[...], bias_ref[...])
    f_c = jnp.dot(ltri, g_c, precision=HI, preferred_element_type=F32)
    g_r = g_c.T[0:2 * M_HEADS, :]
    f_r = jnp.dot(g_r, utri, precision=HI, preferred_element_type=F32)

    for h in range(M_HEADS):
        hs = slice(h * M_DK, (h + 1) * M_DK)
        hf = M_HEADS + h
        a_row = g_r[h:h + 1, :] - f_r[hf:hf + 1, :]
        f_col = f_c[:, hf:hf + 1]
        a_col = g_c[:, h:h + 1] - f_col
        m0 = m_scr[h:h + 1, 0:1]
        a_max = jnp.max(jnp.where(causal, a_row, -jnp.inf), axis=1, keepdims=True)
        m_t = f_col + jnp.maximum(m0, a_max)
        dec0 = jnp.exp(f_col + m0 - m_t)
        d = jnp.exp(jnp.where(causal, (f_col - m_t) + a_row, -jnp.inf))
        qb = q_ref[:, hs]
        kb = k_ref[:, hs] * (M_DK ** -0.5)
        vb = v_ref[:, hs]
        q = qb.astype(F32)
        k = kb.astype(F32)
        w = lax.dot_general(qb, kb, NT_DIMS, preferred_element_type=F32) * d
        c0 = c_scr[h]
        n0 = n_scr[h:h + 1, :]
        num = jnp.dot(w.astype(BF16), vb, preferred_element_type=F32) + dec0 * lax.dot_general(
            qb, c0.astype(BF16), NT_DIMS, preferred_element_type=F32)
        den = jnp.sum(w, axis=1, keepdims=True) + dec0 * jnp.sum(q * n0, axis=1, keepdims=True)
        hh = num / jnp.maximum(jnp.abs(den), jnp.exp(-m_t))
        m_end = m_t[L - 1:L, :]
        f_end = f_col[L - 1:L, :]
        w_end = jnp.exp(f_end + a_col - m_end)
        dec_end = jnp.exp(f_end + m0 - m_end)
        kw = k * w_end
        c_scr[h] = dec_end * c0 + lax.dot_general(vb, kw.astype(BF16), TN_DIMS, preferred_element_type=F32)
        n_scr[h:h + 1, :] = dec_end * n0 + jnp.sum(kw, axis=0, keepdims=True)
        m_scr[h:h + 1, :] = jnp.broadcast_to(m_end, (1, LANES))
        o_ref[:, hs] = _mlstm_out(hh, gmh_ref[:, hs], mo_ref[:, hs]).astype(o_ref.dtype)

    @pl.when(c_idx == n_chunks - 1)
    def _():
        c_out[0] = c_scr[...]
        n_out[0] = n_scr[...]
        m_out[0] = m_scr[...]


def _mlstm_prompt(zb, zf, gate_bias, g_mh, batch, seq):
    L = MLSTM_CHUNK
    nc = seq // L
    rows = lambda b, c: b * nc + c
    gate_blk = ZF_GATE // LANES
    return pl.pallas_call(
        functools.partial(_mlstm_kernel, chunk=L, n_chunks=nc),
        grid=(batch, nc),
        in_specs=[
            pl.BlockSpec((L, M_W), lambda b, c: (rows(b, c), 0)),
            pl.BlockSpec((L, M_W), lambda b, c: (rows(b, c), 1)),
            pl.BlockSpec((L, M_W), lambda b, c: (rows(b, c), 2)),
            pl.BlockSpec((L, M_W), lambda b, c: (rows(b, c), ZF_MO // M_W)),
            pl.BlockSpec((L, LANES), lambda b, c: (rows(b, c), gate_blk)),
            pl.BlockSpec((1, LANES), lambda b, c: (0, 0)),
            pl.BlockSpec((1, M_W), lambda b, c: (0, 0)),
        ],
        out_specs=[
            pl.BlockSpec((L, M_W), lambda b, c: (rows(b, c), 0)),
            pl.BlockSpec((1, M_HEADS, M_DV, M_DK), lambda b, c: (b, 0, 0, 0)),
            pl.BlockSpec((1, 8, M_DK), lambda b, c: (b, 0, 0)),
            pl.BlockSpec((1, 8, LANES), lambda b, c: (b, 0, 0)),
        ],
        out_shape=[
            jax.ShapeDtypeStruct((batch * seq, M_W), BF16),
            jax.ShapeDtypeStruct((batch, M_HEADS, M_DV, M_DK), F32),
            jax.ShapeDtypeStruct((batch, 8, M_DK), F32),
            jax.ShapeDtypeStruct((batch, 8, LANES), F32),
        ],
        scratch_shapes=[
            pltpu.VMEM((M_HEADS, M_DV, M_DK), F32),
            pltpu.VMEM((8, M_DK), F32),
            pltpu.VMEM((8, LANES), F32),
        ],
        compiler_params=_params("parallel", "arbitrary"),
        name="mlstm_prompt",
    )(zb, zb, zb, zf, zf, gate_bias, g_mh)


MLSTM_STEP_TOKENS = 2


def _mlstm_step_kernel(z_ref, bias_ref, c_ref, n_ref, m_ref, gmh_ref, o_ref, c_out, n_out, m_out):
    row = lax.broadcasted_iota(jnp.int32, (M_DV, M_DV), 0)
    col = lax.broadcasted_iota(jnp.int32, (M_DV, M_DV), 1)
    eye = row == col
    for t in range(z_ref.shape[0]):
        _mlstm_step_token(t, eye, z_ref, bias_ref, c_ref, n_ref, m_ref, gmh_ref, o_ref, c_out, n_out, m_out)


def _mlstm_step_token(t, eye, z_ref, bias_ref, c_ref, n_ref, m_ref, gmh_ref, o_ref, c_out, n_out, m_out):
    z = z_ref[t]
    gates = _gates(z[:, GATE_OFF:ATT_OFF], bias_ref[...])
    m_in = m_ref[t]
    lane = lax.broadcasted_iota(jnp.int32, (1, LANES), 1)
    m_new = jnp.zeros((1, LANES), F32)
    for h in range(M_HEADS):
        hs = slice(h * M_DK, (h + 1) * M_DK)
        q = z[:, h * M_DK:(h + 1) * M_DK]
        k = z[:, M_W + h * M_DK:M_W + (h + 1) * M_DK] * (M_DK ** -0.5)
        v = z[:, 2 * M_W + h * M_DV:2 * M_W + (h + 1) * M_DV]
        mo = z[:, 3 * M_W + h * M_DV:3 * M_W + (h + 1) * M_DV]
        ip = gates[:, h:h + 1]
        lf = gates[:, M_HEADS + h:M_HEADS + h + 1]
        m0 = m_in[:, h:h + 1]
        m_t = jnp.maximum(lf + m0, ip)
        dec0 = jnp.exp(lf + m0 - m_t)
        d = jnp.exp(ip - m_t)
        c0 = c_ref[t, h]
        n0 = n_ref[t, :, hs]
        w = jnp.sum(q * k, axis=1, keepdims=True) * d
        cq_col = jnp.sum(c0 * q, axis=1, keepdims=True)
        cq = jnp.sum(jnp.where(eye, cq_col, 0.0), axis=0, keepdims=True)
        num = w * v + dec0 * cq
        den = w + dec0 * jnp.sum(n0 * q, axis=1, keepdims=True)
        hh = num / jnp.maximum(jnp.abs(den), jnp.exp(-m_t))
        v_col = jnp.sum(jnp.where(eye, v, 0.0), axis=1, keepdims=True)
        c_out[t, h] = dec0 * c0 + (d * v_col) * k
        n_out[t, :, hs] = dec0 * n0 + d * k
        m_new = jnp.where(lane == h, m_t, m_new)
        o_ref[t, :, hs] = _mlstm_out(hh, gmh_ref[:, hs], mo).astype(o_ref.dtype)
    m_out[t] = m_new


def _mlstm_sample(z3, gate_bias, c0, n0, m0, g_mh):
    nb = z3.shape[0]
    nt = MLSTM_STEP_TOKENS
    tok = lambda b: (b, 0, 0)
    return pl.pallas_call(
        _mlstm_step_kernel,
        grid=(nb // nt,),
        in_specs=[
            pl.BlockSpec((nt, 1, z3.shape[2]), tok),
            pl.BlockSpec((1, 2 * M_HEADS), lambda b: (0, 0)),
            pl.BlockSpec((nt, M_HEADS, M_DV, M_DK), lambda b: (b, 0, 0, 0)),
            pl.BlockSpec((nt, 1, M_W), tok),
            pl.BlockSpec((nt, 1, M_HEADS), tok),
            pl.BlockSpec((1, M_W), lambda b: (0, 0)),
        ],
        out_specs=[
            pl.BlockSpec((nt, 1, M_W), tok),
            pl.BlockSpec((nt, M_HEADS, M_DV, M_DK), lambda b: (b, 0, 0, 0)),
            pl.BlockSpec((nt, 1, M_W), tok),
            pl.BlockSpec((nt, 1, LANES), tok),
        ],
        out_shape=[
            jax.ShapeDtypeStruct((nb, 1, M_W), F32),
            jax.ShapeDtypeStruct((nb, M_HEADS, M_DV, M_DK), F32),
            jax.ShapeDtypeStruct((nb, 1, M_W), F32),
            jax.ShapeDtypeStruct((nb, 1, LANES), F32),
        ],
        compiler_params=_params("parallel"),
        name="mlstm_sample",
    )(z3, gate_bias, c0, n0, m0, g_mh)


_EVEN_KV = [h for h in range(A_HEADS) if (h // A_GROUP) % 2 == 0]
_ODD_KV = [h for h in range(A_HEADS) if (h // A_GROUP) % 2 == 1]
SWA_HEAD_ORDER = tuple(h for pair in zip(_EVEN_KV, _ODD_KV) for h in pair)
HALF = LANES // 2
KV_SPAN = 2 * WINDOW


def _swa_kernel(sink_ref, q_ref, kc_ref, kp_ref, vc_ref, vp_ref, o_ref, qx_scr, p_scr):
    j = pl.program_id(1)
    lo_half = lax.broadcasted_iota(jnp.int32, (WINDOW, LANES), 1) < HALF
    zero_tile = jnp.zeros((WINDOW, LANES), BF16)
    for slot, head in enumerate(SWA_HEAD_ORDER):
        kv = head // A_GROUP
        tile = q_ref[:, (slot // 2) * LANES:(slot // 2 + 1) * LANES]
        keep = lo_half if slot % 2 == 0 else jnp.logical_not(lo_half)
        qm = jnp.where(keep, tile * (A_HEAD_DIM ** -0.5), jnp.zeros_like(tile))
        parts = [qm, zero_tile] if kv // 2 == 0 else [zero_tile, qm]
        qx_scr[slot * WINDOW:(slot + 1) * WINDOW, :] = jnp.concatenate(parts, axis=1)

    k2 = jnp.concatenate([kp_ref[...], kc_ref[...]], axis=0).astype(BF16)
    v2 = jnp.concatenate([vp_ref[...], vc_ref[...]], axis=0).astype(BF16)
    s_all = lax.dot_general(qx_scr[...], k2, NT_DIMS, preferred_element_type=F32)

    t = lax.broadcasted_iota(jnp.int32, (WINDOW, KV_SPAN), 0)
    kk = lax.broadcasted_iota(jnp.int32, (WINDOW, KV_SPAN), 1)
    in_prev = jnp.logical_and(jnp.logical_and(kk < WINDOW, kk > t), j > 0)
    in_cur = jnp.logical_and(kk >= WINDOW, kk - WINDOW <= t)
    allowed = jnp.logical_or(in_prev, in_cur)
    for slot, head in enumerate(SWA_HEAD_ORDER):
        rows = slice(slot * WINDOW, (slot + 1) * WINDOW)
        s = jnp.where(allowed, s_all[rows, :], -jnp.inf)
        sink = sink_ref[0, head]
        mx = jnp.maximum(jnp.max(s, axis=1, keepdims=True), sink)
        e = jnp.exp(s - mx)
        den = jnp.sum(e, axis=1, keepdims=True) + jnp.exp(sink - mx)
        p_scr[rows, :] = (e * (1.0 / den)).astype(BF16)

    o_all = jnp.dot(p_scr[...], v2, preferred_element_type=F32)
    for c in range(A_HEADS // 2):
        tiles = []
        for slot in (2 * c, 2 * c + 1):
            kv = SWA_HEAD_ORDER[slot] // A_GROUP
            tiles.append(o_all[slot * WINDOW:(slot + 1) * WINDOW, (kv // 2) * LANES:(kv // 2 + 1) * LANES])
        o_ref[:, c * LANES:(c + 1) * LANES] = jnp.where(lo_half, tiles[0], tiles[1]).astype(o_ref.dtype)


def _swa_prompt(zb, zf, sinks, batch, seq):
    nb = seq // WINDOW
    q_blk = (3 * M_W) // A_Q_W
    k_blk = ZF_K // A_KV_W
    cur = lambda b, j: b * nb + j
    prev = lambda b, j: b * nb + jnp.maximum(j - 1, 0)
    return pl.pallas_call(
        _swa_kernel,
        grid=(batch, nb),
        in_specs=[
            pl.BlockSpec(memory_space=pltpu.SMEM),
            pl.BlockSpec((WINDOW, A_Q_W), lambda b, j: (cur(b, j), q_blk)),
            pl.BlockSpec((WINDOW, A_KV_W), lambda b, j: (cur(b, j), k_blk)),
            pl.BlockSpec((WINDOW, A_KV_W), lambda b, j: (prev(b, j), k_blk)),
            pl.BlockSpec((WINDOW, A_KV_W), lambda b, j: (cur(b, j), k_blk + 1)),
            pl.BlockSpec((WINDOW, A_KV_W), lambda b, j: (prev(b, j), k_blk + 1)),
        ],
        out_specs=pl.BlockSpec((WINDOW, A_Q_W), lambda b, j: (cur(b, j), 0)),
        out_shape=jax.ShapeDtypeStruct((batch * seq, A_Q_W), BF16),
        scratch_shapes=[pltpu.VMEM((A_HEADS * WINDOW, A_KV_W), BF16), pltpu.VMEM((A_HEADS * WINDOW, KV_SPAN), BF16)],
        compiler_params=_params("parallel", "arbitrary"),
        name="swa_prompt",
    )(sinks, zb, zf, zf, zf, zf)


SWA_STEP_TOKENS = 4


def _swa_step_kernel(qx_ref, kn_ref, vn_ref, bk_ref, bv_ref, sink_ref, o_ref, ko_ref, vo_ref):
    in_window = lax.broadcasted_iota(jnp.int32, (1, WINDOW), 1) >= 1
    scale = A_HEAD_DIM ** -0.5
    sink = sink_ref[...]
    for t in range(qx_ref.shape[0]):
        qx = qx_ref[t]
        bk = bk_ref[t]
        bv = bv_ref[t]
        kn = kn_ref[t]
        vn = vn_ref[t]
        s = lax.dot_general(qx, bk, NT_DIMS, precision=HI, preferred_element_type=F32) * scale
        s = jnp.where(in_window, s, -jnp.inf)
        s_new = jnp.sum(qx * kn, axis=1, keepdims=True) * scale
        mx = jnp.maximum(jnp.maximum(jnp.max(s, axis=1, keepdims=True), s_new), sink)
        e = jnp.exp(s - mx)
        e_new = jnp.exp(s_new - mx)
        den = jnp.sum(e, axis=1, keepdims=True) + e_new + jnp.exp(sink - mx)
        o_ref[t] = jnp.dot(e / den, bv, precision=HI, preferred_element_type=F32) + (e_new / den) * vn
        ko_ref[t, 0:WINDOW - 1, :] = bk[1:WINDOW, :]
        ko_ref[t, WINDOW - 1:WINDOW, :] = kn
        vo_ref[t, 0:WINDOW - 1, :] = bv[1:WINDOW, :]
        vo_ref[t, WINDOW - 1:WINDOW, :] = vn


def _swa_sample(qx, kn3, vn3, buf_k, buf_v, sink_col):
    nb = qx.shape[0]
    nt = SWA_STEP_TOKENS
    tok = lambda b: (b, 0, 0)
    return pl.pallas_call(
        _swa_step_kernel,
        grid=(nb // nt,),
        in_specs=[
            pl.BlockSpec((nt, A_HEADS, A_KV_W), tok),
            pl.BlockSpec((nt, 1, A_KV_W), tok),
            pl.BlockSpec((nt, 1, A_KV_W), tok),
            pl.BlockSpec((nt, WINDOW, A_KV_W), tok),
            pl.BlockSpec((nt, WINDOW, A_KV_W), tok),
            pl.BlockSpec((A_HEADS, 1), lambda b: (0, 0)),
        ],
        out_specs=[
            pl.BlockSpec((nt, A_HEADS, A_KV_W), tok),
            pl.BlockSpec((nt, WINDOW, A_KV_W), tok),
            pl.BlockSpec((nt, WINDOW, A_KV_W), tok),
        ],
        out_shape=[
            jax.ShapeDtypeStruct((nb, A_HEADS, A_KV_W), F32),
            jax.ShapeDtypeStruct((nb, WINDOW, A_KV_W), F32),
            jax.ShapeDtypeStruct((nb, WINDOW, A_KV_W), F32),
        ],
        compiler_params=_params("parallel"),
        name="swa_sample",
    )(qx, kn3, vn3, buf_k, buf_v, sink_col)


GROUP_LANE0 = 0
EXPERT_LANE0 = 8


def _outproj_kernel(x_ref, om_ref, oa_ref, wm_ref, wa_ref, y_ref, *, precision):
    y_ref[...] = (x_ref[...]
                  + jnp.dot(om_ref[...], wm_ref[...], precision=precision, preferred_element_type=F32)
                  + jnp.dot(oa_ref[...], wa_ref[...], precision=precision, preferred_element_type=F32))


def _outproj(x, o_m, o_a, w, tm, precision=None):
    n = x.shape[0]
    return pl.pallas_call(
        functools.partial(_outproj_kernel, precision=precision),
        grid=(n // tm,),
        in_specs=[
            pl.BlockSpec((tm, D_MODEL), lambda i: (i, 0)),
            pl.BlockSpec((tm, M_W), lambda i: (i, 0)),
            pl.BlockSpec((tm, A_Q_W), lambda i: (i, 0)),
            pl.BlockSpec((M_W, D_MODEL), lambda i: (0, 0)),
            pl.BlockSpec((A_Q_W, D_MODEL), lambda i: (1, 0)),
        ],
        out_specs=pl.BlockSpec((tm, D_MODEL), lambda i: (i, 0)),
        out_shape=jax.ShapeDtypeStruct((n, D_MODEL), F32),
        compiler_params=_params("parallel"),
        name="outproj",
    )(x, o_m, o_a, w, w)


def _route_kernel(x_ref, g_ref, wr_ref, br_ref, cin_ref, info_ref, rows_ref, cnt_ref, carry):
    @pl.when(pl.program_id(0) == 0)
    def _():
        carry[...] = cin_ref[...]

    tm = x_ref.shape[0]
    hn = _rms(x_ref[...], g_ref[...])
    w_r = wr_ref[...]
    hn_hi = hn.astype(BF16)
    hn_lo = (hn - hn_hi.astype(F32)).astype(BF16)
    w_hi = w_r.astype(BF16)
    w_lo = (w_r - w_hi.astype(F32)).astype(BF16)
    logits = (jnp.dot(hn_hi, w_hi, preferred_element_type=F32)
              + (jnp.dot(hn_lo, w_hi, preferred_element_type=F32)
                 + jnp.dot(hn_hi, w_lo, preferred_element_type=F32))) + br_ref[...]
    lt = logits.T
    neg = -jnp.inf
    grp = lax.broadcasted_iota(jnp.int32, (8, tm), 0)
    gl = jnp.where(grp < N_GROUPS, lt[GROUP_LANE0:GROUP_LANE0 + 8, :], neg)
    g_max = jnp.max(gl, axis=0, keepdims=True)
    g_idx = jnp.min(jnp.where(gl == g_max, grp, 8), axis=0, keepdims=True)
    g_w = 1.0 / jnp.sum(jnp.exp(gl - g_max), axis=0, keepdims=True)
    eid = lax.broadcasted_iota(jnp.int32, (N_EXPERTS, tm), 0)
    first = g_idx * EXPERTS_PER_GROUP
    in_group = jnp.logical_and(eid >= first, eid < first + EXPERTS_PER_GROUP)
    el = jnp.where(in_group, lt[EXPERT_LANE0:EXPERT_LANE0 + N_EXPERTS, :], neg)
    l1 = jnp.max(el, axis=0, keepdims=True)
    i1 = jnp.min(jnp.where(el == l1, eid, N_EXPERTS), axis=0, keepdims=True)
    el2 = jnp.where(eid == i1, neg, el)
    l2 = jnp.max(el2, axis=0, keepdims=True)
    i2 = jnp.min(jnp.where(el2 == l2, eid, N_EXPERTS), axis=0, keepdims=True)
    e21 = jnp.exp(l2 - l1)
    w1 = g_w * (1.0 / (1.0 + e21))
    w2 = g_w * (e21 / (1.0 + e21))
    hit1 = eid == i1
    hit2 = eid == i2
    onehot = jnp.logical_or(hit1, hit2).astype(F32)
    r = lax.broadcasted_iota(jnp.int32, (tm, tm), 0)
    c = lax.broadcasted_iota(jnp.int32, (tm, tm), 1)
    before = (r < c).astype(BF16)
    cum = jnp.dot(onehot.astype(BF16), before, preferred_element_type=F32) + carry[:, 0:1]
    rank1 = jnp.sum(jnp.where(hit1, cum, 0.0), axis=0, keepdims=True)
    rank2 = jnp.sum(jnp.where(hit2, cum, 0.0), axis=0, keepdims=True)
    row = lax.broadcasted_iota(jnp.int32, (LANES, tm), 0)
    rows = jnp.where(row == 0, i1.astype(F32), 0.0)
    rows = jnp.where(row == 1, i2.astype(F32), rows)
    rows = jnp.where(row == 2, w1, rows)
    rows = jnp.where(row == 3, w2, rows)
    rows = jnp.where(row == 4, rank1, rows)
    rows = jnp.where(row == 5, rank2, rows)
    rows_ref[...] = rows[0:8, :]
    info_ref[...] = rows.T
    total = carry[:, 0:1] + jnp.sum(onehot, axis=1, keepdims=True)
    carry[...] = jnp.broadcast_to(total, carry.shape)
    cnt_ref[...] = jnp.broadcast_to(total, cnt_ref.shape)


def _route(x, g_ffn, w_r, b_r, count_in, tm):
    n = x.shape[0]
    const = lambda i: (0, 0)
    return pl.pallas_call(
        _route_kernel,
        grid=(n // tm,),
        in_specs=[
            pl.BlockSpec((tm, D_MODEL), lambda i: (i, 0)),
            pl.BlockSpec((1, D_MODEL), const),
            pl.BlockSpec((D_MODEL, LANES), const),
            pl.BlockSpec((1, LANES), const),
            pl.BlockSpec((N_EXPERTS, LANES), const),
        ],
        out_specs=[
            pl.BlockSpec((tm, LANES), lambda i: (i, 0)),
            pl.BlockSpec((8, tm), lambda i: (0, i)),
            pl.BlockSpec((N_EXPERTS, LANES), const),
        ],
        out_shape=[
            jax.ShapeDtypeStruct((n, LANES), F32),
            jax.ShapeDtypeStruct((8, n), F32),
            jax.ShapeDtypeStruct((N_EXPERTS, LANES), F32),
        ],
        scratch_shapes=[pltpu.VMEM((N_EXPERTS, LANES), F32)],
        compiler_params=_params("arbitrary"),
        name="route",
    )(x, g_ffn, w_r, b_r, count_in)


DMA_UNROLL = 8


def _scatter_rows(d0_ref, d1_ref, x_ref, pad_out, sem, tile):
    def row_copy(r, dst_row):
        return pltpu.make_async_copy(x_ref.at[pl.ds(r, 1), :], pad_out.at[pl.ds(dst_row, 1), :], sem)

    for r in range(tile):
        row_copy(r, d0_ref[0, 0, r]).start(priority=0)
        row_copy(r, d1_ref[0, 0, r]).start(priority=1)

    def wait(r, carry):
        row_copy(0, 0).wait()
        row_copy(0, 0).wait()
        return carry

    lax.fori_loop(0, tile, wait, 0, unroll=DMA_UNROLL)


def _dispatch_kernel(zero_ref, d0p_ref, d1p_ref, d0s_ref, d1s_ref, xp_ref, xs_ref, pad_out, zero_scr, sem, zsem,
                     *, tile, n_tiles):
    i = pl.program_id(0)

    @pl.when(i == 0)
    def _():
        zero_scr[...] = jnp.zeros_like(zero_scr)

        def zero_copy(e):
            row0 = pl.multiple_of(zero_ref[e], EXPERT_ROWS)
            return pltpu.make_async_copy(zero_scr, pad_out.at[pl.ds(row0, EXPERT_ROWS), :], zsem)

        for e in range(zero_ref.shape[0]):
            @pl.when(zero_ref[e] >= 0)
            def _():
                zero_copy(e).start()
        for e in range(zero_ref.shape[0]):
            @pl.when(zero_ref[e] >= 0)
            def _():
                zero_copy(e).wait()

    @pl.when(i < n_tiles)
    def _():
        _scatter_rows(d0p_ref, d1p_ref, xp_ref, pad_out, sem, tile)

    @pl.when(i == n_tiles)
    def _():
        _scatter_rows(d0s_ref, d1s_ref, xs_ref, pad_out, sem, xs_ref.shape[0])


def _dispatch(zero_rows, dest_p, dest_s, xp, xs, n_rows, tile):
    n_tiles = xp.shape[0] // tile
    n_s = xs.shape[0]
    last = n_tiles - 1
    tile_spec = pl.BlockSpec((1, 1, tile), lambda i: (jnp.minimum(i, last), 0, 0), memory_space=pltpu.SMEM)
    return pl.pallas_call(
        functools.partial(_dispatch_kernel, tile=tile, n_tiles=n_tiles),
        grid=(n_tiles + 1,),
        in_specs=[
            pl.BlockSpec(memory_space=pltpu.SMEM),
            tile_spec, tile_spec,
            pl.BlockSpec(memory_space=pltpu.SMEM),
            pl.BlockSpec(memory_space=pltpu.SMEM),
            pl.BlockSpec((tile, D_MODEL), lambda i: (jnp.minimum(i, last), 0)),
            pl.BlockSpec((n_s, D_MODEL), lambda i: (0, 0)),
        ],
        out_specs=pl.BlockSpec(memory_space=pl.ANY),
        out_shape=jax.ShapeDtypeStruct((n_rows, D_MODEL), F32),
        scratch_shapes=[pltpu.VMEM((EXPERT_ROWS, D_MODEL), F32), pltpu.SemaphoreType.DMA(()),
                        pltpu.SemaphoreType.DMA(())],
        compiler_params=_params("arbitrary"),
        name="dispatch",
    )(zero_rows, dest_p[0].reshape(n_tiles, 1, tile), dest_p[1].reshape(n_tiles, 1, tile),
      dest_s[0].reshape(1, 1, n_s), dest_s[1].reshape(1, 1, n_s), xp, xs)


def _expert_kernel(be_ref, nused_ref, x_ref, g_ref, wg_ref, wu_ref, wd_ref, y_ref, wg_s, wu_s, wd_s):
    i = pl.program_id(0)
    prev = be_ref[jnp.maximum(i - 1, 0)]
    active = i < nused_ref[0]

    @pl.when(jnp.logical_and(active, jnp.logical_or(i == 0, be_ref[i] != prev)))
    def _():
        wg_s[...] = wg_ref[...].astype(BF16)
        wu_s[...] = wu_ref[...].astype(BF16)
        wd_s[...] = wd_ref[...].astype(BF16)

    @pl.when(active)
    def _():
        hn = _rms(x_ref[...], g_ref[...]).astype(BF16)
        gate = jnp.dot(hn, wg_s[...], preferred_element_type=F32)
        up = jnp.dot(hn, wu_s[...], preferred_element_type=F32)
        act = (gate * jax.nn.sigmoid(gate)) * up
        y_ref[...] = jnp.dot(act.astype(BF16), wd_s[...], preferred_element_type=F32)


def _experts(block_expert, n_used, x_pad, g_ffn, w_gate, w_up, w_down):
    n_rows = x_pad.shape[0]
    nb = n_rows // EXPERT_ROWS
    used_block = lambda i, be, nu: (jnp.minimum(i, nu[0] - 1), 0)
    grid_spec = pltpu.PrefetchScalarGridSpec(
        num_scalar_prefetch=2,
        grid=(nb,),
        in_specs=[
            pl.BlockSpec((EXPERT_ROWS, D_MODEL), used_block),
            pl.BlockSpec((1, D_MODEL), lambda i, be, nu: (0, 0)),
            pl.BlockSpec((None, None, D_MODEL, D_EXPERT), lambda i, be, nu: (0, be[i], 0, 0)),
            pl.BlockSpec((None, None, D_MODEL, D_EXPERT), lambda i, be, nu: (0, be[i], 0, 0)),
            pl.BlockSpec((None, None, D_EXPERT, D_MODEL), lambda i, be, nu: (0, be[i], 0, 0)),
        ],
        out_specs=pl.BlockSpec((EXPERT_ROWS, D_MODEL), used_block),
        scratch_shapes=[
            pltpu.VMEM((D_MODEL, D_EXPERT), BF16),
            pltpu.VMEM((D_MODEL, D_EXPERT), BF16),
            pltpu.VMEM((D_EXPERT, D_MODEL), BF16),
        ],
    )
    return pl.pallas_call(
        _expert_kernel,
        grid_spec=grid_spec,
        out_shape=jax.ShapeDtypeStruct((n_rows, D_MODEL), F32),
        input_output_aliases={2: 0},
        compiler_params=_params("arbitrary"),
        name="experts",
    )(block_expert, n_used, x_pad, g_ffn, w_gate, w_up, w_down)


def _final_kernel(d0_ref, d1_ref, d0n_ref, d1n_ref, x_ref, info_ref, p_ref, y_hbm, wpg_ref, wpp_ref, gple_ref,
                  gfin_ref, out_ref, ybuf, sem):
    tm = x_ref.shape[0]
    i = pl.program_id(0)
    buf = i % 2

    def row_copy(src_row, b, k, r):
        return pltpu.make_async_copy(y_hbm.at[pl.ds(src_row, 1), :], ybuf.at[b, k, pl.ds(r, 1), :], sem.at[b])

    def gather(da_ref, db_ref, b):
        def start(r, carry):
            row_copy(da_ref[0, 0, r], b, 0, r).start(priority=0)
            row_copy(db_ref[0, 0, r], b, 1, r).start(priority=1)
            return carry
        lax.fori_loop(0, tm, start, 0, unroll=DMA_UNROLL)

    def wait_all(b):
        def wait(r, carry):
            row_copy(0, b, 0, 0).wait()
            row_copy(0, b, 1, 0).wait()
            return carry
        lax.fori_loop(0, tm, wait, 0, unroll=DMA_UNROLL)

    @pl.when(i == 0)
    def _():
        gather(d0_ref, d1_ref, 0)

    wait_all(buf)
    for r in range(tm):
        row_copy(d0n_ref[0, 0, r], 1 - buf, 0, r).start()
        row_copy(d1n_ref[0, 0, r], 1 - buf, 1, r).start()

    info = info_ref[...]
    x2 = x_ref[...] + (ybuf[buf, 0] * info[:, 2:3] + ybuf[buf, 1] * info[:, 3:4])
    gate = jax.nn.sigmoid(jnp.dot(_rms(x2, gple_ref[...]).astype(BF16), wpg_ref[...], preferred_element_type=F32))
    proj = jnp.dot(p_ref[...].astype(BF16), wpp_ref[...], preferred_element_type=F32)
    out_ref[...] = _rms(x2 + gate * proj, gfin_ref[...])

    @pl.when(i == pl.num_programs(0) - 1)
    def _():
        wait_all(1 - buf)


def _final(dest, x1, info, p, y_pad, w_pg, w_pp, g_ple, g_final, tm):
    n = x1.shape[0]
    d0 = dest[0].reshape(n // tm, 1, tm)
    d1 = dest[1].reshape(n // tm, 1, tm)
    const = lambda i: (0, 0)
    last = n // tm - 1
    cur = pl.BlockSpec((1, 1, tm), lambda i: (i, 0, 0), memory_space=pltpu.SMEM)
    nxt = pl.BlockSpec((1, 1, tm), lambda i: (jnp.minimum(i + 1, last), 0, 0), memory_space=pltpu.SMEM)
    return pl.pallas_call(
        _final_kernel,
        grid=(n // tm,),
        in_specs=[
            cur, cur, nxt, nxt,
            pl.BlockSpec((tm, D_MODEL), lambda i: (i, 0)),
            pl.BlockSpec((tm, LANES), lambda i: (i, 0)),
            pl.BlockSpec((tm, PLE_DIM), lambda i: (i, 0)),
            pl.BlockSpec(memory_space=pl.ANY),
            pl.BlockSpec((D_MODEL, D_MODEL), const),
            pl.BlockSpec((PLE_DIM, D_MODEL), const),
            pl.BlockSpec((1, D_MODEL), const),
            pl.BlockSpec((1, D_MODEL), const),
        ],
        out_specs=pl.BlockSpec((tm, D_MODEL), lambda i: (i, 0)),
        out_shape=jax.ShapeDtypeStruct((n, D_MODEL), F32),
        scratch_shapes=[pltpu.VMEM((2, 2, tm, D_MODEL), F32), pltpu.SemaphoreType.DMA((2,))],
        compiler_params=_params("arbitrary"),
        name="final",
    )(d0, d1, d0, d1, x1, info, p, y_pad, w_pg, w_pp, g_ple, g_final)


def _pad_lanes(a):
    return jnp.pad(a, ((0, 0), (0, LANES - a.shape[1])))


def kernel(x_prompt, x_sample, p_prompt, p_sample, state_mlstm_C, state_mlstm_n, state_mlstm_m, state_swa_k, state_swa_v, g_mix, w_in, b_igate, b_fgate, g_mh, attn_sinks, w_out, g_ffn, w_router_g, b_router_g, w_router_e, b_router_e, w_gate, w_up, w_down, g_ple, w_ple_gate, w_ple_proj, g_final):
    batch, seq, d = x_prompt.shape
    dec = x_sample.shape[0]
    n_p = batch * seq
    xp = x_prompt.reshape(n_p, d)
    xs = x_sample.reshape(dec, d)

    w_in0 = w_in[0]
    head_order = jnp.array(SWA_HEAD_ORDER, jnp.int32)
    w_aq = w_in0[:, ATT_OFF:ATT_OFF + A_Q_W].reshape(d, A_HEADS, A_HEAD_DIM)[:, head_order].reshape(d, A_Q_W)
    zf_pad = ZF_W - (ZF_GATE + LANES)
    w_cat = jnp.concatenate(
        [w_in0[:, :3 * M_W], w_aq,
         w_in0[:, 3 * M_W:GATE_OFF], w_in0[:, ATT_OFF + A_Q_W:],
         _pad_lanes(w_in0[:, GATE_OFF:ATT_OFF]), jnp.zeros((d, zf_pad), F32)],
        axis=1).astype(BF16)
    gate_bias8 = jnp.concatenate([b_igate, b_fgate], axis=1)
    gate_bias = _pad_lanes(gate_bias8)
    w_oa = w_out[0, M_W:].reshape(A_HEADS, A_HEAD_DIM, d)[head_order].reshape(A_Q_W, d)
    w_o = jnp.concatenate([w_out[0, :M_W], w_oa], axis=0).astype(BF16)
    kv_of_head = jnp.arange(A_HEADS, dtype=jnp.int32) // A_GROUP
    head_mask = (kv_of_head[:, None] == jnp.arange(A_KV_HEADS, dtype=jnp.int32)[None, :]).astype(F32)
    gap = EXPERT_LANE0 - N_GROUPS
    w_r = _pad_lanes(jnp.concatenate([w_router_g[0], jnp.zeros((d, gap), F32), w_router_e[0]], axis=1))
    b_r = _pad_lanes(jnp.concatenate([b_router_g, jnp.zeros((1, gap), F32), b_router_e], axis=1))
    w_pg = w_ple_gate[0].astype(BF16)
    w_pp = w_ple_proj[0].astype(BF16)
    sink_col = attn_sinks.reshape(A_HEADS, 1)

    zb, zf = _inproj_split(xp, g_mix, w_cat, tm=1024, tn=1024)
    om_p, c_p, n_p8, m_p8 = _mlstm_prompt(zb, zf, gate_bias, g_mh, batch, seq)
    oa_p = _swa_prompt(zb, zf, attn_sinks, batch, seq)
    x1p = _outproj(xp, om_p, oa_p, w_o, tm=512)

    zs = _inproj(xs, g_mix, w_in0, tm=dec, tn=512, precision=HI)
    om_s, c_s, n_s, m_s = _mlstm_sample(
        zs.reshape(dec, 1, zs.shape[1]), gate_bias8,
        state_mlstm_C[0], state_mlstm_n[0].reshape(dec, 1, M_W), state_mlstm_m[0].reshape(dec, 1, M_HEADS), g_mh)
    ks_off = ATT_OFF + A_Q_W
    vs_off = ks_off + A_KV_W
    q_heads = zs[:, ATT_OFF:ks_off].reshape(dec, A_HEADS, 1, A_HEAD_DIM)
    qx = (q_heads * head_mask[None, :, :, None]).reshape(dec, A_HEADS, A_KV_W)
    ox_s, k_s, v_s = _swa_sample(
        qx,
        zs[:, ks_off:vs_off].reshape(dec, 1, A_KV_W),
        zs[:, vs_off:].reshape(dec, 1, A_KV_W),
        state_swa_k[0].reshape(dec, WINDOW, A_KV_W), state_swa_v[0].reshape(dec, WINDOW, A_KV_W), sink_col)
    oa_s = jnp.sum(ox_s.reshape(dec, A_HEADS, A_KV_HEADS, A_HEAD_DIM) * head_mask[None, :, :, None], axis=2)
    x1s = _outproj(xs, om_s.reshape(dec, M_W), oa_s.reshape(dec, A_Q_W), w_out[0], tm=dec, precision=HI)

    info_p, rows_p, cnt_p = _route(x1p, g_ffn, w_r, b_r, jnp.zeros((N_EXPERTS, LANES), F32), tm=512)
    info_s, rows_s, cnt = _route(x1s, g_ffn, w_r, b_r, cnt_p, tm=dec)
    counts = cnt[:, 0].astype(jnp.int32)
    padded = ((counts + EXPERT_ROWS - 1) // EXPERT_ROWS) * EXPERT_ROWS
    pad_end = jnp.cumsum(padded)
    pad_start = pad_end - padded
    n_assign = 2 * (n_p + dec)
    n_blocks = -(-n_assign // EXPERT_ROWS) + N_EXPERTS
    block_row0 = jnp.arange(n_blocks, dtype=jnp.int32) * EXPERT_ROWS
    block_expert = jnp.minimum(
        jnp.sum((pad_end[None, :] <= block_row0[:, None]).astype(jnp.int32), axis=1), N_EXPERTS - 1)
    n_used = (pad_end[-1:] // EXPERT_ROWS).astype(jnp.int32)
    spare = n_used + jnp.arange(N_EXPERTS, dtype=jnp.int32)
    zero_rows = jnp.concatenate([
        jnp.where(padded > 0, pad_end - EXPERT_ROWS, -1),
        jnp.where(spare < n_blocks, spare * EXPERT_ROWS, -1)]).astype(jnp.int32)

    def dest_of(rows):
        e = rows[0:2].astype(jnp.int32)
        hit = e[:, :, None] == jnp.arange(N_EXPERTS, dtype=jnp.int32)
        return jnp.sum(jnp.where(hit, pad_start, 0), axis=2) + rows[4:6].astype(jnp.int32)

    dest_p = dest_of(rows_p)
    dest_s = dest_of(rows_s)

    x_pad = _dispatch(zero_rows, dest_p, dest_s, x1p, x1s, n_blocks * EXPERT_ROWS, tile=512)
    y_pad = _experts(block_expert, n_used, x_pad, g_ffn, w_gate, w_up, w_down)
    y_p = _final(dest_p, x1p, info_p, p_prompt[0].reshape(n_p, PLE_DIM), y_pad, w_pg, w_pp, g_ple, g_final.reshape(1, d), tm=256)
    y_s = _final(dest_s, x1s, info_s, p_sample[0].reshape(dec, PLE_DIM), y_pad, w_pg, w_pp, g_ple, g_final.reshape(1, d), tm=dec)

    zf3 = zf.reshape(batch, seq, ZF_W)
    swa_k_p = zf3[:, seq - WINDOW:, ZF_K:ZF_V].reshape(1, batch, WINDOW, A_KV_HEADS, A_HEAD_DIM)
    swa_v_p = zf3[:, seq - WINDOW:, ZF_V:ZF_GATE].reshape(1, batch, WINDOW, A_KV_HEADS, A_HEAD_DIM)
    return (
        y_p.reshape(batch, seq, d),
        y_s.reshape(dec, 1, d),
        c_p[None],
        n_p8[None, :, :M_HEADS, :],
        m_p8[None, :, :M_HEADS, 0],
        swa_k_p,
        swa_v_p,
        c_s[None],
        n_s.reshape(1, dec, M_HEADS, M_DK),
        m_s[None, :, 0, :M_HEADS],
        k_s.reshape(1, dec, WINDOW, A_KV_HEADS, A_HEAD_DIM),
        v_s.reshape(1, dec, WINDOW, A_KV_HEADS, A_HEAD_DIM),
    )
```

```python
import functools

import jax
import jax.numpy as jnp
from jax import lax
from jax.experimental import pallas as pl
from jax.experimental.pallas import tpu as pltpu

F32 = jnp.float32
BF16 = jnp.bfloat16
HI = lax.Precision.HIGHEST

D_MODEL = 2048
M_HEADS = 4
M_DK = 256
M_DV = 256
A_HEADS = 16
A_KV_HEADS = 4
A_HEAD_DIM = 64
A_GROUP = A_HEADS // A_KV_HEADS
WINDOW = 128
M_W = M_HEADS * M_DK
A_Q_W = A_HEADS * A_HEAD_DIM
A_KV_W = A_KV_HEADS * A_HEAD_DIM
GATE_OFF = 4 * M_W
ATT_OFF = GATE_OFF + 2 * M_HEADS
ZB_W = 3 * M_W + A_Q_W
ZF_MO, ZF_K, ZF_V, ZF_GATE = 0, M_W, M_W + A_KV_W, M_W + 2 * A_KV_W
ZF_W = 2 * M_W
N_GROUPS = 4
EXPERTS_PER_GROUP = 8
N_EXPERTS = N_GROUPS * EXPERTS_PER_GROUP
D_EXPERT = 512
PLE_DIM = 256
EPS = 1e-6

LANES = 128
SUBLANES = 8
INPROJ_ROWS, INPROJ_COLS = 1024, 1024
OUTPROJ_ROWS = 512
ROUTE_ROWS = 512
DISPATCH_ROWS = 512
FINAL_ROWS = 256
MLSTM_CHUNK = 256
EXPERT_ROWS = 256
VMEM_LIMIT = 56 * 1024 * 1024
NT_DIMS = (((1,), (1,)), ((), ()))
TN_DIMS = (((0,), (0,)), ((), ()))


def _params(*sem):
    return pltpu.CompilerParams(dimension_semantics=sem, vmem_limit_bytes=VMEM_LIMIT)


def _rms(x, g):
    r = lax.rsqrt(jnp.mean(x * x, axis=-1, keepdims=True) + EPS)
    return (x * r) * g


def _log_sigmoid(x):
    return jnp.minimum(x, 0.0) - jnp.log1p(jnp.exp(-jnp.abs(x)))


def _inproj_kernel(x_ref, g_ref, w_ref, z_ref, hn_scr, *, precision):
    @pl.when(pl.program_id(1) == 0)
    def _():
        hn_scr[...] = _rms(x_ref[...], g_ref[...]).astype(hn_scr.dtype)

    z_ref[...] = jnp.dot(hn_scr[...], w_ref[...], precision=precision, preferred_element_type=F32)


def _inproj(x, g_mix, w, tm, tn, precision=None):
    n = x.shape[0]
    width = w.shape[1]
    return pl.pallas_call(
        functools.partial(_inproj_kernel, precision=precision),
        grid=(n // tm, pl.cdiv(width, tn)),
        in_specs=[
            pl.BlockSpec((tm, D_MODEL), lambda i, j: (i, 0)),
            pl.BlockSpec((1, D_MODEL), lambda i, j: (0, 0)),
            pl.BlockSpec((D_MODEL, tn), lambda i, j: (0, j)),
        ],
        out_specs=pl.BlockSpec((tm, tn), lambda i, j: (i, j)),
        out_shape=jax.ShapeDtypeStruct((n, width), F32),
        scratch_shapes=[pltpu.VMEM((tm, D_MODEL), w.dtype)],
        compiler_params=_params("parallel", "arbitrary"),
        name="inproj",
    )(x, g_mix, w)


def _inproj_split_kernel(x_ref, g_ref, w_ref, zb_ref, zf_ref, hn_scr, *, n_bf16_blocks):
    j = pl.program_id(1)

    @pl.when(j == 0)
    def _():
        hn_scr[...] = _rms(x_ref[...], g_ref[...]).astype(BF16)

    acc = jnp.dot(hn_scr[...], w_ref[...], preferred_element_type=F32)

    @pl.when(j < n_bf16_blocks)
    def _():
        zb_ref[...] = acc.astype(BF16)

    @pl.when(j >= n_bf16_blocks)
    def _():
        zf_ref[...] = acc


def _inproj_split(x, g_mix, w, tm, tn):
    n = x.shape[0]
    nb = ZB_W // tn
    nf = ZF_W // tn
    return pl.pallas_call(
        functools.partial(_inproj_split_kernel, n_bf16_blocks=nb),
        grid=(n // tm, nb + nf),
        in_specs=[
            pl.BlockSpec((tm, D_MODEL), lambda i, j: (i, 0)),
            pl.BlockSpec((1, D_MODEL), lambda i, j: (0, 0)),
            pl.BlockSpec((D_MODEL, tn), lambda i, j: (0, j)),
        ],
        out_specs=[
            pl.BlockSpec((tm, tn), lambda i, j: (i, jnp.minimum(j, nb - 1))),
            pl.BlockSpec((tm, tn), lambda i, j: (i, jnp.maximum(j - nb, 0))),
        ],
        out_shape=[
            jax.ShapeDtypeStruct((n, ZB_W), BF16),
            jax.ShapeDtypeStruct((n, ZF_W), F32),
        ],
        scratch_shapes=[pltpu.VMEM((tm, D_MODEL), BF16)],
        compiler_params=_params("parallel", "arbitrary"),
        name="inproj_prompt",
    )(x, g_mix, w)


def _mlstm_out(h, gmh, mo):
    hn = h * lax.rsqrt(jnp.mean(h * h, axis=-1, keepdims=True) + EPS)
    return (hn * gmh) * jax.nn.sigmoid(mo)


def _gates(g_raw, bias):
    g = g_raw + bias
    lane = lax.broadcasted_iota(jnp.int32, g.shape, g.ndim - 1)
    return jnp.where(lane < M_HEADS, g, _log_sigmoid(g))


def _mlstm_kernel(q_ref, k_ref, v_ref, mo_ref, g_ref, bias_ref, gmh_ref,
                  o_ref, c_out, n_out, m_out, c_scr, n_scr, m_scr, *, chunk, n_chunks):
    c_idx = pl.program_id(1)

    @pl.when(c_idx == 0)
    def _():
        c_scr[...] = jnp.zeros_like(c_scr)
        n_scr[...] = jnp.zeros_like(n_scr)
        m_scr[...] = jnp.zeros_like(m_scr)

    L = chunk
    row = lax.broadcasted_iota(jnp.int32, (L, L), 0)
    col = lax.broadcasted_iota(jnp.int32, (L, L), 1)
    causal = row >= col
    ltri = causal.astype(F32)
    utri = (row <= col).astype(F32)
    g_c = _gates(g_ref[...], bias_ref[...])
    f_c = jnp.dot(ltri, g_c, precision=HI, preferred_element_type=F32)
    g_r = g_c.T[0:2 * M_HEADS, :]
    f_r = jnp.dot(g_r, utri, precision=HI, preferred_element_type=F32)

    for h in range(M_HEADS):
        hs = slice(h * M_DK, (h + 1) * M_DK)
        hf = M_HEADS + h
        a_row = g_r[h:h + 1, :] - f_r[hf:hf + 1, :]
        f_col = f_c[:, hf:hf + 1]
        a_col = g_c[:, h:h + 1] - f_col
        m0 = m_scr[h:h + 1, 0:1]
        a_max = jnp.max(jnp.where(causal, a_row, -jnp.inf), axis=1, keepdims=True)
        m_t = f_col + jnp.maximum(m0, a_max)
        dec0 = jnp.exp(f_col + m0 - m_t)
        d = jnp.exp(jnp.where(causal, (f_col - m_t) + a_row, -jnp.inf))
        qb = q_ref[:, hs]
        kb = k_ref[:, hs] * (M_DK ** -0.5)
        vb = v_ref[:, hs]
        q = qb.astype(F32)
        k = kb.astype(F32)
        w = lax.dot_general(qb, kb, NT_DIMS, preferred_element_type=F32) * d
        c0 = c_scr[h]
        n0 = n_scr[h:h + 1, :]
        num = jnp.dot(w.astype(BF16), vb, preferred_element_type=F32) + dec0 * lax.dot_general(
            qb, c0.astype(BF16), NT_DIMS, preferred_element_type=F32)
        den = jnp.sum(w, axis=1, keepdims=True) + dec0 * jnp.sum(q * n0, axis=1, keepdims=True)
        hh = num / jnp.maximum(jnp.abs(den), jnp.exp(-m_t))
        m_end = m_t[L - 1:L, :]
        f_end = f_col[L - 1:L, :]
        w_end = jnp.exp(f_end + a_col - m_end)
        dec_end = jnp.exp(f_end + m0 - m_end)
        kw = k * w_end
        c_scr[h] = dec_end * c0 + lax.dot_general(vb, kw.astype(BF16), TN_DIMS, preferred_element_type=F32)
        n_scr[h:h + 1, :] = dec_end * n0 + jnp.sum(kw, axis=0, keepdims=True)
        m_scr[h:h + 1, :] = jnp.broadcast_to(m_end, (1, LANES))
        o_ref[:, hs] = _mlstm_out(hh, gmh_ref[:, hs], mo_ref[:, hs]).astype(o_ref.dtype)

    @pl.when(c_idx == n_chunks - 1)
    def _():
        c_out[0] = c_scr[...]
        n_out[0] = n_scr[...]
        m_out[0] = m_scr[...]


def _mlstm_prompt(zb, zf, gate_bias, g_mh, batch, seq):
    L = MLSTM_CHUNK
    nc = seq // L
    rows = lambda b, c: b * nc + c
    gate_blk = ZF_GATE // LANES
    return pl.pallas_call(
        functools.partial(_mlstm_kernel, chunk=L, n_chunks=nc),
        grid=(batch, nc),
        in_specs=[
            pl.BlockSpec((L, M_W), lambda b, c: (rows(b, c), 0)),
            pl.BlockSpec((L, M_W), lambda b, c: (rows(b, c), 1)),
            pl.BlockSpec((L, M_W), lambda b, c: (rows(b, c), 2)),
            pl.BlockSpec((L, M_W), lambda b, c: (rows(b, c), ZF_MO // M_W)),
            pl.BlockSpec((L, LANES), lambda b, c: (rows(b, c), gate_blk)),
            pl.BlockSpec((1, LANES), lambda b, c: (0, 0)),
            pl.BlockSpec((1, M_W), lambda b, c: (0, 0)),
        ],
        out_specs=[
            pl.BlockSpec((L, M_W), lambda b, c: (rows(b, c), 0)),
            pl.BlockSpec((1, M_HEADS, M_DV, M_DK), lambda b, c: (b, 0, 0, 0)),
            pl.BlockSpec((1, 8, M_DK), lambda b, c: (b, 0, 0)),
            pl.BlockSpec((1, 8, LANES), lambda b, c: (b, 0, 0)),
        ],
        out_shape=[
            jax.ShapeDtypeStruct((batch * seq, M_W), BF16),
            jax.ShapeDtypeStruct((batch, M_HEADS, M_DV, M_DK), F32),
            jax.ShapeDtypeStruct((batch, 8, M_DK), F32),
            jax.ShapeDtypeStruct((batch, 8, LANES), F32),
        ],
        scratch_shapes=[
            pltpu.VMEM((M_HEADS, M_DV, M_DK), F32),
            pltpu.VMEM((8, M_DK), F32),
            pltpu.VMEM((8, LANES), F32),
        ],
        compiler_params=_params("parallel", "arbitrary"),
        name="mlstm_prompt",
    )(zb, zb, zb, zf, zf, gate_bias, g_mh)


MLSTM_STEP_TOKENS = 4


def _mlstm_step_kernel(z_ref, bias_ref, c_ref, n_ref, m_ref, gmh_ref, o_ref, c_out, n_out, m_out):
    row = lax.broadcasted_iota(jnp.int32, (M_DV, M_DV), 0)
    col = lax.broadcasted_iota(jnp.int32, (M_DV, M_DV), 1)
    eye = row == col
    for t in range(z_ref.shape[0]):
        _mlstm_step_token(t, eye, z_ref, bias_ref, c_ref, n_ref, m_ref, gmh_ref, o_ref, c_out, n_out, m_out)


def _mlstm_step_token(t, eye, z_ref, bias_ref, c_ref, n_ref, m_ref, gmh_ref, o_ref, c_out, n_out, m_out):
    z = z_ref[t]
    gates = _gates(z[:, GATE_OFF:ATT_OFF], bias_ref[...])
    m_in = m_ref[t]
    lane = lax.broadcasted_iota(jnp.int32, (1, LANES), 1)
    m_new = jnp.zeros((1, LANES), F32)
    for h in range(M_HEADS):
        hs = slice(h * M_DK, (h + 1) * M_DK)
        q = z[:, h * M_DK:(h + 1) * M_DK]
        k = z[:, M_W + h * M_DK:M_W + (h + 1) * M_DK] * (M_DK ** -0.5)
        v = z[:, 2 * M_W + h * M_DV:2 * M_W + (h + 1) * M_DV]
        mo = z[:, 3 * M_W + h * M_DV:3 * M_W + (h + 1) * M_DV]
        ip = gates[:, h:h + 1]
        lf = gates[:, M_HEADS + h:M_HEADS + h + 1]
        m0 = m_in[:, h:h + 1]
        m_t = jnp.maximum(lf + m0, ip)
        dec0 = jnp.exp(lf + m0 - m_t)
        d = jnp.exp(ip - m_t)
        c0 = c_ref[t, h]
        n0 = n_ref[t, :, hs]
        w = jnp.sum(q * k, axis=1, keepdims=True) * d
        cq_col = jnp.sum(c0 * q, axis=1, keepdims=True)
        cq = jnp.sum(jnp.where(eye, cq_col, 0.0), axis=0, keepdims=True)
        num = w * v + dec0 * cq
        den = w + dec0 * jnp.sum(n0 * q, axis=1, keepdims=True)
        hh = num / jnp.maximum(jnp.abs(den), jnp.exp(-m_t))
        v_col = jnp.sum(jnp.where(eye, v, 0.0), axis=1, keepdims=True)
        c_out[t, h] = dec0 * c0 + (d * v_col) * k
        n_out[t, :, hs] = dec0 * n0 + d * k
        m_new = jnp.where(lane == h, m_t, m_new)
        o_ref[t, :, hs] = _mlstm_out(hh, gmh_ref[:, hs], mo).astype(o_ref.dtype)
    m_out[t] = m_new


def _mlstm_sample(z3, gate_bias, c0, n0, m0, g_mh):
    nb = z3.shape[0]
    nt = MLSTM_STEP_TOKENS
    tok = lambda b: (b, 0, 0)
    return pl.pallas_call(
        _mlstm_step_kernel,
        grid=(nb // nt,),
        in_specs=[
            pl.BlockSpec((nt, 1, z3.shape[2]), tok),
            pl.BlockSpec((1, 2 * M_HEADS), lambda b: (0, 0)),
            pl.BlockSpec((nt, M_HEADS, M_DV, M_DK), lambda b: (b, 0, 0, 0)),
            pl.BlockSpec((nt, 1, M_W), tok),
            pl.BlockSpec((nt, 1, M_HEADS), tok),
            pl.BlockSpec((1, M_W), lambda b: (0, 0)),
        ],
        out_specs=[
            pl.BlockSpec((nt, 1, M_W), tok),
            pl.BlockSpec((nt, M_HEADS, M_DV, M_DK), lambda b: (b, 0, 0, 0)),
            pl.BlockSpec((nt, 1, M_W), tok),
            pl.BlockSpec((nt, 1, LANES), tok),
        ],
        out_shape=[
            jax.ShapeDtypeStruct((nb, 1, M_W), F32),
            jax.ShapeDtypeStruct((nb, M_HEADS, M_DV, M_DK), F32),
            jax.ShapeDtypeStruct((nb, 1, M_W), F32),
            jax.ShapeDtypeStruct((nb, 1, LANES), F32),
        ],
        compiler_params=_params("parallel"),
        name="mlstm_sample",
    )(z3, gate_bias, c0, n0, m0, g_mh)


_EVEN_KV = [h for h in range(A_HEADS) if (h // A_GROUP) % 2 == 0]
_ODD_KV = [h for h in range(A_HEADS) if (h // A_GROUP) % 2 == 1]
SWA_HEAD_ORDER = tuple(h for pair in zip(_EVEN_KV, _ODD_KV) for h in pair)
HALF = LANES // 2
KV_SPAN = 2 * WINDOW


def _swa_kernel(sink_ref, q_ref, kc_ref, kp_ref, vc_ref, vp_ref, o_ref, qx_scr, p_scr):
    j = pl.program_id(1)
    lo_half = lax.broadcasted_iota(jnp.int32, (WINDOW, LANES), 1) < HALF
    zero_tile = jnp.zeros((WINDOW, LANES), BF16)
    for slot, head in enumerate(SWA_HEAD_ORDER):
        kv = head // A_GROUP
        tile = q_ref[:, (slot // 2) * LANES:(slot // 2 + 1) * LANES]
        keep = lo_half if slot % 2 == 0 else jnp.logical_not(lo_half)
        qm = jnp.where(keep, tile * (A_HEAD_DIM ** -0.5), jnp.zeros_like(tile))
        parts = [qm, zero_tile] if kv // 2 == 0 else [zero_tile, qm]
        qx_scr[slot * WINDOW:(slot + 1) * WINDOW, :] = jnp.concatenate(parts, axis=1)

    k2 = jnp.concatenate([kp_ref[...], kc_ref[...]], axis=0).astype(BF16)
    v2 = jnp.concatenate([vp_ref[...], vc_ref[...]], axis=0).astype(BF16)
    s_all = lax.dot_general(qx_scr[...], k2, NT_DIMS, preferred_element_type=F32)

    t = lax.broadcasted_iota(jnp.int32, (WINDOW, KV_SPAN), 0)
    kk = lax.broadcasted_iota(jnp.int32, (WINDOW, KV_SPAN), 1)
    in_prev = jnp.logical_and(jnp.logical_and(kk < WINDOW, kk > t), j > 0)
    in_cur = jnp.logical_and(kk >= WINDOW, kk - WINDOW <= t)
    allowed = jnp.logical_or(in_prev, in_cur)
    for slot, head in enumerate(SWA_HEAD_ORDER):
        rows = slice(slot * WINDOW, (slot + 1) * WINDOW)
        s = jnp.where(allowed, s_all[rows, :], -jnp.inf)
        sink = sink_ref[0, head]
        mx = jnp.maximum(jnp.max(s, axis=1, keepdims=True), sink)
        e = jnp.exp(s - mx)
        den = jnp.sum(e, axis=1, keepdims=True) + jnp.exp(sink - mx)
        p_scr[rows, :] = (e * (1.0 / den)).astype(BF16)

    o_all = jnp.dot(p_scr[...], v2, preferred_element_type=F32)
    for c in range(A_HEADS // 2):
        tiles = []
        for slot in (2 * c, 2 * c + 1):
            kv = SWA_HEAD_ORDER[slot] // A_GROUP
            tiles.append(o_all[slot * WINDOW:(slot + 1) * WINDOW, (kv // 2) * LANES:(kv // 2 + 1) * LANES])
        o_ref[:, c * LANES:(c + 1) * LANES] = jnp.where(lo_half, tiles[0], tiles[1]).astype(o_ref.dtype)


def _swa_prompt(zb, zf, sinks, batch, seq):
    nb = seq // WINDOW
    q_blk = (3 * M_W) // A_Q_W
    k_blk = ZF_K // A_KV_W
    cur = lambda b, j: b * nb + j
    prev = lambda b, j: b * nb + jnp.maximum(j - 1, 0)
    return pl.pallas_call(
        _swa_kernel,
        grid=(batch, nb),
        in_specs=[
            pl.BlockSpec(memory_space=pltpu.SMEM),
            pl.BlockSpec((WINDOW, A_Q_W), lambda b, j: (cur(b, j), q_blk)),
            pl.BlockSpec((WINDOW, A_KV_W), lambda b, j: (cur(b, j), k_blk)),
            pl.BlockSpec((WINDOW, A_KV_W), lambda b, j: (prev(b, j), k_blk)),
            pl.BlockSpec((WINDOW, A_KV_W), lambda b, j: (cur(b, j), k_blk + 1)),
            pl.BlockSpec((WINDOW, A_KV_W), lambda b, j: (prev(b, j), k_blk + 1)),
        ],
        out_specs=pl.BlockSpec((WINDOW, A_Q_W), lambda b, j: (cur(b, j), 0)),
        out_shape=jax.ShapeDtypeStruct((batch * seq, A_Q_W), BF16),
        scratch_shapes=[pltpu.VMEM((A_HEADS * WINDOW, A_KV_W), BF16), pltpu.VMEM((A_HEADS * WINDOW, KV_SPAN), BF16)],
        compiler_params=_params("parallel", "arbitrary"),
        name="swa_prompt",
    )(sinks, zb, zf, zf, zf, zf)


SWA_STEP_TOKENS = 8


def _swa_step_kernel(qx_ref, kn_ref, vn_ref, bk_ref, bv_ref, sink_ref, o_ref, ko_ref, vo_ref):
    in_window = lax.broadcasted_iota(jnp.int32, (1, WINDOW), 1) >= 1
    scale = A_HEAD_DIM ** -0.5
    sink = sink_ref[...]
    for t in range(qx_ref.shape[0]):
        qx = qx_ref[t]
        bk = bk_ref[t]
        bv = bv_ref[t]
        kn = kn_ref[t]
        vn = vn_ref[t]
        s = lax.dot_general(qx, bk, NT_DIMS, precision=HI, preferred_element_type=F32) * scale
        s = jnp.where(in_window, s, -jnp.inf)
        s_new = jnp.sum(qx * kn, axis=1, keepdims=True) * scale
        mx = jnp.maximum(jnp.maximum(jnp.max(s, axis=1, keepdims=True), s_new), sink)
        e = jnp.exp(s - mx)
        e_new = jnp.exp(s_new - mx)
        den = jnp.sum(e, axis=1, keepdims=True) + e_new + jnp.exp(sink - mx)
        o_ref[t] = jnp.dot(e / den, bv, precision=HI, preferred_element_type=F32) + (e_new / den) * vn
        ko_ref[t, 0:WINDOW - 1, :] = bk[1:WINDOW, :]
        ko_ref[t, WINDOW - 1:WINDOW, :] = kn
        vo_ref[t, 0:WINDOW - 1, :] = bv[1:WINDOW, :]
        vo_ref[t, WINDOW - 1:WINDOW, :] = vn


def _swa_sample(qx, kn3, vn3, buf_k, buf_v, sink_col):
    nb = qx.shape[0]
    nt = SWA_STEP_TOKENS
    tok = lambda b: (b, 0, 0)
    return pl.pallas_call(
        _swa_step_kernel,
        grid=(nb // nt,),
        in_specs=[
            pl.BlockSpec((nt, A_HEADS, A_KV_W), tok),
            pl.BlockSpec((nt, 1, A_KV_W), tok),
            pl.BlockSpec((nt, 1, A_KV_W), tok),
            pl.BlockSpec((nt, WINDOW, A_KV_W), tok),
            pl.BlockSpec((nt, WINDOW, A_KV_W), tok),
            pl.BlockSpec((A_HEADS, 1), lambda b: (0, 0)),
        ],
        out_specs=[
            pl.BlockSpec((nt, A_HEADS, A_KV_W), tok),
            pl.BlockSpec((nt, WINDOW, A_KV_W), tok),
            pl.BlockSpec((nt, WINDOW, A_KV_W), tok),
        ],
        out_shape=[
            jax.ShapeDtypeStruct((nb, A_HEADS, A_KV_W), F32),
            jax.ShapeDtypeStruct((nb, WINDOW, A_KV_W), F32),
            jax.ShapeDtypeStruct((nb, WINDOW, A_KV_W), F32),
        ],
        compiler_params=_params("parallel"),
        name="swa_sample",
    )(qx, kn3, vn3, buf_k, buf_v, sink_col)


GROUP_LANE0 = 0
EXPERT_LANE0 = 8


def _outproj_kernel(x_ref, om_ref, oa_ref, wm_ref, wa_ref, y_ref, *, precision):
    y_ref[...] = (x_ref[...]
                  + jnp.dot(om_ref[...], wm_ref[...], precision=precision, preferred_element_type=F32)
                  + jnp.dot(oa_ref[...], wa_ref[...], precision=precision, preferred_element_type=F32))


def _outproj(x, o_m, o_a, w, tm, precision=None):
    n = x.shape[0]
    return pl.pallas_call(
        functools.partial(_outproj_kernel, precision=precision),
        grid=(n // tm,),
        in_specs=[
            pl.BlockSpec((tm, D_MODEL), lambda i: (i, 0)),
            pl.BlockSpec((tm, M_W), lambda i: (i, 0)),
            pl.BlockSpec((tm, A_Q_W), lambda i: (i, 0)),
            pl.BlockSpec((M_W, D_MODEL), lambda i: (0, 0)),
            pl.BlockSpec((A_Q_W, D_MODEL), lambda i: (1, 0)),
        ],
        out_specs=pl.BlockSpec((tm, D_MODEL), lambda i: (i, 0)),
        out_shape=jax.ShapeDtypeStruct((n, D_MODEL), F32),
        compiler_params=_params("parallel"),
        name="outproj",
    )(x, o_m, o_a, w, w)


def _route_kernel(x_ref, g_ref, wr_ref, br_ref, cin_ref, info_ref, rows_ref, cnt_ref, carry):
    @pl.when(pl.program_id(0) == 0)
    def _():
        carry[...] = cin_ref[...]

    tm = x_ref.shape[0]
    hn = _rms(x_ref[...], g_ref[...])
    w_r = wr_ref[...]
    hn_hi = hn.astype(BF16)
    hn_lo = (hn - hn_hi.astype(F32)).astype(BF16)
    w_hi = w_r.astype(BF16)
    w_lo = (w_r - w_hi.astype(F32)).astype(BF16)
    logits = (jnp.dot(hn_hi, w_hi, preferred_element_type=F32)
              + (jnp.dot(hn_lo, w_hi, preferred_element_type=F32)
                 + jnp.dot(hn_hi, w_lo, preferred_element_type=F32))) + br_ref[...]
    lt = logits.T
    neg = -jnp.inf
    grp = lax.broadcasted_iota(jnp.int32, (SUBLANES, tm), 0)
    gl = jnp.where(grp < N_GROUPS, lt[GROUP_LANE0:GROUP_LANE0 + SUBLANES, :], neg)
    g_max = jnp.max(gl, axis=0, keepdims=True)
    g_idx = jnp.min(jnp.where(gl == g_max, grp, N_GROUPS), axis=0, keepdims=True)
    g_w = 1.0 / jnp.sum(jnp.exp(gl - g_max), axis=0, keepdims=True)
    eid = lax.broadcasted_iota(jnp.int32, (N_EXPERTS, tm), 0)
    first = g_idx * EXPERTS_PER_GROUP
    in_group = jnp.logical_and(eid >= first, eid < first + EXPERTS_PER_GROUP)
    el = jnp.where(in_group, lt[EXPERT_LANE0:EXPERT_LANE0 + N_EXPERTS, :], neg)
    l1 = jnp.max(el, axis=0, keepdims=True)
    i1 = jnp.min(jnp.where(el == l1, eid, N_EXPERTS), axis=0, keepdims=True)
    el2 = jnp.where(eid == i1, neg, el)
    l2 = jnp.max(el2, axis=0, keepdims=True)
    i2 = jnp.min(jnp.where(el2 == l2, eid, N_EXPERTS), axis=0, keepdims=True)
    e21 = jnp.exp(l2 - l1)
    w1 = g_w * (1.0 / (1.0 + e21))
    w2 = g_w * (e21 / (1.0 + e21))
    hit1 = eid == i1
    hit2 = eid == i2
    onehot = jnp.logical_or(hit1, hit2).astype(F32)
    r = lax.broadcasted_iota(jnp.int32, (tm, tm), 0)
    c = lax.broadcasted_iota(jnp.int32, (tm, tm), 1)
    before = (r < c).astype(BF16)
    cum = jnp.dot(onehot.astype(BF16), before, preferred_element_type=F32) + carry[:, 0:1]
    rank1 = jnp.sum(jnp.where(hit1, cum, 0.0), axis=0, keepdims=True)
    rank2 = jnp.sum(jnp.where(hit2, cum, 0.0), axis=0, keepdims=True)
    row = lax.broadcasted_iota(jnp.int32, (LANES, tm), 0)
    rows = jnp.where(row == 0, i1.astype(F32), 0.0)
    rows = jnp.where(row == 1, i2.astype(F32), rows)
    rows = jnp.where(row == 2, w1, rows)
    rows = jnp.where(row == 3, w2, rows)
    rows = jnp.where(row == 4, rank1, rows)
    rows = jnp.where(row == 5, rank2, rows)
    rows_ref[...] = rows[0:8, :]
    info_ref[...] = rows.T
    total = carry[:, 0:1] + jnp.sum(onehot, axis=1, keepdims=True)
    carry[...] = jnp.broadcast_to(total, carry.shape)
    cnt_ref[...] = jnp.broadcast_to(total, cnt_ref.shape)


def _route(x, g_ffn, w_r, b_r, count_in, tm):
    n = x.shape[0]
    const = lambda i: (0, 0)
    return pl.pallas_call(
        _route_kernel,
        grid=(n // tm,),
        in_specs=[
            pl.BlockSpec((tm, D_MODEL), lambda i: (i, 0)),
            pl.BlockSpec((1, D_MODEL), const),
            pl.BlockSpec((D_MODEL, LANES), const),
            pl.BlockSpec((1, LANES), const),
            pl.BlockSpec((N_EXPERTS, LANES), const),
        ],
        out_specs=[
            pl.BlockSpec((tm, LANES), lambda i: (i, 0)),
            pl.BlockSpec((8, tm), lambda i: (0, i)),
            pl.BlockSpec((N_EXPERTS, LANES), const),
        ],
        out_shape=[
            jax.ShapeDtypeStruct((n, LANES), F32),
            jax.ShapeDtypeStruct((8, n), F32),
            jax.ShapeDtypeStruct((N_EXPERTS, LANES), F32),
        ],
        scratch_shapes=[pltpu.VMEM((N_EXPERTS, LANES), F32)],
        compiler_params=_params("arbitrary"),
        name="route",
    )(x, g_ffn, w_r, b_r, count_in)


DMA_UNROLL = 8


def _scatter_rows(d0_ref, d1_ref, x_ref, pad_out, sem, tile):
    def row_copy(r, dst_row):
        return pltpu.make_async_copy(x_ref.at[pl.ds(r, 1), :], pad_out.at[pl.ds(dst_row, 1), :], sem)

    for r in range(tile):
        row_copy(r, d0_ref[0, 0, r]).start(priority=0)
        row_copy(r, d1_ref[0, 0, r]).start(priority=1)

    def wait(r, carry):
        row_copy(0, 0).wait()
        row_copy(0, 0).wait()
        return carry

    lax.fori_loop(0, tile, wait, 0, unroll=DMA_UNROLL)


def _dispatch_kernel(zero_ref, d0p_ref, d1p_ref, d0s_ref, d1s_ref, xp_ref, xs_ref, pad_out, zero_scr, sem, zsem,
                     *, tile, n_tiles):
    i = pl.program_id(0)

    @pl.when(i == 0)
    def _():
        zero_scr[...] = jnp.zeros_like(zero_scr)

        def zero_copy(e):
            row0 = pl.multiple_of(zero_ref[e], EXPERT_ROWS)
            return pltpu.make_async_copy(zero_scr, pad_out.at[pl.ds(row0, EXPERT_ROWS), :], zsem)

        for e in range(zero_ref.shape[0]):
            @pl.when(zero_ref[e] >= 0)
            def _():
                zero_copy(e).start()
        for e in range(zero_ref.shape[0]):
            @pl.when(zero_ref[e] >= 0)
            def _():
                zero_copy(e).wait()

    @pl.when(i < n_tiles)
    def _():
        _scatter_rows(d0p_ref, d1p_ref, xp_ref, pad_out, sem, tile)

    @pl.when(i == n_tiles)
    def _():
        _scatter_rows(d0s_ref, d1s_ref, xs_ref, pad_out, sem, xs_ref.shape[0])


def _dispatch(zero_rows, dest_p, dest_s, xp, xs, n_rows, tile):
    n_tiles = xp.shape[0] // tile
    n_s = xs.shape[0]
    last = n_tiles - 1
    tile_spec = pl.BlockSpec((1, 1, tile), lambda i: (jnp.minimum(i, last), 0, 0), memory_space=pltpu.SMEM)
    return pl.pallas_call(
        functools.partial(_dispatch_kernel, tile=tile, n_tiles=n_tiles),
        grid=(n_tiles + 1,),
        in_specs=[
            pl.BlockSpec(memory_space=pltpu.SMEM),
            tile_spec, tile_spec,
            pl.BlockSpec(memory_space=pltpu.SMEM),
            pl.BlockSpec(memory_space=pltpu.SMEM),
            pl.BlockSpec((tile, D_MODEL), lambda i: (jnp.minimum(i, last), 0)),
            pl.BlockSpec((n_s, D_MODEL), lambda i: (0, 0)),
        ],
        out_specs=pl.BlockSpec(memory_space=pl.ANY),
        out_shape=jax.ShapeDtypeStruct((n_rows, D_MODEL), F32),
        scratch_shapes=[pltpu.VMEM((EXPERT_ROWS, D_MODEL), F32), pltpu.SemaphoreType.DMA(()),
                        pltpu.SemaphoreType.DMA(())],
        compiler_params=_params("arbitrary"),
        name="dispatch",
    )(zero_rows, dest_p[0].reshape(n_tiles, 1, tile), dest_p[1].reshape(n_tiles, 1, tile),
      dest_s[0].reshape(1, 1, n_s), dest_s[1].reshape(1, 1, n_s), xp, xs)


def _expert_kernel(be_ref, nused_ref, x_ref, g_ref, wg_ref, wu_ref, wd_ref, y_ref, wg_s, wu_s, wd_s):
    i = pl.program_id(0)
    prev = be_ref[jnp.maximum(i - 1, 0)]
    active = i < nused_ref[0]

    @pl.when(jnp.logical_and(active, jnp.logical_or(i == 0, be_ref[i] != prev)))
    def _():
        wg_s[...] = wg_ref[...].astype(BF16)
        wu_s[...] = wu_ref[...].astype(BF16)
        wd_s[...] = wd_ref[...].astype(BF16)

    @pl.when(active)
    def _():
        hn = _rms(x_ref[...], g_ref[...]).astype(BF16)
        gate = jnp.dot(hn, wg_s[...], preferred_element_type=F32)
        up = jnp.dot(hn, wu_s[...], preferred_element_type=F32)
        act = (gate * jax.nn.sigmoid(gate)) * up
        y_ref[...] = jnp.dot(act.astype(BF16), wd_s[...], preferred_element_type=F32)


def _experts(block_expert, n_used, x_pad, g_ffn, w_gate, w_up, w_down):
    n_rows = x_pad.shape[0]
    nb = n_rows // EXPERT_ROWS
    used_block = lambda i, be, nu: (jnp.minimum(i, nu[0] - 1), 0)
    grid_spec = pltpu.PrefetchScalarGridSpec(
        num_scalar_prefetch=2,
        grid=(nb,),
        in_specs=[
            pl.BlockSpec((EXPERT_ROWS, D_MODEL), used_block),
            pl.BlockSpec((1, D_MODEL), lambda i, be, nu: (0, 0)),
            pl.BlockSpec((None, None, D_MODEL, D_EXPERT), lambda i, be, nu: (0, be[i], 0, 0)),
            pl.BlockSpec((None, None, D_MODEL, D_EXPERT), lambda i, be, nu: (0, be[i], 0, 0)),
            pl.BlockSpec((None, None, D_EXPERT, D_MODEL), lambda i, be, nu: (0, be[i], 0, 0)),
        ],
        out_specs=pl.BlockSpec((EXPERT_ROWS, D_MODEL), used_block),
        scratch_shapes=[
            pltpu.VMEM((D_MODEL, D_EXPERT), BF16),
            pltpu.VMEM((D_MODEL, D_EXPERT), BF16),
            pltpu.VMEM((D_EXPERT, D_MODEL), BF16),
        ],
    )
    return pl.pallas_call(
        _expert_kernel,
        grid_spec=grid_spec,
        out_shape=jax.ShapeDtypeStruct((n_rows, D_MODEL), F32),
        input_output_aliases={2: 0},
        compiler_params=_params("arbitrary"),
        name="experts",
    )(block_expert, n_used, x_pad, g_ffn, w_gate, w_up, w_down)


def _final_kernel(d0_ref, d1_ref, d0n_ref, d1n_ref, x_ref, info_ref, p_ref, y_hbm, wpg_ref, wpp_ref, gple_ref,
                  gfin_ref, out_ref, ybuf, sem):
    tm = x_ref.shape[0]
    i = pl.program_id(0)
    buf = i % 2

    def row_copy(src_row, b, k, r):
        return pltpu.make_async_copy(y_hbm.at[pl.ds(src_row, 1), :], ybuf.at[b, k, pl.ds(r, 1), :], sem.at[b])

    def gather(da_ref, db_ref, b):
        def start(r, carry):
            row_copy(da_ref[0, 0, r], b, 0, r).start(priority=0)
            row_copy(db_ref[0, 0, r], b, 1, r).start(priority=1)
            return carry
        lax.fori_loop(0, tm, start, 0, unroll=DMA_UNROLL)

    def wait_all(b):
        def wait(r, carry):
            row_copy(0, b, 0, 0).wait()
            row_copy(0, b, 1, 0).wait()
            return carry
        lax.fori_loop(0, tm, wait, 0, unroll=DMA_UNROLL)

    @pl.when(i == 0)
    def _():
        gather(d0_ref, d1_ref, 0)

    wait_all(buf)
    for r in range(tm):
        row_copy(d0n_ref[0, 0, r], 1 - buf, 0, r).start()
        row_copy(d1n_ref[0, 0, r], 1 - buf, 1, r).start()

    info = info_ref[...]
    x2 = x_ref[...] + (ybuf[buf, 0] * info[:, 2:3] + ybuf[buf, 1] * info[:, 3:4])
    gate = jax.nn.sigmoid(jnp.dot(_rms(x2, gple_ref[...]).astype(BF16), wpg_ref[...], preferred_element_type=F32))
    proj = jnp.dot(p_ref[...].astype(BF16), wpp_ref[...], preferred_element_type=F32)
    out_ref[...] = _rms(x2 + gate * proj, gfin_ref[...])

    @pl.when(i == pl.num_programs(0) - 1)
    def _():
        wait_all(1 - buf)


def _final(dest, x1, info, p, y_pad, w_pg, w_pp, g_ple, g_final, tm):
    n = x1.shape[0]
    d0 = dest[0].reshape(n // tm, 1, tm)
    d1 = dest[1].reshape(n // tm, 1, tm)
    const = lambda i: (0, 0)
    last = n // tm - 1
    cur = pl.BlockSpec((1, 1, tm), lambda i: (i, 0, 0), memory_space=pltpu.SMEM)
    nxt = pl.BlockSpec((1, 1, tm), lambda i: (jnp.minimum(i + 1, last), 0, 0), memory_space=pltpu.SMEM)
    return pl.pallas_call(
        _final_kernel,
        grid=(n // tm,),
        in_specs=[
            cur, cur, nxt, nxt,
            pl.BlockSpec((tm, D_MODEL), lambda i: (i, 0)),
            pl.BlockSpec((tm, LANES), lambda i: (i, 0)),
            pl.BlockSpec((tm, PLE_DIM), lambda i: (i, 0)),
            pl.BlockSpec(memory_space=pl.ANY),
            pl.BlockSpec((D_MODEL, D_MODEL), const),
            pl.BlockSpec((PLE_DIM, D_MODEL), const),
            pl.BlockSpec((1, D_MODEL), const),
            pl.BlockSpec((1, D_MODEL), const),
        ],
        out_specs=pl.BlockSpec((tm, D_MODEL), lambda i: (i, 0)),
        out_shape=jax.ShapeDtypeStruct((n, D_MODEL), F32),
        scratch_shapes=[pltpu.VMEM((2, 2, tm, D_MODEL), F32), pltpu.SemaphoreType.DMA((2,))],
        compiler_params=_params("arbitrary"),
        name="final",
    )(d0, d1, d0, d1, x1, info, p, y_pad, w_pg, w_pp, g_ple, g_final)


def _pad_lanes(a):
    return jnp.pad(a, ((0, 0), (0, LANES - a.shape[1])))


def kernel(x_prompt, x_sample, p_prompt, p_sample, state_mlstm_C, state_mlstm_n, state_mlstm_m, state_swa_k, state_swa_v, g_mix, w_in, b_igate, b_fgate, g_mh, attn_sinks, w_out, g_ffn, w_router_g, b_router_g, w_router_e, b_router_e, w_gate, w_up, w_down, g_ple, w_ple_gate, w_ple_proj, g_final):
    batch, seq, d = x_prompt.shape
    dec = x_sample.shape[0]
    n_p = batch * seq
    xp = x_prompt.reshape(n_p, d)
    xs = x_sample.reshape(dec, d)

    w_in0 = w_in[0]
    head_order = jnp.array(SWA_HEAD_ORDER, jnp.int32)
    w_aq = w_in0[:, ATT_OFF:ATT_OFF + A_Q_W].reshape(d, A_HEADS, A_HEAD_DIM)[:, head_order].reshape(d, A_Q_W)
    zf_pad = ZF_W - (ZF_GATE + LANES)
    w_cat = jnp.concatenate(
        [w_in0[:, :3 * M_W], w_aq,
         w_in0[:, 3 * M_W:GATE_OFF], w_in0[:, ATT_OFF + A_Q_W:],
         _pad_lanes(w_in0[:, GATE_OFF:ATT_OFF]), jnp.zeros((d, zf_pad), F32)],
        axis=1).astype(BF16)
    gate_bias8 = jnp.concatenate([b_igate, b_fgate], axis=1)
    gate_bias = _pad_lanes(gate_bias8)
    w_oa = w_out[0, M_W:].reshape(A_HEADS, A_HEAD_DIM, d)[head_order].reshape(A_Q_W, d)
    w_o = jnp.concatenate([w_out[0, :M_W], w_oa], axis=0).astype(BF16)
    kv_of_head = jnp.arange(A_HEADS, dtype=jnp.int32) // A_GROUP
    head_mask = (kv_of_head[:, None] == jnp.arange(A_KV_HEADS, dtype=jnp.int32)[None, :]).astype(F32)
    gap = EXPERT_LANE0 - N_GROUPS
    w_r = _pad_lanes(jnp.concatenate([w_router_g[0], jnp.zeros((d, gap), F32), w_router_e[0]], axis=1))
    b_r = _pad_lanes(jnp.concatenate([b_router_g, jnp.zeros((1, gap), F32), b_router_e], axis=1))
    w_pg = w_ple_gate[0].astype(BF16)
    w_pp = w_ple_proj[0].astype(BF16)
    sink_col = attn_sinks.reshape(A_HEADS, 1)

    zb, zf = _inproj_split(xp, g_mix, w_cat, tm=INPROJ_ROWS, tn=INPROJ_COLS)
    om_p, c_p, n_p8, m_p8 = _mlstm_prompt(zb, zf, gate_bias, g_mh, batch, seq)
    oa_p = _swa_prompt(zb, zf, attn_sinks, batch, seq)
    x1p = _outproj(xp, om_p, oa_p, w_o, tm=OUTPROJ_ROWS)

    zs = _inproj(xs, g_mix, w_in0, tm=dec, tn=512, precision=HI)
    om_s, c_s, n_s, m_s = _mlstm_sample(
        zs.reshape(dec, 1, zs.shape[1]), gate_bias8,
        state_mlstm_C[0], state_mlstm_n[0].reshape(dec, 1, M_W), state_mlstm_m[0].reshape(dec, 1, M_HEADS), g_mh)
    ks_off = ATT_OFF + A_Q_W
    vs_off = ks_off + A_KV_W
    q_heads = zs[:, ATT_OFF:ks_off].reshape(dec, A_HEADS, 1, A_HEAD_DIM)
    qx = (q_heads * head_mask[None, :, :, None]).reshape(dec, A_HEADS, A_KV_W)
    ox_s, k_s, v_s = _swa_sample(
        qx,
        zs[:, ks_off:vs_off].reshape(dec, 1, A_KV_W),
        zs[:, vs_off:].reshape(dec, 1, A_KV_W),
        state_swa_k[0].reshape(dec, WINDOW, A_KV_W), state_swa_v[0].reshape(dec, WINDOW, A_KV_W), sink_col)
    oa_s = jnp.sum(ox_s.reshape(dec, A_HEADS, A_KV_HEADS, A_HEAD_DIM) * head_mask[None, :, :, None], axis=2)
    x1s = _outproj(xs, om_s.reshape(dec, M_W), oa_s.reshape(dec, A_Q_W), w_out[0], tm=dec, precision=HI)

    info_p, rows_p, cnt_p = _route(x1p, g_ffn, w_r, b_r, jnp.zeros((N_EXPERTS, LANES), F32), tm=ROUTE_ROWS)
    info_s, rows_s, cnt = _route(x1s, g_ffn, w_r, b_r, cnt_p, tm=dec)
    counts = cnt[:, 0].astype(jnp.int32)
    padded = ((counts + EXPERT_ROWS - 1) // EXPERT_ROWS) * EXPERT_ROWS
    pad_end = jnp.cumsum(padded)
    pad_start = pad_end - padded
    n_assign = 2 * (n_p + dec)
    n_blocks = -(-n_assign // EXPERT_ROWS) + N_EXPERTS
    block_row0 = jnp.arange(n_blocks, dtype=jnp.int32) * EXPERT_ROWS
    block_expert = jnp.minimum(
        jnp.sum((pad_end[None, :] <= block_row0[:, None]).astype(jnp.int32), axis=1), N_EXPERTS - 1)
    n_used = (pad_end[-1:] // EXPERT_ROWS).astype(jnp.int32)
    spare = n_used + jnp.arange(N_EXPERTS, dtype=jnp.int32)
    zero_rows = jnp.concatenate([
        jnp.where(padded > 0, pad_end - EXPERT_ROWS, -1),
        jnp.where(spare < n_blocks, spare * EXPERT_ROWS, -1)]).astype(jnp.int32)

    def dest_of(rows):
        e = rows[0:2].astype(jnp.int32)
        hit = e[:, :, None] == jnp.arange(N_EXPERTS, dtype=jnp.int32)
        return jnp.sum(jnp.where(hit, pad_start, 0), axis=2) + rows[4:6].astype(jnp.int32)

    dest_p = dest_of(rows_p)
    dest_s = dest_of(rows_s)

    x_pad = _dispatch(zero_rows, dest_p, dest_s, x1p, x1s, n_blocks * EXPERT_ROWS, tile=DISPATCH_ROWS)
    y_pad = _experts(block_expert, n_used, x_pad, g_ffn, w_gate, w_up, w_down)
    y_p = _final(dest_p, x1p, info_p, p_prompt[0].reshape(n_p, PLE_DIM), y_pad, w_pg, w_pp, g_ple, g_final.reshape(1, d), tm=FINAL_ROWS)
    y_s = _final(dest_s, x1s, info_s, p_sample[0].reshape(dec, PLE_DIM), y_pad, w_pg, w_pp, g_ple, g_final.reshape(1, d), tm=dec)

    zf3 = zf.reshape(batch, seq, ZF_W)
    swa_k_p = zf3[:, seq - WINDOW:, ZF_K:ZF_V].reshape(1, batch, WINDOW, A_KV_HEADS, A_HEAD_DIM)
    swa_v_p = zf3[:, seq - WINDOW:, ZF_V:ZF_GATE].reshape(1, batch, WINDOW, A_KV_HEADS, A_HEAD_DIM)
    return (
        y_p.reshape(batch, seq, d),
        y_s.reshape(dec, 1, d),
        c_p[None],
        n_p8[None, :, :M_HEADS, :],
        m_p8[None, :, :M_HEADS, 0],
        swa_k_p,
        swa_v_p,
        c_s[None],
        n_s.reshape(1, dec, M_HEADS, M_DK),
        m_s[None, :, 0, :M_HEADS],
        k_s.reshape(1, dec, WINDOW, A_KV_HEADS, A_HEAD_DIM),
        v_s.reshape(1, dec, WINDOW, A_KV_HEADS, A_HEAD_DIM),
    )
```

```python
import functools

import jax
import jax.numpy as jnp
from jax import lax
from jax.experimental import pallas as pl
from jax.experimental.pallas import tpu as pltpu

F32 = jnp.float32
BF16 = jnp.bfloat16
HI = lax.Precision.HIGHEST

D_MODEL = 2048
M_HEADS = 4
M_DK = 256
M_DV = 256
A_HEADS = 16
A_KV_HEADS = 4
A_HEAD_DIM = 64
A_GROUP = A_HEADS // A_KV_HEADS
WINDOW = 128
M_W = M_HEADS * M_DK
A_Q_W = A_HEADS * A_HEAD_DIM
A_KV_W = A_KV_HEADS * A_HEAD_DIM
GATE_OFF = 4 * M_W
ATT_OFF = GATE_OFF + 2 * M_HEADS
ZB_W = 3 * M_W + A_Q_W
ZF_MO, ZF_K, ZF_V, ZF_GATE = 0, M_W, M_W + A_KV_W, M_W + 2 * A_KV_W
ZF_W = 2 * M_W
N_GROUPS = 4
EXPERTS_PER_GROUP = 8
N_EXPERTS = N_GROUPS * EXPERTS_PER_GROUP
D_EXPERT = 512
PLE_DIM = 256
EPS = 1e-6

LANES = 128
SUBLANES = 8
INPROJ_ROWS, INPROJ_COLS = 1024, 1024
OUTPROJ_ROWS = 512
ROUTE_ROWS = 512
DISPATCH_ROWS = 512
FINAL_ROWS = 256
MLSTM_CHUNK = 256
EXPERT_ROWS = 256
VMEM_LIMIT = 56 * 1024 * 1024
NT_DIMS = (((1,), (1,)), ((), ()))
TN_DIMS = (((0,), (0,)), ((), ()))


def _params(*sem):
    return pltpu.CompilerParams(dimension_semantics=sem, vmem_limit_bytes=VMEM_LIMIT)


def _rms(x, g):
    r = lax.rsqrt(jnp.mean(x * x, axis=-1, keepdims=True) + EPS)
    return (x * r) * g


def _log_sigmoid(x):
    return jnp.minimum(x, 0.0) - jnp.log1p(jnp.exp(-jnp.abs(x)))


def _inproj_kernel(x_ref, g_ref, w_ref, z_ref, hn_scr, *, precision):
    @pl.when(pl.program_id(1) == 0)
    def _():
        hn_scr[...] = _rms(x_ref[...], g_ref[...]).astype(hn_scr.dtype)

    z_ref[...] = jnp.dot(hn_scr[...], w_ref[...], precision=precision, preferred_element_type=F32)


def _inproj(x, g_mix, w, tm, tn, precision=None):
    n = x.shape[0]
    width = w.shape[1]
    return pl.pallas_call(
        functools.partial(_inproj_kernel, precision=precision),
        grid=(n // tm, pl.cdiv(width, tn)),
        in_specs=[
            pl.BlockSpec((tm, D_MODEL), lambda i, j: (i, 0)),
            pl.BlockSpec((1, D_MODEL), lambda i, j: (0, 0)),
            pl.BlockSpec((D_MODEL, tn), lambda i, j: (0, j)),
        ],
        out_specs=pl.BlockSpec((tm, tn), lambda i, j: (i, j)),
        out_shape=jax.ShapeDtypeStruct((n, width), F32),
        scratch_shapes=[pltpu.VMEM((tm, D_MODEL), w.dtype)],
        compiler_params=_params("parallel", "arbitrary"),
        name="inproj",
    )(x, g_mix, w)


def _inproj_split_kernel(x_ref, g_ref, w_ref, zb_ref, zf_ref, hn_scr, *, n_bf16_blocks):
    j = pl.program_id(1)

    @pl.when(j == 0)
    def _():
        hn_scr[...] = _rms(x_ref[...], g_ref[...]).astype(BF16)

    acc = jnp.dot(hn_scr[...], w_ref[...], preferred_element_type=F32)

    @pl.when(j < n_bf16_blocks)
    def _():
        zb_ref[...] = acc.astype(BF16)

    @pl.when(j >= n_bf16_blocks)
    def _():
        zf_ref[...] = acc


def _inproj_split(x, g_mix, w, tm, tn):
    n = x.shape[0]
    nb = ZB_W // tn
    nf = ZF_W // tn
    return pl.pallas_call(
        functools.partial(_inproj_split_kernel, n_bf16_blocks=nb),
        grid=(n // tm, nb + nf),
        in_specs=[
            pl.BlockSpec((tm, D_MODEL), lambda i, j: (i, 0)),
            pl.BlockSpec((1, D_MODEL), lambda i, j: (0, 0)),
            pl.BlockSpec((D_MODEL, tn), lambda i, j: (0, j)),
        ],
        out_specs=[
            pl.BlockSpec((tm, tn), lambda i, j: (i, jnp.minimum(j, nb - 1))),
            pl.BlockSpec((tm, tn), lambda i, j: (i, jnp.maximum(j - nb, 0))),
        ],
        out_shape=[
            jax.ShapeDtypeStruct((n, ZB_W), BF16),
            jax.ShapeDtypeStruct((n, ZF_W), F32),
        ],
        scratch_shapes=[pltpu.VMEM((tm, D_MODEL), BF16)],
        compiler_params=_params("parallel", "arbitrary"),
        name="inproj_prompt",
    )(x, g_mix, w)


def _mlstm_out(h, gmh, mo):
    hn = h * lax.rsqrt(jnp.mean(h * h, axis=-1, keepdims=True) + EPS)
    return (hn * gmh) * jax.nn.sigmoid(mo)


def _gates(g_raw, bias):
    g = g_raw + bias
    lane = lax.broadcasted_iota(jnp.int32, g.shape, g.ndim - 1)
    return jnp.where(lane < M_HEADS, g, _log_sigmoid(g))


def _mlstm_kernel(q_ref, k_ref, v_ref, mo_ref, g_ref, bias_ref, gmh_ref,
                  o_ref, c_out, n_out, m_out, c_scr, n_scr, m_scr, *, chunk, n_chunks):
    c_idx = pl.program_id(1)

    @pl.when(c_idx == 0)
    def _():
        c_scr[...] = jnp.zeros_like(c_scr)
        n_scr[...] = jnp.zeros_like(n_scr)
        m_scr[...] = jnp.zeros_like(m_scr)

    L = chunk
    row = lax.broadcasted_iota(jnp.int32, (L, L), 0)
    col = lax.broadcasted_iota(jnp.int32, (L, L), 1)
    causal = row >= col
    ltri = causal.astype(F32)
    utri = (row <= col).astype(F32)
    g_c = _gates(g_ref[...], bias_ref[...])
    f_c = jnp.dot(ltri, g_c, precision=HI, preferred_element_type=F32)
    g_r = g_c.T[0:2 * M_HEADS, :]
    f_r = jnp.dot(g_r, utri, precision=HI, preferred_element_type=F32)

    for h in range(M_HEADS):
        hs = slice(h * M_DK, (h + 1) * M_DK)
        hf = M_HEADS + h
        a_row = g_r[h:h + 1, :] - f_r[hf:hf + 1, :]
        f_col = f_c[:, hf:hf + 1]
        a_col = g_c[:, h:h + 1] - f_col
        m0 = m_scr[h:h + 1, 0:1]
        a_max = jnp.max(jnp.where(causal, a_row, -jnp.inf), axis=1, keepdims=True)
        m_t = f_col + jnp.maximum(m0, a_max)
        dec0 = jnp.exp(f_col + m0 - m_t)
        d = jnp.exp(jnp.where(causal, (f_col - m_t) + a_row, -jnp.inf))
        qb = q_ref[:, hs]
        kb = k_ref[:, hs] * (M_DK ** -0.5)
        vb = v_ref[:, hs]
        q = qb.astype(F32)
        k = kb.astype(F32)
        w = lax.dot_general(qb, kb, NT_DIMS, preferred_element_type=F32) * d
        c0 = c_scr[h]
        n0 = n_scr[h:h + 1, :]
        num = jnp.dot(w.astype(BF16), vb, preferred_element_type=F32) + dec0 * lax.dot_general(
            qb, c0.astype(BF16), NT_DIMS, preferred_element_type=F32)
        den = jnp.sum(w, axis=1, keepdims=True) + dec0 * jnp.sum(q * n0, axis=1, keepdims=True)
        hh = num / jnp.maximum(jnp.abs(den), jnp.exp(-m_t))
        m_end = m_t[L - 1:L, :]
        f_end = f_col[L - 1:L, :]
        w_end = jnp.exp(f_end + a_col - m_end)
        dec_end = jnp.exp(f_end + m0 - m_end)
        kw = k * w_end
        c_scr[h] = dec_end * c0 + lax.dot_general(vb, kw.astype(BF16), TN_DIMS, preferred_element_type=F32)
        n_scr[h:h + 1, :] = dec_end * n0 + jnp.sum(kw, axis=0, keepdims=True)
        m_scr[h:h + 1, :] = jnp.broadcast_to(m_end, (1, LANES))
        o_ref[:, hs] = _mlstm_out(hh, gmh_ref[:, hs], mo_ref[:, hs]).astype(o_ref.dtype)

    @pl.when(c_idx == n_chunks - 1)
    def _():
        c_out[0] = c_scr[...]
        n_out[0] = n_scr[...]
        m_out[0] = m_scr[...]


def _mlstm_prompt(zb, zf, gate_bias, g_mh, batch, seq):
    L = MLSTM_CHUNK
    nc = seq // L
    rows = lambda b, c: b * nc + c
    gate_blk = ZF_GATE // LANES
    return pl.pallas_call(
        functools.partial(_mlstm_kernel, chunk=L, n_chunks=nc),
        grid=(batch, nc),
        in_specs=[
            pl.BlockSpec((L, M_W), lambda b, c: (rows(b, c), 0)),
            pl.BlockSpec((L, M_W), lambda b, c: (rows(b, c), 1)),
            pl.BlockSpec((L, M_W), lambda b, c: (rows(b, c), 2)),
            pl.BlockSpec((L, M_W), lambda b, c: (rows(b, c), ZF_MO // M_W)),
            pl.BlockSpec((L, LANES), lambda b, c: (rows(b, c), gate_blk)),
            pl.BlockSpec((1, LANES), lambda b, c: (0, 0)),
            pl.BlockSpec((1, M_W), lambda b, c: (0, 0)),
        ],
        out_specs=[
            pl.BlockSpec((L, M_W), lambda b, c: (rows(b, c), 0)),
            pl.BlockSpec((1, M_HEADS, M_DV, M_DK), lambda b, c: (b, 0, 0, 0)),
            pl.BlockSpec((1, 8, M_DK), lambda b, c: (b, 0, 0)),
            pl.BlockSpec((1, 8, LANES), lambda b, c: (b, 0, 0)),
        ],
        out_shape=[
            jax.ShapeDtypeStruct((batch * seq, M_W), BF16),
            jax.ShapeDtypeStruct((batch, M_HEADS, M_DV, M_DK), F32),
            jax.ShapeDtypeStruct((batch, 8, M_DK), F32),
            jax.ShapeDtypeStruct((batch, 8, LANES), F32),
        ],
        scratch_shapes=[
            pltpu.VMEM((M_HEADS, M_DV, M_DK), F32),
            pltpu.VMEM((8, M_DK), F32),
            pltpu.VMEM((8, LANES), F32),
        ],
        compiler_params=_params("parallel", "arbitrary"),
        name="mlstm_prompt",
    )(zb, zb, zb, zf, zf, gate_bias, g_mh)


MLSTM_STEP_TOKENS = 4


def _mlstm_step_kernel(z_ref, bias_ref, c_ref, n_ref, m_ref, gmh_ref, o_ref, c_out, n_out, m_out):
    row = lax.broadcasted_iota(jnp.int32, (M_DV, M_DV), 0)
    col = lax.broadcasted_iota(jnp.int32, (M_DV, M_DV), 1)
    eye = row == col
    for t in range(z_ref.shape[0]):
        _mlstm_step_token(t, eye, z_ref, bias_ref, c_ref, n_ref, m_ref, gmh_ref, o_ref, c_out, n_out, m_out)


def _mlstm_step_token(t, eye, z_ref, bias_ref, c_ref, n_ref, m_ref, gmh_ref, o_ref, c_out, n_out, m_out):
    z = z_ref[t]
    gates = _gates(z[:, GATE_OFF:ATT_OFF], bias_ref[...])
    m_in = m_ref[t]
    lane = lax.broadcasted_iota(jnp.int32, (1, LANES), 1)
    m_new = jnp.zeros((1, LANES), F32)
    for h in range(M_HEADS):
        hs = slice(h * M_DK, (h + 1) * M_DK)
        q = z[:, h * M_DK:(h + 1) * M_DK]
        k = z[:, M_W + h * M_DK:M_W + (h + 1) * M_DK] * (M_DK ** -0.5)
        v = z[:, 2 * M_W + h * M_DV:2 * M_W + (h + 1) * M_DV]
        mo = z[:, 3 * M_W + h * M_DV:3 * M_W + (h + 1) * M_DV]
        ip = gates[:, h:h + 1]
        lf = gates[:, M_HEADS + h:M_HEADS + h + 1]
        m0 = m_in[:, h:h + 1]
        m_t = jnp.maximum(lf + m0, ip)
        dec0 = jnp.exp(lf + m0 - m_t)
        d = jnp.exp(ip - m_t)
        c0 = c_ref[t, h]
        n0 = n_ref[t, :, hs]
        w = jnp.sum(q * k, axis=1, keepdims=True) * d
        cq_col = jnp.sum(c0 * q, axis=1, keepdims=True)
        cq = jnp.sum(jnp.where(eye, cq_col, 0.0), axis=0, keepdims=True)
        num = w * v + dec0 * cq
        den = w + dec0 * jnp.sum(n0 * q, axis=1, keepdims=True)
        hh = num / jnp.maximum(jnp.abs(den), jnp.exp(-m_t))
        v_col = jnp.sum(jnp.where(eye, v, 0.0), axis=1, keepdims=True)
        c_out[t, h] = dec0 * c0 + (d * v_col) * k
        n_out[t, :, hs] = dec0 * n0 + d * k
        m_new = jnp.where(lane == h, m_t, m_new)
        o_ref[t, :, hs] = _mlstm_out(hh, gmh_ref[:, hs], mo).astype(o_ref.dtype)
    m_out[t] = m_new


def _mlstm_sample(z3, gate_bias, c0, n0, m0, g_mh):
    nb = z3.shape[0]
    nt = MLSTM_STEP_TOKENS
    tok = lambda b: (b, 0, 0)
    return pl.pallas_call(
        _mlstm_step_kernel,
        grid=(nb // nt,),
        in_specs=[
            pl.BlockSpec((nt, 1, z3.shape[2]), tok),
            pl.BlockSpec((1, 2 * M_HEADS), lambda b: (0, 0)),
            pl.BlockSpec((nt, M_HEADS, M_DV, M_DK), lambda b: (b, 0, 0, 0)),
            pl.BlockSpec((nt, 1, M_W), tok),
            pl.BlockSpec((nt, 1, M_HEADS), tok),
            pl.BlockSpec((1, M_W), lambda b: (0, 0)),
        ],
        out_specs=[
            pl.BlockSpec((nt, 1, M_W), tok),
            pl.BlockSpec((nt, M_HEADS, M_DV, M_DK), lambda b: (b, 0, 0, 0)),
            pl.BlockSpec((nt, 1, M_W), tok),
            pl.BlockSpec((nt, 1, LANES), tok),
        ],
        out_shape=[
            jax.ShapeDtypeStruct((nb, 1, M_W), F32),
            jax.ShapeDtypeStruct((nb, M_HEADS, M_DV, M_DK), F32),
            jax.ShapeDtypeStruct((nb, 1, M_W), F32),
            jax.ShapeDtypeStruct((nb, 1, LANES), F32),
        ],
        compiler_params=_params("parallel"),
        name="mlstm_sample",
    )(z3, gate_bias, c0, n0, m0, g_mh)


_EVEN_KV = [h for h in range(A_HEADS) if (h // A_GROUP) % 2 == 0]
_ODD_KV = [h for h in range(A_HEADS) if (h // A_GROUP) % 2 == 1]
SWA_HEAD_ORDER = tuple(h for pair in zip(_EVEN_KV, _ODD_KV) for h in pair)
HALF = LANES // 2
KV_SPAN = 2 * WINDOW


def _swa_kernel(sink_ref, q_ref, kc_ref, kp_ref, vc_ref, vp_ref, o_ref, qx_scr, p_scr):
    j = pl.program_id(1)
    lo_half = lax.broadcasted_iota(jnp.int32, (WINDOW, LANES), 1) < HALF
    zero_tile = jnp.zeros((WINDOW, LANES), BF16)
    for slot, head in enumerate(SWA_HEAD_ORDER):
        kv = head // A_GROUP
        tile = q_ref[:, (slot // 2) * LANES:(slot // 2 + 1) * LANES]
        keep = lo_half if slot % 2 == 0 else jnp.logical_not(lo_half)
        qm = jnp.where(keep, tile * (A_HEAD_DIM ** -0.5), jnp.zeros_like(tile))
        parts = [qm, zero_tile] if kv // 2 == 0 else [zero_tile, qm]
        qx_scr[slot * WINDOW:(slot + 1) * WINDOW, :] = jnp.concatenate(parts, axis=1)

    k2 = jnp.concatenate([kp_ref[...], kc_ref[...]], axis=0).astype(BF16)
    v2 = jnp.concatenate([vp_ref[...], vc_ref[...]], axis=0).astype(BF16)
    s_all = lax.dot_general(qx_scr[...], k2, NT_DIMS, preferred_element_type=F32)

    t = lax.broadcasted_iota(jnp.int32, (WINDOW, KV_SPAN), 0)
    kk = lax.broadcasted_iota(jnp.int32, (WINDOW, KV_SPAN), 1)
    in_prev = jnp.logical_and(jnp.logical_and(kk < WINDOW, kk > t), j > 0)
    in_cur = jnp.logical_and(kk >= WINDOW, kk - WINDOW <= t)
    allowed = jnp.logical_or(in_prev, in_cur)
    for slot, head in enumerate(SWA_HEAD_ORDER):
        rows = slice(slot * WINDOW, (slot + 1) * WINDOW)
        s = jnp.where(allowed, s_all[rows, :], -jnp.inf)
        sink = sink_ref[0, head]
        mx = jnp.maximum(jnp.max(s, axis=1, keepdims=True), sink)
        e = jnp.exp(s - mx)
        den = jnp.sum(e, axis=1, keepdims=True) + jnp.exp(sink - mx)
        p_scr[rows, :] = (e * (1.0 / den)).astype(BF16)

    o_all = jnp.dot(p_scr[...], v2, preferred_element_type=F32)
    for c in range(A_HEADS // 2):
        tiles = []
        for slot in (2 * c, 2 * c + 1):
            kv = SWA_HEAD_ORDER[slot] // A_GROUP
            tiles.append(o_all[slot * WINDOW:(slot + 1) * WINDOW, (kv // 2) * LANES:(kv // 2 + 1) * LANES])
        o_ref[:, c * LANES:(c + 1) * LANES] = jnp.where(lo_half, tiles[0], tiles[1]).astype(o_ref.dtype)


def _swa_prompt(zb, zf, sinks, batch, seq):
    nb = seq // WINDOW
    q_blk = (3 * M_W) // A_Q_W
    k_blk = ZF_K // A_KV_W
    cur = lambda b, j: b * nb + j
    prev = lambda b, j: b * nb + jnp.maximum(j - 1, 0)
    return pl.pallas_call(
        _swa_kernel,
        grid=(batch, nb),
        in_specs=[
            pl.BlockSpec(memory_space=pltpu.SMEM),
            pl.BlockSpec((WINDOW, A_Q_W), lambda b, j: (cur(b, j), q_blk)),
            pl.BlockSpec((WINDOW, A_KV_W), lambda b, j: (cur(b, j), k_blk)),
            pl.BlockSpec((WINDOW, A_KV_W), lambda b, j: (prev(b, j), k_blk)),
            pl.BlockSpec((WINDOW, A_KV_W), lambda b, j: (cur(b, j), k_blk + 1)),
            pl.BlockSpec((WINDOW, A_KV_W), lambda b, j: (prev(b, j), k_blk + 1)),
        ],
        out_specs=pl.BlockSpec((WINDOW, A_Q_W), lambda b, j: (cur(b, j), 0)),
        out_shape=jax.ShapeDtypeStruct((batch * seq, A_Q_W), BF16),
        scratch_shapes=[pltpu.VMEM((A_HEADS * WINDOW, A_KV_W), BF16), pltpu.VMEM((A_HEADS * WINDOW, KV_SPAN), BF16)],
        compiler_params=_params("parallel", "arbitrary"),
        name="swa_prompt",
    )(sinks, zb, zf, zf, zf, zf)


SWA_STEP_TOKENS = 8


def _swa_step_kernel(qx_ref, kn_ref, vn_ref, bk_ref, bv_ref, sink_ref, o_ref, ko_ref, vo_ref):
    in_window = lax.broadcasted_iota(jnp.int32, (1, WINDOW), 1) >= 1
    scale = A_HEAD_DIM ** -0.5
    sink = sink_ref[...]
    for t in range(qx_ref.shape[0]):
        qx = qx_ref[t]
        bk = bk_ref[t]
        bv = bv_ref[t]
        kn = kn_ref[t]
        vn = vn_ref[t]
        s = lax.dot_general(qx, bk, NT_DIMS, precision=HI, preferred_element_type=F32) * scale
        s = jnp.where(in_window, s, -jnp.inf)
        s_new = jnp.sum(qx * kn, axis=1, keepdims=True) * scale
        mx = jnp.maximum(jnp.maximum(jnp.max(s, axis=1, keepdims=True), s_new), sink)
        e = jnp.exp(s - mx)
        e_new = jnp.exp(s_new - mx)
        den = jnp.sum(e, axis=1, keepdims=True) + e_new + jnp.exp(sink - mx)
        o_ref[t] = jnp.dot(e / den, bv, precision=HI, preferred_element_type=F32) + (e_new / den) * vn
        ko_ref[t, 0:WINDOW - 1, :] = bk[1:WINDOW, :]
        ko_ref[t, WINDOW - 1:WINDOW, :] = kn
        vo_ref[t, 0:WINDOW - 1, :] = bv[1:WINDOW, :]
        vo_ref[t, WINDOW - 1:WINDOW, :] = vn


def _swa_sample(qx, kn3, vn3, buf_k, buf_v, sink_col):
    nb = qx.shape[0]
    nt = SWA_STEP_TOKENS
    tok = lambda b: (b, 0, 0)
    return pl.pallas_call(
        _swa_step_kernel,
        grid=(nb // nt,),
        in_specs=[
            pl.BlockSpec((nt, A_HEADS, A_KV_W), tok),
            pl.BlockSpec((nt, 1, A_KV_W), tok),
            pl.BlockSpec((nt, 1, A_KV_W), tok),
            pl.BlockSpec((nt, WINDOW, A_KV_W), tok),
            pl.BlockSpec((nt, WINDOW, A_KV_W), tok),
            pl.BlockSpec((A_HEADS, 1), lambda b: (0, 0)),
        ],
        out_specs=[
            pl.BlockSpec((nt, A_HEADS, A_KV_W), tok),
            pl.BlockSpec((nt, WINDOW, A_KV_W), tok),
            pl.BlockSpec((nt, WINDOW, A_KV_W), tok),
        ],
        out_shape=[
            jax.ShapeDtypeStruct((nb, A_HEADS, A_KV_W), F32),
            jax.ShapeDtypeStruct((nb, WINDOW, A_KV_W), F32),
            jax.ShapeDtypeStruct((nb, WINDOW, A_KV_W), F32),
        ],
        compiler_params=_params("parallel"),
        name="swa_sample",
    )(qx, kn3, vn3, buf_k, buf_v, sink_col)


GROUP_LANE0 = 0
EXPERT_LANE0 = 8


def _outproj_kernel(x_ref, om_ref, oa_ref, wm_ref, wa_ref, y_ref, *, precision):
    y_ref[...] = (x_ref[...]
                  + jnp.dot(om_ref[...], wm_ref[...], precision=precision, preferred_element_type=F32)
                  + jnp.dot(oa_ref[...], wa_ref[...], precision=precision, preferred_element_type=F32))


def _outproj(x, o_m, o_a, w, tm, precision=None):
    n = x.shape[0]
    return pl.pallas_call(
        functools.partial(_outproj_kernel, precision=precision),
        grid=(n // tm,),
        in_specs=[
            pl.BlockSpec((tm, D_MODEL), lambda i: (i, 0)),
            pl.BlockSpec((tm, M_W), lambda i: (i, 0)),
            pl.BlockSpec((tm, A_Q_W), lambda i: (i, 0)),
            pl.BlockSpec((M_W, D_MODEL), lambda i: (0, 0)),
            pl.BlockSpec((A_Q_W, D_MODEL), lambda i: (1, 0)),
        ],
        out_specs=pl.BlockSpec((tm, D_MODEL), lambda i: (i, 0)),
        out_shape=jax.ShapeDtypeStruct((n, D_MODEL), F32),
        compiler_params=_params("parallel"),
        name="outproj",
    )(x, o_m, o_a, w, w)


def _route_kernel(x_ref, g_ref, wr_ref, br_ref, cin_ref, info_ref, rows_ref, cnt_ref, carry):
    @pl.when(pl.program_id(0) == 0)
    def _():
        carry[...] = cin_ref[...]

    tm = x_ref.shape[0]
    hn = _rms(x_ref[...], g_ref[...])
    w_r = wr_ref[...]
    hn_hi = hn.astype(BF16)
    hn_lo = (hn - hn_hi.astype(F32)).astype(BF16)
    w_hi = w_r.astype(BF16)
    w_lo = (w_r - w_hi.astype(F32)).astype(BF16)
    logits = (jnp.dot(hn_hi, w_hi, preferred_element_type=F32)
              + (jnp.dot(hn_lo, w_hi, preferred_element_type=F32)
                 + jnp.dot(hn_hi, w_lo, preferred_element_type=F32))) + br_ref[...]
    lt = logits.T
    neg = -jnp.inf
    grp = lax.broadcasted_iota(jnp.int32, (SUBLANES, tm), 0)
    gl = jnp.where(grp < N_GROUPS, lt[GROUP_LANE0:GROUP_LANE0 + SUBLANES, :], neg)
    g_max = jnp.max(gl, axis=0, keepdims=True)
    g_idx = jnp.min(jnp.where(gl == g_max, grp, N_GROUPS), axis=0, keepdims=True)
    g_w = 1.0 / jnp.sum(jnp.exp(gl - g_max), axis=0, keepdims=True)
    eid = lax.broadcasted_iota(jnp.int32, (N_EXPERTS, tm), 0)
    first = g_idx * EXPERTS_PER_GROUP
    in_group = jnp.logical_and(eid >= first, eid < first + EXPERTS_PER_GROUP)
    el = jnp.where(in_group, lt[EXPERT_LANE0:EXPERT_LANE0 + N_EXPERTS, :], neg)
    l1 = jnp.max(el, axis=0, keepdims=True)
    i1 = jnp.min(jnp.where(el == l1, eid, N_EXPERTS), axis=0, keepdims=True)
    el2 = jnp.where(eid == i1, neg, el)
    l2 = jnp.max(el2, axis=0, keepdims=True)
    i2 = jnp.min(jnp.where(el2 == l2, eid, N_EXPERTS), axis=0, keepdims=True)
    e21 = jnp.exp(l2 - l1)
    w1 = g_w * (1.0 / (1.0 + e21))
    w2 = g_w * (e21 / (1.0 + e21))
    hit1 = eid == i1
    hit2 = eid == i2
    onehot = jnp.logical_or(hit1, hit2).astype(F32)
    r = lax.broadcasted_iota(jnp.int32, (tm, tm), 0)
    c = lax.broadcasted_iota(jnp.int32, (tm, tm), 1)
    before = (r < c).astype(BF16)
    cum = jnp.dot(onehot.astype(BF16), before, preferred_element_type=F32) + carry[:, 0:1]
    rank1 = jnp.sum(jnp.where(hit1, cum, 0.0), axis=0, keepdims=True)
    rank2 = jnp.sum(jnp.where(hit2, cum, 0.0), axis=0, keepdims=True)
    row = lax.broadcasted_iota(jnp.int32, (LANES, tm), 0)
    rows = jnp.where(row == 0, i1.astype(F32), 0.0)
    rows = jnp.where(row == 1, i2.astype(F32), rows)
    rows = jnp.where(row == 2, w1, rows)
    rows = jnp.where(row == 3, w2, rows)
    rows = jnp.where(row == 4, rank1, rows)
    rows = jnp.where(row == 5, rank2, rows)
    rows_ref[...] = rows[0:8, :]
    info_ref[...] = rows.T
    total = carry[:, 0:1] + jnp.sum(onehot, axis=1, keepdims=True)
    carry[...] = jnp.broadcast_to(total, carry.shape)
    cnt_ref[...] = jnp.broadcast_to(total, cnt_ref.shape)


def _route(x, g_ffn, w_r, b_r, count_in, tm):
    n = x.shape[0]
    const = lambda i: (0, 0)
    return pl.pallas_call(
        _route_kernel,
        grid=(n // tm,),
        in_specs=[
            pl.BlockSpec((tm, D_MODEL), lambda i: (i, 0)),
            pl.BlockSpec((1, D_MODEL), const),
            pl.BlockSpec((D_MODEL, LANES), const),
            pl.BlockSpec((1, LANES), const),
            pl.BlockSpec((N_EXPERTS, LANES), const),
        ],
        out_specs=[
            pl.BlockSpec((tm, LANES), lambda i: (i, 0)),
            pl.BlockSpec((8, tm), lambda i: (0, i)),
            pl.BlockSpec((N_EXPERTS, LANES), const),
        ],
        out_shape=[
            jax.ShapeDtypeStruct((n, LANES), F32),
            jax.ShapeDtypeStruct((8, n), F32),
            jax.ShapeDtypeStruct((N_EXPERTS, LANES), F32),
        ],
        scratch_shapes=[pltpu.VMEM((N_EXPERTS, LANES), F32)],
        compiler_params=_params("arbitrary"),
        name="route",
    )(x, g_ffn, w_r, b_r, count_in)


DMA_UNROLL = 8


def _scatter_rows(d0_ref, d1_ref, x_ref, pad_out, sem, tile):
    def row_copy(r, dst_row):
        return pltpu.make_async_copy(x_ref.at[pl.ds(r, 1), :], pad_out.at[pl.ds(dst_row, 1), :], sem)

    for r in range(tile):
        row_copy(r, d0_ref[0, 0, r]).start(priority=0)
        row_copy(r, d1_ref[0, 0, r]).start(priority=1)

    def wait(r, carry):
        row_copy(0, 0).wait()
        row_copy(0, 0).wait()
        return carry

    lax.fori_loop(0, tile, wait, 0, unroll=DMA_UNROLL)


def _dispatch_kernel(zero_ref, d0p_ref, d1p_ref, d0s_ref, d1s_ref, xp_ref, xs_ref, pad_out, zero_scr, sem, zsem,
                     spare_sem, *, tile, n_tiles):
    i = pl.program_id(0)

    def zero_copy(e):
        row0 = pl.multiple_of(zero_ref[e], EXPERT_ROWS)
        fill_sem = zsem if e < N_EXPERTS else spare_sem
        return pltpu.make_async_copy(zero_scr, pad_out.at[pl.ds(row0, EXPERT_ROWS), :], fill_sem)

    def for_each_fill(lo, hi, action):
        for e in range(lo, hi):
            @pl.when(zero_ref[e] >= 0)
            def _():
                action(zero_copy(e))

    @pl.when(i == 0)
    def _():
        zero_scr[...] = jnp.zeros_like(zero_scr)
        for_each_fill(0, zero_ref.shape[0], lambda c: c.start())
        for_each_fill(0, N_EXPERTS, lambda c: c.wait())

    @pl.when(i < n_tiles)
    def _():
        _scatter_rows(d0p_ref, d1p_ref, xp_ref, pad_out, sem, tile)

    @pl.when(i == n_tiles)
    def _():
        _scatter_rows(d0s_ref, d1s_ref, xs_ref, pad_out, sem, xs_ref.shape[0])
        for_each_fill(N_EXPERTS, zero_ref.shape[0], lambda c: c.wait())


def _dispatch(zero_rows, dest_p, dest_s, xp, xs, n_rows, tile):
    n_tiles = xp.shape[0] // tile
    n_s = xs.shape[0]
    last = n_tiles - 1
    tile_spec = pl.BlockSpec((1, 1, tile), lambda i: (jnp.minimum(i, last), 0, 0), memory_space=pltpu.SMEM)
    return pl.pallas_call(
        functools.partial(_dispatch_kernel, tile=tile, n_tiles=n_tiles),
        grid=(n_tiles + 1,),
        in_specs=[
            pl.BlockSpec(memory_space=pltpu.SMEM),
            tile_spec, tile_spec,
            pl.BlockSpec(memory_space=pltpu.SMEM),
            pl.BlockSpec(memory_space=pltpu.SMEM),
            pl.BlockSpec((tile, D_MODEL), lambda i: (jnp.minimum(i, last), 0)),
            pl.BlockSpec((n_s, D_MODEL), lambda i: (0, 0)),
        ],
        out_specs=pl.BlockSpec(memory_space=pl.ANY),
        out_shape=jax.ShapeDtypeStruct((n_rows, D_MODEL), F32),
        scratch_shapes=[pltpu.VMEM((EXPERT_ROWS, D_MODEL), F32), pltpu.SemaphoreType.DMA(()),
                        pltpu.SemaphoreType.DMA(()), pltpu.SemaphoreType.DMA(())],
        compiler_params=_params("arbitrary"),
        name="dispatch",
    )(zero_rows, dest_p[0].reshape(n_tiles, 1, tile), dest_p[1].reshape(n_tiles, 1, tile),
      dest_s[0].reshape(1, 1, n_s), dest_s[1].reshape(1, 1, n_s), xp, xs)


def _expert_kernel(be_ref, nused_ref, x_ref, g_ref, wg_ref, wu_ref, wd_ref, y_ref, wg_s, wu_s, wd_s):
    i = pl.program_id(0)
    prev = be_ref[jnp.maximum(i - 1, 0)]
    active = i < nused_ref[0]

    @pl.when(jnp.logical_and(active, jnp.logical_or(i == 0, be_ref[i] != prev)))
    def _():
        wg_s[...] = wg_ref[...].astype(BF16)
        wu_s[...] = wu_ref[...].astype(BF16)
        wd_s[...] = wd_ref[...].astype(BF16)

    @pl.when(active)
    def _():
        hn = _rms(x_ref[...], g_ref[...]).astype(BF16)
        gate = jnp.dot(hn, wg_s[...], preferred_element_type=F32)
        up = jnp.dot(hn, wu_s[...], preferred_element_type=F32)
        act = (gate * jax.nn.sigmoid(gate)) * up
        y_ref[...] = jnp.dot(act.astype(BF16), wd_s[...], preferred_element_type=F32)


def _experts(block_expert, n_used, x_pad, g_ffn, w_gate, w_up, w_down):
    n_rows = x_pad.shape[0]
    nb = n_rows // EXPERT_ROWS
    used_block = lambda i, be, nu: (jnp.minimum(i, nu[0] - 1), 0)
    grid_spec = pltpu.PrefetchScalarGridSpec(
        num_scalar_prefetch=2,
        grid=(nb,),
        in_specs=[
            pl.BlockSpec((EXPERT_ROWS, D_MODEL), used_block),
            pl.BlockSpec((1, D_MODEL), lambda i, be, nu: (0, 0)),
            pl.BlockSpec((None, None, D_MODEL, D_EXPERT), lambda i, be, nu: (0, be[i], 0, 0)),
            pl.BlockSpec((None, None, D_MODEL, D_EXPERT), lambda i, be, nu: (0, be[i], 0, 0)),
            pl.BlockSpec((None, None, D_EXPERT, D_MODEL), lambda i, be, nu: (0, be[i], 0, 0)),
        ],
        out_specs=pl.BlockSpec((EXPERT_ROWS, D_MODEL), used_block),
        scratch_shapes=[
            pltpu.VMEM((D_MODEL, D_EXPERT), BF16),
            pltpu.VMEM((D_MODEL, D_EXPERT), BF16),
            pltpu.VMEM((D_EXPERT, D_MODEL), BF16),
        ],
    )
    return pl.pallas_call(
        _expert_kernel,
        grid_spec=grid_spec,
        out_shape=jax.ShapeDtypeStruct((n_rows, D_MODEL), F32),
        input_output_aliases={2: 0},
        compiler_params=_params("arbitrary"),
        name="experts",
    )(block_expert, n_used, x_pad, g_ffn, w_gate, w_up, w_down)


def _final_kernel(d0_ref, d1_ref, d0n_ref, d1n_ref, x_ref, info_ref, p_ref, y_hbm, wpg_ref, wpp_ref, gple_ref,
                  gfin_ref, out_ref, ybuf, sem):
    tm = x_ref.shape[0]
    i = pl.program_id(0)
    buf = i % 2

    def row_copy(src_row, b, k, r):
        return pltpu.make_async_copy(y_hbm.at[pl.ds(src_row, 1), :], ybuf.at[b, k, pl.ds(r, 1), :], sem.at[b])

    def gather(da_ref, db_ref, b):
        def start(r, carry):
            row_copy(da_ref[0, 0, r], b, 0, r).start(priority=0)
            row_copy(db_ref[0, 0, r], b, 1, r).start(priority=1)
            return carry
        lax.fori_loop(0, tm, start, 0, unroll=DMA_UNROLL)

    def wait_all(b):
        def wait(r, carry):
            row_copy(0, b, 0, 0).wait()
            row_copy(0, b, 1, 0).wait()
            return carry
        lax.fori_loop(0, tm, wait, 0, unroll=DMA_UNROLL)

    @pl.when(i == 0)
    def _():
        gather(d0_ref, d1_ref, 0)

    wait_all(buf)
    for r in range(tm):
        row_copy(d0n_ref[0, 0, r], 1 - buf, 0, r).start()
        row_copy(d1n_ref[0, 0, r], 1 - buf, 1, r).start()

    info = info_ref[...]
    x2 = x_ref[...] + (ybuf[buf, 0] * info[:, 2:3] + ybuf[buf, 1] * info[:, 3:4])
    gate = jax.nn.sigmoid(jnp.dot(_rms(x2, gple_ref[...]).astype(BF16), wpg_ref[...], preferred_element_type=F32))
    proj = jnp.dot(p_ref[...].astype(BF16), wpp_ref[...], preferred_element_type=F32)
    out_ref[...] = _rms(x2 + gate * proj, gfin_ref[...])

    @pl.when(i == pl.num_programs(0) - 1)
    def _():
        wait_all(1 - buf)


def _final(dest, x1, info, p, y_pad, w_pg, w_pp, g_ple, g_final, tm):
    n = x1.shape[0]
    d0 = dest[0].reshape(n // tm, 1, tm)
    d1 = dest[1].reshape(n // tm, 1, tm)
    const = lambda i: (0, 0)
    last = n // tm - 1
    cur = pl.BlockSpec((1, 1, tm), lambda i: (i, 0, 0), memory_space=pltpu.SMEM)
    nxt = pl.BlockSpec((1, 1, tm), lambda i: (jnp.minimum(i + 1, last), 0, 0), memory_space=pltpu.SMEM)
    return pl.pallas_call(
        _final_kernel,
        grid=(n // tm,),
        in_specs=[
            cur, cur, nxt, nxt,
            pl.BlockSpec((tm, D_MODEL), lambda i: (i, 0)),
            pl.BlockSpec((tm, LANES), lambda i: (i, 0)),
            pl.BlockSpec((tm, PLE_DIM), lambda i: (i, 0)),
            pl.BlockSpec(memory_space=pl.ANY),
            pl.BlockSpec((D_MODEL, D_MODEL), const),
            pl.BlockSpec((PLE_DIM, D_MODEL), const),
            pl.BlockSpec((1, D_MODEL), const),
            pl.BlockSpec((1, D_MODEL), const),
        ],
        out_specs=pl.BlockSpec((tm, D_MODEL), lambda i: (i, 0)),
        out_shape=jax.ShapeDtypeStruct((n, D_MODEL), F32),
        scratch_shapes=[pltpu.VMEM((2, 2, tm, D_MODEL), F32), pltpu.SemaphoreType.DMA((2,))],
        compiler_params=_params("arbitrary"),
        name="final",
    )(d0, d1, d0, d1, x1, info, p, y_pad, w_pg, w_pp, g_ple, g_final)


def _pad_lanes(a):
    return jnp.pad(a, ((0, 0), (0, LANES - a.shape[1])))


def kernel(x_prompt, x_sample, p_prompt, p_sample, state_mlstm_C, state_mlstm_n, state_mlstm_m, state_swa_k, state_swa_v, g_mix, w_in, b_igate, b_fgate, g_mh, attn_sinks, w_out, g_ffn, w_router_g, b_router_g, w_router_e, b_router_e, w_gate, w_up, w_down, g_ple, w_ple_gate, w_ple_proj, g_final):
    batch, seq, d = x_prompt.shape
    dec = x_sample.shape[0]
    n_p = batch * seq
    xp = x_prompt.reshape(n_p, d)
    xs = x_sample.reshape(dec, d)

    w_in0 = w_in[0]
    head_order = jnp.array(SWA_HEAD_ORDER, jnp.int32)
    w_aq = w_in0[:, ATT_OFF:ATT_OFF + A_Q_W].reshape(d, A_HEADS, A_HEAD_DIM)[:, head_order].reshape(d, A_Q_W)
    zf_pad = ZF_W - (ZF_GATE + LANES)
    w_cat = jnp.concatenate(
        [w_in0[:, :3 * M_W], w_aq,
         w_in0[:, 3 * M_W:GATE_OFF], w_in0[:, ATT_OFF + A_Q_W:],
         _pad_lanes(w_in0[:, GATE_OFF:ATT_OFF]), jnp.zeros((d, zf_pad), F32)],
        axis=1).astype(BF16)
    gate_bias8 = jnp.concatenate([b_igate, b_fgate], axis=1)
    gate_bias = _pad_lanes(gate_bias8)
    w_oa = w_out[0, M_W:].reshape(A_HEADS, A_HEAD_DIM, d)[head_order].reshape(A_Q_W, d)
    w_o = jnp.concatenate([w_out[0, :M_W], w_oa], axis=0).astype(BF16)
    kv_of_head = jnp.arange(A_HEADS, dtype=jnp.int32) // A_GROUP
    head_mask = (kv_of_head[:, None] == jnp.arange(A_KV_HEADS, dtype=jnp.int32)[None, :]).astype(F32)
    gap = EXPERT_LANE0 - N_GROUPS
    w_r = _pad_lanes(jnp.concatenate([w_router_g[0], jnp.zeros((d, gap), F32), w_router_e[0]], axis=1))
    b_r = _pad_lanes(jnp.concatenate([b_router_g, jnp.zeros((1, gap), F32), b_router_e], axis=1))
    w_pg = w_ple_gate[0].astype(BF16)
    w_pp = w_ple_proj[0].astype(BF16)
    sink_col = attn_sinks.reshape(A_HEADS, 1)

    zb, zf = _inproj_split(xp, g_mix, w_cat, tm=INPROJ_ROWS, tn=INPROJ_COLS)
    om_p, c_p, n_p8, m_p8 = _mlstm_prompt(zb, zf, gate_bias, g_mh, batch, seq)
    oa_p = _swa_prompt(zb, zf, attn_sinks, batch, seq)
    x1p = _outproj(xp, om_p, oa_p, w_o, tm=OUTPROJ_ROWS)

    zs = _inproj(xs, g_mix, w_in0, tm=dec, tn=512, precision=HI)
    om_s, c_s, n_s, m_s = _mlstm_sample(
        zs.reshape(dec, 1, zs.shape[1]), gate_bias8,
        state_mlstm_C[0], state_mlstm_n[0].reshape(dec, 1, M_W), state_mlstm_m[0].reshape(dec, 1, M_HEADS), g_mh)
    ks_off = ATT_OFF + A_Q_W
    vs_off = ks_off + A_KV_W
    q_heads = zs[:, ATT_OFF:ks_off].reshape(dec, A_HEADS, 1, A_HEAD_DIM)
    qx = (q_heads * head_mask[None, :, :, None]).reshape(dec, A_HEADS, A_KV_W)
    ox_s, k_s, v_s = _swa_sample(
        qx,
        zs[:, ks_off:vs_off].reshape(dec, 1, A_KV_W),
        zs[:, vs_off:].reshape(dec, 1, A_KV_W),
        state_swa_k[0].reshape(dec, WINDOW, A_KV_W), state_swa_v[0].reshape(dec, WINDOW, A_KV_W), sink_col)
    oa_s = jnp.sum(ox_s.reshape(dec, A_HEADS, A_KV_HEADS, A_HEAD_DIM) * head_mask[None, :, :, None], axis=2)
    x1s = _outproj(xs, om_s.reshape(dec, M_W), oa_s.reshape(dec, A_Q_W), w_out[0], tm=dec, precision=HI)

    info_p, rows_p, cnt_p = _route(x1p, g_ffn, w_r, b_r, jnp.zeros((N_EXPERTS, LANES), F32), tm=ROUTE_ROWS)
    info_s, rows_s, cnt = _route(x1s, g_ffn, w_r, b_r, cnt_p, tm=dec)
    counts = cnt[:, 0].astype(jnp.int32)
    padded = ((counts + EXPERT_ROWS - 1) // EXPERT_ROWS) * EXPERT_ROWS
    pad_end = jnp.cumsum(padded)
    pad_start = pad_end - padded
    n_assign = 2 * (n_p + dec)
    n_blocks = -(-n_assign // EXPERT_ROWS) + N_EXPERTS
    block_row0 = jnp.arange(n_blocks, dtype=jnp.int32) * EXPERT_ROWS
    block_expert = jnp.minimum(
        jnp.sum((pad_end[None, :] <= block_row0[:, None]).astype(jnp.int32), axis=1), N_EXPERTS - 1)
    n_used = (pad_end[-1:] // EXPERT_ROWS).astype(jnp.int32)
    spare = n_used + jnp.arange(N_EXPERTS, dtype=jnp.int32)
    zero_rows = jnp.concatenate([
        jnp.where(padded > 0, pad_end - EXPERT_ROWS, -1),
        jnp.where(spare < n_blocks, spare * EXPERT_ROWS, -1)]).astype(jnp.int32)

    def dest_of(rows):
        e = rows[0:2].astype(jnp.int32)
        hit = e[:, :, None] == jnp.arange(N_EXPERTS, dtype=jnp.int32)
        return jnp.sum(jnp.where(hit, pad_start, 0), axis=2) + rows[4:6].astype(jnp.int32)

    dest_p = dest_of(rows_p)
    dest_s = dest_of(rows_s)

    x_pad = _dispatch(zero_rows, dest_p, dest_s, x1p, x1s, n_blocks * EXPERT_ROWS, tile=DISPATCH_ROWS)
    y_pad = _experts(block_expert, n_used, x_pad, g_ffn, w_gate, w_up, w_down)
    y_p = _final(dest_p, x1p, info_p, p_prompt[0].reshape(n_p, PLE_DIM), y_pad, w_pg, w_pp, g_ple, g_final.reshape(1, d), tm=FINAL_ROWS)
    y_s = _final(dest_s, x1s, info_s, p_sample[0].reshape(dec, PLE_DIM), y_pad, w_pg, w_pp, g_ple, g_final.reshape(1, d), tm=dec)

    zf3 = zf.reshape(batch, seq, ZF_W)
    swa_k_p = zf3[:, seq - WINDOW:, ZF_K:ZF_V].reshape(1, batch, WINDOW, A_KV_HEADS, A_HEAD_DIM)
    swa_v_p = zf3[:, seq - WINDOW:, ZF_V:ZF_GATE].reshape(1, batch, WINDOW, A_KV_HEADS, A_HEAD_DIM)
    return (
        y_p.reshape(batch, seq, d),
        y_s.reshape(dec, 1, d),
        c_p[None],
        n_p8[None, :, :M_HEADS, :],
        m_p8[None, :, :M_HEADS, 0],
        swa_k_p,
        swa_v_p,
        c_s[None],
        n_s.reshape(1, dec, M_HEADS, M_DK),
        m_s[None, :, 0, :M_HEADS],
        k_s.reshape(1, dec, WINDOW, A_KV_HEADS, A_HEAD_DIM),
        v_s.reshape(1, dec, WINDOW, A_KV_HEADS, A_HEAD_DIM),
    )
```

```python
import functools

import jax
import jax.numpy as jnp
from jax import lax
from jax.experimental import pallas as pl
from jax.experimental.pallas import tpu as pltpu

F32 = jnp.float32
BF16 = jnp.bfloat16
HI = lax.Precision.HIGHEST

D_MODEL = 2048
M_HEADS = 4
M_DK = 256
M_DV = 256
A_HEADS = 16
A_KV_HEADS = 4
A_HEAD_DIM = 64
A_GROUP = A_HEADS // A_KV_HEADS
WINDOW = 128
M_W = M_HEADS * M_DK
A_Q_W = A_HEADS * A_HEAD_DIM
A_KV_W = A_KV_HEADS * A_HEAD_DIM
GATE_OFF = 4 * M_W
ATT_OFF = GATE_OFF + 2 * M_HEADS
ZB_W = 3 * M_W + A_Q_W
ZF_MO, ZF_K, ZF_V, ZF_GATE = 0, M_W, M_W + A_KV_W, M_W + 2 * A_KV_W
ZF_W = 2 * M_W
N_GROUPS = 4
EXPERTS_PER_GROUP = 8
N_EXPERTS = N_GROUPS * EXPERTS_PER_GROUP
D_EXPERT = 512
PLE_DIM = 256
EPS = 1e-6

LANES = 128
SUBLANES = 8
INPROJ_ROWS, INPROJ_COLS = 1024, 1024
OUTPROJ_ROWS = 512
ROUTE_ROWS = 512
DISPATCH_ROWS = 512
FINAL_ROWS = 256
MLSTM_CHUNK = 256
EXPERT_ROWS = 256
VMEM_LIMIT = 56 * 1024 * 1024
NT_DIMS = (((1,), (1,)), ((), ()))
TN_DIMS = (((0,), (0,)), ((), ()))


def _params(*sem):
    return pltpu.CompilerParams(dimension_semantics=sem, vmem_limit_bytes=VMEM_LIMIT)


def _rms(x, g):
    r = lax.rsqrt(jnp.mean(x * x, axis=-1, keepdims=True) + EPS)
    return (x * r) * g


def _log_sigmoid(x):
    return jnp.minimum(x, 0.0) - jnp.log1p(jnp.exp(-jnp.abs(x)))


def _inproj_kernel(x_ref, g_ref, w_ref, z_ref, hn_scr, *, precision):
    @pl.when(pl.program_id(1) == 0)
    def _():
        hn_scr[...] = _rms(x_ref[...], g_ref[...]).astype(hn_scr.dtype)

    z_ref[...] = jnp.dot(hn_scr[...], w_ref[...], precision=precision, preferred_element_type=F32)


def _inproj(x, g_mix, w, tm, tn, precision=None):
    n = x.shape[0]
    width = w.shape[1]
    return pl.pallas_call(
        functools.partial(_inproj_kernel, precision=precision),
        grid=(n // tm, pl.cdiv(width, tn)),
        in_specs=[
            pl.BlockSpec((tm, D_MODEL), lambda i, j: (i, 0)),
            pl.BlockSpec((1, D_MODEL), lambda i, j: (0, 0)),
            pl.BlockSpec((D_MODEL, tn), lambda i, j: (0, j)),
        ],
        out_specs=pl.BlockSpec((tm, tn), lambda i, j: (i, j)),
        out_shape=jax.ShapeDtypeStruct((n, width), F32),
        scratch_shapes=[pltpu.VMEM((tm, D_MODEL), w.dtype)],
        compiler_params=_params("parallel", "arbitrary"),
        name="inproj",
    )(x, g_mix, w)


def _inproj_split_kernel(x_ref, g_ref, w_ref, zb_ref, zf_ref, hn_scr, *, n_bf16_blocks):
    j = pl.program_id(1)

    @pl.when(j == 0)
    def _():
        hn_scr[...] = _rms(x_ref[...], g_ref[...]).astype(BF16)

    acc = jnp.dot(hn_scr[...], w_ref[...], preferred_element_type=F32)

    @pl.when(j < n_bf16_blocks)
    def _():
        zb_ref[...] = acc.astype(BF16)

    @pl.when(j >= n_bf16_blocks)
    def _():
        zf_ref[...] = acc


def _inproj_split(x, g_mix, w, tm, tn):
    n = x.shape[0]
    nb = ZB_W // tn
    nf = ZF_W // tn
    return pl.pallas_call(
        functools.partial(_inproj_split_kernel, n_bf16_blocks=nb),
        grid=(n // tm, nb + nf),
        in_specs=[
            pl.BlockSpec((tm, D_MODEL), lambda i, j: (i, 0)),
            pl.BlockSpec((1, D_MODEL), lambda i, j: (0, 0)),
            pl.BlockSpec((D_MODEL, tn), lambda i, j: (0, j)),
        ],
        out_specs=[
            pl.BlockSpec((tm, tn), lambda i, j: (i, jnp.minimum(j, nb - 1))),
            pl.BlockSpec((tm, tn), lambda i, j: (i, jnp.maximum(j - nb, 0))),
        ],
        out_shape=[
            jax.ShapeDtypeStruct((n, ZB_W), BF16),
            jax.ShapeDtypeStruct((n, ZF_W), F32),
        ],
        scratch_shapes=[pltpu.VMEM((tm, D_MODEL), BF16)],
        compiler_params=_params("parallel", "arbitrary"),
        name="inproj_prompt",
    )(x, g_mix, w)


def _mlstm_out(h, gmh, mo):
    hn = h * lax.rsqrt(jnp.mean(h * h, axis=-1, keepdims=True) + EPS)
    return (hn * gmh) * jax.nn.sigmoid(mo)


def _gates(g_raw, bias):
    g = g_raw + bias
    lane = lax.broadcasted_iota(jnp.int32, g.shape, g.ndim - 1)
    return jnp.where(lane < M_HEADS, g, _log_sigmoid(g))


def _mlstm_kernel(q_ref, k_ref, v_ref, mo_ref, g_ref, bias_ref, gmh_ref,
                  o_ref, c_out, n_out, m_out, c_scr, n_scr, m_scr, *, chunk, n_chunks):
    c_idx = pl.program_id(1)

    @pl.when(c_idx == 0)
    def _():
        c_scr[...] = jnp.zeros_like(c_scr)
        n_scr[...] = jnp.zeros_like(n_scr)
        m_scr[...] = jnp.zeros_like(m_scr)

    L = chunk
    row = lax.broadcasted_iota(jnp.int32, (L, L), 0)
    col = lax.broadcasted_iota(jnp.int32, (L, L), 1)
    causal = row >= col
    ltri = causal.astype(F32)
    utri = (row <= col).astype(F32)
    g_c = _gates(g_ref[...], bias_ref[...])
    f_c = jnp.dot(ltri, g_c, precision=HI, preferred_element_type=F32)
    g_r = g_c.T[0:2 * M_HEADS, :]
    f_r = jnp.dot(g_r, utri, precision=HI, preferred_element_type=F32)

    for h in range(M_HEADS):
        hs = slice(h * M_DK, (h + 1) * M_DK)
        hf = M_HEADS + h
        a_row = g_r[h:h + 1, :] - f_r[hf:hf + 1, :]
        f_col = f_c[:, hf:hf + 1]
        a_col = g_c[:, h:h + 1] - f_col
        m0 = m_scr[h:h + 1, 0:1]
        a_max = jnp.max(jnp.where(causal, a_row, -jnp.inf), axis=1, keepdims=True)
        m_t = f_col + jnp.maximum(m0, a_max)
        dec0 = jnp.exp(f_col + m0 - m_t)
        d = jnp.exp(jnp.where(causal, (f_col - m_t) + a_row, -jnp.inf))
        qb = q_ref[:, hs]
        kb = k_ref[:, hs] * (M_DK ** -0.5)
        vb = v_ref[:, hs]
        q = qb.astype(F32)
        k = kb.astype(F32)
        w = lax.dot_general(qb, kb, NT_DIMS, preferred_element_type=F32) * d
        c0 = c_scr[h]
        n0 = n_scr[h:h + 1, :]
        num = jnp.dot(w.astype(BF16), vb, preferred_element_type=F32) + dec0 * lax.dot_general(
            qb, c0.astype(BF16), NT_DIMS, preferred_element_type=F32)
        den = jnp.sum(w, axis=1, keepdims=True) + dec0 * jnp.sum(q * n0, axis=1, keepdims=True)
        hh = num / jnp.maximum(jnp.abs(den), jnp.exp(-m_t))
        m_end = m_t[L - 1:L, :]
        f_end = f_col[L - 1:L, :]
        w_end = jnp.exp(f_end + a_col - m_end)
        dec_end = jnp.exp(f_end + m0 - m_end)
        kw = k * w_end
        c_scr[h] = dec_end * c0 + lax.dot_general(vb, kw.astype(BF16), TN_DIMS, preferred_element_type=F32)
        n_scr[h:h + 1, :] = dec_end * n0 + jnp.sum(kw, axis=0, keepdims=True)
        m_scr[h:h + 1, :] = jnp.broadcast_to(m_end, (1, LANES))
        o_ref[:, hs] = _mlstm_out(hh, gmh_ref[:, hs], mo_ref[:, hs]).astype(o_ref.dtype)

    @pl.when(c_idx == n_chunks - 1)
    def _():
        c_out[0] = c_scr[...]
        n_out[0] = n_scr[...]
        m_out[0] = m_scr[...]


def _mlstm_prompt(zb, zf, gate_bias, g_mh, batch, seq):
    L = MLSTM_CHUNK
    nc = seq // L
    rows = lambda b, c: b * nc + c
    gate_blk = ZF_GATE // LANES
    return pl.pallas_call(
        functools.partial(_mlstm_kernel, chunk=L, n_chunks=nc),
        grid=(batch, nc),
        in_specs=[
            pl.BlockSpec((L, M_W), lambda b, c: (rows(b, c), 0)),
            pl.BlockSpec((L, M_W), lambda b, c: (rows(b, c), 1)),
            pl.BlockSpec((L, M_W), lambda b, c: (rows(b, c), 2)),
            pl.BlockSpec((L, M_W), lambda b, c: (rows(b, c), ZF_MO // M_W)),
            pl.BlockSpec((L, LANES), lambda b, c: (rows(b, c), gate_blk)),
            pl.BlockSpec((1, LANES), lambda b, c: (0, 0)),
            pl.BlockSpec((1, M_W), lambda b, c: (0, 0)),
        ],
        out_specs=[
            pl.BlockSpec((L, M_W), lambda b, c: (rows(b, c), 0)),
            pl.BlockSpec((1, M_HEADS, M_DV, M_DK), lambda b, c: (b, 0, 0, 0)),
            pl.BlockSpec((1, 8, M_DK), lambda b, c: (b, 0, 0)),
            pl.BlockSpec((1, 8, LANES), lambda b, c: (b, 0, 0)),
        ],
        out_shape=[
            jax.ShapeDtypeStruct((batch * seq, M_W), BF16),
            jax.ShapeDtypeStruct((batch, M_HEADS, M_DV, M_DK), F32),
            jax.ShapeDtypeStruct((batch, 8, M_DK), F32),
            jax.ShapeDtypeStruct((batch, 8, LANES), F32),
        ],
        scratch_shapes=[
            pltpu.VMEM((M_HEADS, M_DV, M_DK), F32),
            pltpu.VMEM((8, M_DK), F32),
            pltpu.VMEM((8, LANES), F32),
        ],
        compiler_params=_params("parallel", "arbitrary"),
        name="mlstm_prompt",
    )(zb, zb, zb, zf, zf, gate_bias, g_mh)


MLSTM_STEP_TOKENS = 4


def _mlstm_step_kernel(z_ref, bias_ref, c_ref, n_ref, m_ref, gmh_ref, o_ref, c_out, n_out, m_out):
    row = lax.broadcasted_iota(jnp.int32, (M_DV, M_DV), 0)
    col = lax.broadcasted_iota(jnp.int32, (M_DV, M_DV), 1)
    eye = row == col
    for t in range(z_ref.shape[0]):
        _mlstm_step_token(t, eye, z_ref, bias_ref, c_ref, n_ref, m_ref, gmh_ref, o_ref, c_out, n_out, m_out)


def _mlstm_step_token(t, eye, z_ref, bias_ref, c_ref, n_ref, m_ref, gmh_ref, o_ref, c_out, n_out, m_out):
    z = z_ref[t]
    gates = _gates(z[:, GATE_OFF:ATT_OFF], bias_ref[...])
    m_in = m_ref[t]
    lane = lax.broadcasted_iota(jnp.int32, (1, LANES), 1)
    m_new = jnp.zeros((1, LANES), F32)
    for h in range(M_HEADS):
        hs = slice(h * M_DK, (h + 1) * M_DK)
        q = z[:, h * M_DK:(h + 1) * M_DK]
        k = z[:, M_W + h * M_DK:M_W + (h + 1) * M_DK] * (M_DK ** -0.5)
        v = z[:, 2 * M_W + h * M_DV:2 * M_W + (h + 1) * M_DV]
        mo = z[:, 3 * M_W + h * M_DV:3 * M_W + (h + 1) * M_DV]
        ip = gates[:, h:h + 1]
        lf = gates[:, M_HEADS + h:M_HEADS + h + 1]
        m0 = m_in[:, h:h + 1]
        m_t = jnp.maximum(lf + m0, ip)
        dec0 = jnp.exp(lf + m0 - m_t)
        d = jnp.exp(ip - m_t)
        c0 = c_ref[t, h]
        n0 = n_ref[t, :, hs]
        w = jnp.sum(q * k, axis=1, keepdims=True) * d
        cq_col = jnp.sum(c0 * q, axis=1, keepdims=True)
        cq = jnp.sum(jnp.where(eye, cq_col, 0.0), axis=0, keepdims=True)
        num = w * v + dec0 * cq
        den = w + dec0 * jnp.sum(n0 * q, axis=1, keepdims=True)
        hh = num / jnp.maximum(jnp.abs(den), jnp.exp(-m_t))
        v_col = jnp.sum(jnp.where(eye, v, 0.0), axis=1, keepdims=True)
        c_out[t, h] = dec0 * c0 + (d * v_col) * k
        n_out[t, :, hs] = dec0 * n0 + d * k
        m_new = jnp.where(lane == h, m_t, m_new)
        o_ref[t, :, hs] = _mlstm_out(hh, gmh_ref[:, hs], mo).astype(o_ref.dtype)
    m_out[t] = m_new


def _mlstm_sample(z3, gate_bias, c0, n0, m0, g_mh):
    nb = z3.shape[0]
    nt = MLSTM_STEP_TOKENS
    tok = lambda b: (b, 0, 0)
    return pl.pallas_call(
        _mlstm_step_kernel,
        grid=(nb // nt,),
        in_specs=[
            pl.BlockSpec((nt, 1, z3.shape[2]), tok),
            pl.BlockSpec((1, 2 * M_HEADS), lambda b: (0, 0)),
            pl.BlockSpec((nt, M_HEADS, M_DV, M_DK), lambda b: (b, 0, 0, 0)),
            pl.BlockSpec((nt, 1, M_W), tok),
            pl.BlockSpec((nt, 1, M_HEADS), tok),
            pl.BlockSpec((1, M_W), lambda b: (0, 0)),
        ],
        out_specs=[
            pl.BlockSpec((nt, 1, M_W), tok),
            pl.BlockSpec((nt, M_HEADS, M_DV, M_DK), lambda b: (b, 0, 0, 0)),
            pl.BlockSpec((nt, 1, M_W), tok),
            pl.BlockSpec((nt, 1, LANES), tok),
        ],
        out_shape=[
            jax.ShapeDtypeStruct((nb, 1, M_W), F32),
            jax.ShapeDtypeStruct((nb, M_HEADS, M_DV, M_DK), F32),
            jax.ShapeDtypeStruct((nb, 1, M_W), F32),
            jax.ShapeDtypeStruct((nb, 1, LANES), F32),
        ],
        compiler_params=_params("parallel"),
        name="mlstm_sample",
    )(z3, gate_bias, c0, n0, m0, g_mh)


_EVEN_KV = [h for h in range(A_HEADS) if (h // A_GROUP) % 2 == 0]
_ODD_KV = [h for h in range(A_HEADS) if (h // A_GROUP) % 2 == 1]
SWA_HEAD_ORDER = tuple(h for pair in zip(_EVEN_KV, _ODD_KV) for h in pair)
HALF = LANES // 2
KV_SPAN = 2 * WINDOW


def _swa_kernel(sink_ref, q_ref, kc_ref, kp_ref, vc_ref, vp_ref, o_ref, qx_scr, p_scr):
    j = pl.program_id(1)
    lo_half = lax.broadcasted_iota(jnp.int32, (WINDOW, LANES), 1) < HALF
    zero_tile = jnp.zeros((WINDOW, LANES), BF16)
    for slot, head in enumerate(SWA_HEAD_ORDER):
        kv = head // A_GROUP
        tile = q_ref[:, (slot // 2) * LANES:(slot // 2 + 1) * LANES]
        keep = lo_half if slot % 2 == 0 else jnp.logical_not(lo_half)
        qm = jnp.where(keep, tile * (A_HEAD_DIM ** -0.5), jnp.zeros_like(tile))
        parts = [qm, zero_tile] if kv // 2 == 0 else [zero_tile, qm]
        qx_scr[slot * WINDOW:(slot + 1) * WINDOW, :] = jnp.concatenate(parts, axis=1)

    k2 = jnp.concatenate([kp_ref[...], kc_ref[...]], axis=0).astype(BF16)
    v2 = jnp.concatenate([vp_ref[...], vc_ref[...]], axis=0).astype(BF16)
    s_all = lax.dot_general(qx_scr[...], k2, NT_DIMS, preferred_element_type=F32)

    t = lax.broadcasted_iota(jnp.int32, (WINDOW, KV_SPAN), 0)
    kk = lax.broadcasted_iota(jnp.int32, (WINDOW, KV_SPAN), 1)
    in_prev = jnp.logical_and(jnp.logical_and(kk < WINDOW, kk > t), j > 0)
    in_cur = jnp.logical_and(kk >= WINDOW, kk - WINDOW <= t)
    allowed = jnp.logical_or(in_prev, in_cur)
    for slot, head in enumerate(SWA_HEAD_ORDER):
        rows = slice(slot * WINDOW, (slot + 1) * WINDOW)
        s = jnp.where(allowed, s_all[rows, :], -jnp.inf)
        sink = sink_ref[0, head]
        mx = jnp.maximum(jnp.max(s, axis=1, keepdims=True), sink)
        e = jnp.exp(s - mx)
        den = jnp.sum(e, axis=1, keepdims=True) + jnp.exp(sink - mx)
        p_scr[rows, :] = (e * (1.0 / den)).astype(BF16)

    o_all = jnp.dot(p_scr[...], v2, preferred_element_type=F32)
    for c in range(A_HEADS // 2):
        tiles = []
        for slot in (2 * c, 2 * c + 1):
            kv = SWA_HEAD_ORDER[slot] // A_GROUP
            tiles.append(o_all[slot * WINDOW:(slot + 1) * WINDOW, (kv // 2) * LANES:(kv // 2 + 1) * LANES])
        o_ref[:, c * LANES:(c + 1) * LANES] = jnp.where(lo_half, tiles[0], tiles[1]).astype(o_ref.dtype)


def _swa_prompt(zb, zf, sinks, batch, seq):
    nb = seq // WINDOW
    q_blk = (3 * M_W) // A_Q_W
    k_blk = ZF_K // A_KV_W
    cur = lambda b, j: b * nb + j
    prev = lambda b, j: b * nb + jnp.maximum(j - 1, 0)
    return pl.pallas_call(
        _swa_kernel,
        grid=(batch, nb),
        in_specs=[
            pl.BlockSpec(memory_space=pltpu.SMEM),
            pl.BlockSpec((WINDOW, A_Q_W), lambda b, j: (cur(b, j), q_blk)),
            pl.BlockSpec((WINDOW, A_KV_W), lambda b, j: (cur(b, j), k_blk)),
            pl.BlockSpec((WINDOW, A_KV_W), lambda b, j: (prev(b, j), k_blk)),
            pl.BlockSpec((WINDOW, A_KV_W), lambda b, j: (cur(b, j), k_blk + 1)),
            pl.BlockSpec((WINDOW, A_KV_W), lambda b, j: (prev(b, j), k_blk + 1)),
        ],
        out_specs=pl.BlockSpec((WINDOW, A_Q_W), lambda b, j: (cur(b, j), 0)),
        out_shape=jax.ShapeDtypeStruct((batch * seq, A_Q_W), BF16),
        scratch_shapes=[pltpu.VMEM((A_HEADS * WINDOW, A_KV_W), BF16), pltpu.VMEM((A_HEADS * WINDOW, KV_SPAN), BF16)],
        compiler_params=_params("parallel", "arbitrary"),
        name="swa_prompt",
    )(sinks, zb, zf, zf, zf, zf)


SWA_STEP_TOKENS = 8


def _swa_step_kernel(qx_ref, kn_ref, vn_ref, bk_ref, bv_ref, sink_ref, o_ref, ko_ref, vo_ref):
    in_window = lax.broadcasted_iota(jnp.int32, (1, WINDOW), 1) >= 1
    scale = A_HEAD_DIM ** -0.5
    sink = sink_ref[...]
    for t in range(qx_ref.shape[0]):
        qx = qx_ref[t]
        bk = bk_ref[t]
        bv = bv_ref[t]
        kn = kn_ref[t]
        vn = vn_ref[t]
        s = lax.dot_general(qx, bk, NT_DIMS, precision=HI, preferred_element_type=F32) * scale
        s = jnp.where(in_window, s, -jnp.inf)
        s_new = jnp.sum(qx * kn, axis=1, keepdims=True) * scale
        mx = jnp.maximum(jnp.maximum(jnp.max(s, axis=1, keepdims=True), s_new), sink)
        e = jnp.exp(s - mx)
        e_new = jnp.exp(s_new - mx)
        den = jnp.sum(e, axis=1, keepdims=True) + e_new + jnp.exp(sink - mx)
        o_ref[t] = jnp.dot(e / den, bv, precision=HI, preferred_element_type=F32) + (e_new / den) * vn
        ko_ref[t, 0:WINDOW - 1, :] = bk[1:WINDOW, :]
        ko_ref[t, WINDOW - 1:WINDOW, :] = kn
        vo_ref[t, 0:WINDOW - 1, :] = bv[1:WINDOW, :]
        vo_ref[t, WINDOW - 1:WINDOW, :] = vn


def _swa_sample(qx, kn3, vn3, buf_k, buf_v, sink_col):
    nb = qx.shape[0]
    nt = SWA_STEP_TOKENS
    tok = lambda b: (b, 0, 0)
    return pl.pallas_call(
        _swa_step_kernel,
        grid=(nb // nt,),
        in_specs=[
            pl.BlockSpec((nt, A_HEADS, A_KV_W), tok),
            pl.BlockSpec((nt, 1, A_KV_W), tok),
            pl.BlockSpec((nt, 1, A_KV_W), tok),
            pl.BlockSpec((nt, WINDOW, A_KV_W), tok),
            pl.BlockSpec((nt, WINDOW, A_KV_W), tok),
            pl.BlockSpec((A_HEADS, 1), lambda b: (0, 0)),
        ],
        out_specs=[
            pl.BlockSpec((nt, A_HEADS, A_KV_W), tok),
            pl.BlockSpec((nt, WINDOW, A_KV_W), tok),
            pl.BlockSpec((nt, WINDOW, A_KV_W), tok),
        ],
        out_shape=[
            jax.ShapeDtypeStruct((nb, A_HEADS, A_KV_W), F32),
            jax.ShapeDtypeStruct((nb, WINDOW, A_KV_W), F32),
            jax.ShapeDtypeStruct((nb, WINDOW, A_KV_W), F32),
        ],
        compiler_params=_params("parallel"),
        name="swa_sample",
    )(qx, kn3, vn3, buf_k, buf_v, sink_col)


GROUP_LANE0 = 0
EXPERT_LANE0 = 8


def _outproj_kernel(x_ref, om_ref, oa_ref, wm_ref, wa_ref, y_ref, *, precision):
    y_ref[...] = (x_ref[...]
                  + jnp.dot(om_ref[...], wm_ref[...], precision=precision, preferred_element_type=F32)
                  + jnp.dot(oa_ref[...], wa_ref[...], precision=precision, preferred_element_type=F32))


def _outproj(x, o_m, o_a, w, tm, precision=None):
    n = x.shape[0]
    return pl.pallas_call(
        functools.partial(_outproj_kernel, precision=precision),
        grid=(n // tm,),
        in_specs=[
            pl.BlockSpec((tm, D_MODEL), lambda i: (i, 0)),
            pl.BlockSpec((tm, M_W), lambda i: (i, 0)),
            pl.BlockSpec((tm, A_Q_W), lambda i: (i, 0)),
            pl.BlockSpec((M_W, D_MODEL), lambda i: (0, 0)),
            pl.BlockSpec((A_Q_W, D_MODEL), lambda i: (1, 0)),
        ],
        out_specs=pl.BlockSpec((tm, D_MODEL), lambda i: (i, 0)),
        out_shape=jax.ShapeDtypeStruct((n, D_MODEL), F32),
        compiler_params=_params("parallel"),
        name="outproj",
    )(x, o_m, o_a, w, w)


def _route_kernel(x_ref, g_ref, wr_ref, br_ref, cin_ref, info_ref, rows_ref, cnt_ref, carry):
    @pl.when(pl.program_id(0) == 0)
    def _():
        carry[...] = cin_ref[...]

    tm = x_ref.shape[0]
    hn = _rms(x_ref[...], g_ref[...])
    w_r = wr_ref[...]
    hn_hi = hn.astype(BF16)
    hn_lo = (hn - hn_hi.astype(F32)).astype(BF16)
    w_hi = w_r.astype(BF16)
    w_lo = (w_r - w_hi.astype(F32)).astype(BF16)
    logits = (jnp.dot(hn_hi, w_hi, preferred_element_type=F32)
              + (jnp.dot(hn_lo, w_hi, preferred_element_type=F32)
                 + jnp.dot(hn_hi, w_lo, preferred_element_type=F32))) + br_ref[...]
    lt = logits.T
    neg = -jnp.inf
    grp = lax.broadcasted_iota(jnp.int32, (SUBLANES, tm), 0)
    gl = jnp.where(grp < N_GROUPS, lt[GROUP_LANE0:GROUP_LANE0 + SUBLANES, :], neg)
    g_max = jnp.max(gl, axis=0, keepdims=True)
    g_idx = jnp.min(jnp.where(gl == g_max, grp, N_GROUPS), axis=0, keepdims=True)
    g_w = 1.0 / jnp.sum(jnp.exp(gl - g_max), axis=0, keepdims=True)
    eid = lax.broadcasted_iota(jnp.int32, (N_EXPERTS, tm), 0)
    first = g_idx * EXPERTS_PER_GROUP
    in_group = jnp.logical_and(eid >= first, eid < first + EXPERTS_PER_GROUP)
    el = jnp.where(in_group, lt[EXPERT_LANE0:EXPERT_LANE0 + N_EXPERTS, :], neg)
    l1 = jnp.max(el, axis=0, keepdims=True)
    i1 = jnp.min(jnp.where(el == l1, eid, N_EXPERTS), axis=0, keepdims=True)
    el2 = jnp.where(eid == i1, neg, el)
    l2 = jnp.max(el2, axis=0, keepdims=True)
    i2 = jnp.min(jnp.where(el2 == l2, eid, N_EXPERTS), axis=0, keepdims=True)
    e21 = jnp.exp(l2 - l1)
    w1 = g_w * (1.0 / (1.0 + e21))
    w2 = g_w * (e21 / (1.0 + e21))
    hit1 = eid == i1
    hit2 = eid == i2
    onehot = jnp.logical_or(hit1, hit2).astype(F32)
    r = lax.broadcasted_iota(jnp.int32, (tm, tm), 0)
    c = lax.broadcasted_iota(jnp.int32, (tm, tm), 1)
    before = (r < c).astype(BF16)
    cum = jnp.dot(onehot.astype(BF16), before, preferred_element_type=F32) + carry[:, 0:1]
    rank1 = jnp.sum(jnp.where(hit1, cum, 0.0), axis=0, keepdims=True)
    rank2 = jnp.sum(jnp.where(hit2, cum, 0.0), axis=0, keepdims=True)
    row = lax.broadcasted_iota(jnp.int32, (LANES, tm), 0)
    rows = jnp.where(row == 0, i1.astype(F32), 0.0)
    rows = jnp.where(row == 1, i2.astype(F32), rows)
    rows = jnp.where(row == 2, w1, rows)
    rows = jnp.where(row == 3, w2, rows)
    rows = jnp.where(row == 4, rank1, rows)
    rows = jnp.where(row == 5, rank2, rows)
    rows_ref[...] = rows[0:8, :]
    info_ref[...] = rows.T
    total = carry[:, 0:1] + jnp.sum(onehot, axis=1, keepdims=True)
    carry[...] = jnp.broadcast_to(total, carry.shape)
    cnt_ref[...] = jnp.broadcast_to(total, cnt_ref.shape)


def _route(x, g_ffn, w_r, b_r, count_in, tm):
    n = x.shape[0]
    const = lambda i: (0, 0)
    return pl.pallas_call(
        _route_kernel,
        grid=(n // tm,),
        in_specs=[
            pl.BlockSpec((tm, D_MODEL), lambda i: (i, 0)),
            pl.BlockSpec((1, D_MODEL), const),
            pl.BlockSpec((D_MODEL, LANES), const),
            pl.BlockSpec((1, LANES), const),
            pl.BlockSpec((N_EXPERTS, LANES), const),
        ],
        out_specs=[
            pl.BlockSpec((tm, LANES), lambda i: (i, 0)),
            pl.BlockSpec((8, tm), lambda i: (0, i)),
            pl.BlockSpec((N_EXPERTS, LANES), const),
        ],
        out_shape=[
            jax.ShapeDtypeStruct((n, LANES), F32),
            jax.ShapeDtypeStruct((8, n), F32),
            jax.ShapeDtypeStruct((N_EXPERTS, LANES), F32),
        ],
        scratch_shapes=[pltpu.VMEM((N_EXPERTS, LANES), F32)],
        compiler_params=_params("arbitrary"),
        name="route",
    )(x, g_ffn, w_r, b_r, count_in)


DMA_UNROLL = 8


def _scatter_rows(d0_ref, d1_ref, x_ref, pad_out, sem, tile):
    def row_copy(r, dst_row):
        return pltpu.make_async_copy(x_ref.at[pl.ds(r, 1), :], pad_out.at[pl.ds(dst_row, 1), :], sem)

    for r in range(tile):
        row_copy(r, d0_ref[0, 0, r]).start(priority=0)
        row_copy(r, d1_ref[0, 0, r]).start(priority=1)

    def wait(r, carry):
        row_copy(0, 0).wait()
        row_copy(0, 0).wait()
        return carry

    lax.fori_loop(0, tile, wait, 0, unroll=DMA_UNROLL)


def _dispatch_kernel(zero_ref, d0p_ref, d1p_ref, d0s_ref, d1s_ref, xp_ref, xs_ref, pad_out, zero_scr, sem, zsem,
                     *, tile, n_tiles):
    i = pl.program_id(0)

    @pl.when(i == 0)
    def _():
        zero_scr[...] = jnp.zeros_like(zero_scr)

        def zero_copy(e):
            row0 = pl.multiple_of(zero_ref[e], EXPERT_ROWS)
            return pltpu.make_async_copy(zero_scr, pad_out.at[pl.ds(row0, EXPERT_ROWS), :], zsem)

        for e in range(zero_ref.shape[0]):
            @pl.when(zero_ref[e] >= 0)
            def _():
                zero_copy(e).start()
        for e in range(zero_ref.shape[0]):
            @pl.when(zero_ref[e] >= 0)
            def _():
                zero_copy(e).wait()

    @pl.when(i < n_tiles)
    def _():
        _scatter_rows(d0p_ref, d1p_ref, xp_ref, pad_out, sem, tile)

    @pl.when(i == n_tiles)
    def _():
        _scatter_rows(d0s_ref, d1s_ref, xs_ref, pad_out, sem, xs_ref.shape[0])


def _dispatch(zero_rows, dest_p, dest_s, xp, xs, n_rows, tile):
    n_tiles = xp.shape[0] // tile
    n_s = xs.shape[0]
    last = n_tiles - 1
    tile_spec = pl.BlockSpec((1, 1, tile), lambda i: (jnp.minimum(i, last), 0, 0), memory_space=pltpu.SMEM)
    return pl.pallas_call(
        functools.partial(_dispatch_kernel, tile=tile, n_tiles=n_tiles),
        grid=(n_tiles + 1,),
        in_specs=[
            pl.BlockSpec(memory_space=pltpu.SMEM),
            tile_spec, tile_spec,
            pl.BlockSpec(memory_space=pltpu.SMEM),
            pl.BlockSpec(memory_space=pltpu.SMEM),
            pl.BlockSpec((tile, D_MODEL), lambda i: (jnp.minimum(i, last), 0)),
            pl.BlockSpec((n_s, D_MODEL), lambda i: (0, 0)),
        ],
        out_specs=pl.BlockSpec(memory_space=pl.ANY),
        out_shape=jax.ShapeDtypeStruct((n_rows, D_MODEL), F32),
        scratch_shapes=[pltpu.VMEM((EXPERT_ROWS, D_MODEL), F32), pltpu.SemaphoreType.DMA(()),
                        pltpu.SemaphoreType.DMA(())],
        compiler_params=_params("arbitrary"),
        name="dispatch",
    )(zero_rows, dest_p[0].reshape(n_tiles, 1, tile), dest_p[1].reshape(n_tiles, 1, tile),
      dest_s[0].reshape(1, 1, n_s), dest_s[1].reshape(1, 1, n_s), xp, xs)


def _expert_kernel(be_ref, nused_ref, first_ref, next_ref, slot_ref, x_ref, g_ref, wg_hbm, wu_hbm, wd_hbm, y_ref,
                   wg_s, wu_s, wd_s, wg_f, wu_f, wd_f, sem):
    i = pl.program_id(0)
    active = i < nused_ref[0]

    def fetch(e, s):
        return (pltpu.make_async_copy(wg_hbm.at[0, e], wg_f.at[s], sem.at[s]),
                pltpu.make_async_copy(wu_hbm.at[0, e], wu_f.at[s], sem.at[s]),
                pltpu.make_async_copy(wd_hbm.at[0, e], wd_f.at[s], sem.at[s]))

    @pl.when(i == 0)
    def _():
        for c in fetch(be_ref[0], 0):
            c.start()

    @pl.when(jnp.logical_and(active, first_ref[i] == 1))
    def _():
        s = slot_ref[i]
        for c in fetch(be_ref[i], s):
            c.wait()
        wg_s[...] = wg_f[s].astype(BF16)
        wu_s[...] = wu_f[s].astype(BF16)
        wd_s[...] = wd_f[s].astype(BF16)

        @pl.when(next_ref[i] >= 0)
        def _():
            for c in fetch(next_ref[i], 1 - s):
                c.start()

    @pl.when(active)
    def _():
        hn = _rms(x_ref[...], g_ref[...]).astype(BF16)
        gate = jnp.dot(hn, wg_s[...], preferred_element_type=F32)
        up = jnp.dot(hn, wu_s[...], preferred_element_type=F32)
        act = (gate * jax.nn.sigmoid(gate)) * up
        y_ref[...] = jnp.dot(act.astype(BF16), wd_s[...], preferred_element_type=F32)


def _experts(block_expert, n_used, x_pad, g_ffn, w_gate, w_up, w_down):
    n_rows = x_pad.shape[0]
    nb = n_rows // EXPERT_ROWS
    blk = jnp.arange(nb, dtype=jnp.int32)
    used = blk < n_used[0]
    prev_e = jnp.concatenate([jnp.full((1,), -1, jnp.int32), block_expert[:-1]])
    first = jnp.logical_and(used, block_expert != prev_e)
    slot = ((jnp.cumsum(first.astype(jnp.int32)) - 1) % 2).astype(jnp.int32)
    later = jnp.logical_and(used[None, :], block_expert[None, :] > block_expert[:, None])
    nxt = jnp.min(jnp.where(later, block_expert[None, :], N_EXPERTS), axis=1)
    nxt = jnp.where(nxt == N_EXPERTS, -1, nxt).astype(jnp.int32)
    used_block = lambda i, be, nu, *_: (jnp.minimum(i, nu[0] - 1), 0)
    grid_spec = pltpu.PrefetchScalarGridSpec(
        num_scalar_prefetch=5,
        grid=(nb,),
        in_specs=[
            pl.BlockSpec((EXPERT_ROWS, D_MODEL), used_block),
            pl.BlockSpec((1, D_MODEL), lambda i, *_: (0, 0)),
            pl.BlockSpec(memory_space=pl.ANY),
            pl.BlockSpec(memory_space=pl.ANY),
            pl.BlockSpec(memory_space=pl.ANY),
        ],
        out_specs=pl.BlockSpec((EXPERT_ROWS, D_MODEL), used_block),
        scratch_shapes=[
            pltpu.VMEM((D_MODEL, D_EXPERT), BF16),
            pltpu.VMEM((D_MODEL, D_EXPERT), BF16),
            pltpu.VMEM((D_EXPERT, D_MODEL), BF16),
            pltpu.VMEM((2, D_MODEL, D_EXPERT), F32),
            pltpu.VMEM((2, D_MODEL, D_EXPERT), F32),
            pltpu.VMEM((2, D_EXPERT, D_MODEL), F32),
            pltpu.SemaphoreType.DMA((2,)),
        ],
    )
    return pl.pallas_call(
        _expert_kernel,
        grid_spec=grid_spec,
        out_shape=jax.ShapeDtypeStruct((n_rows, D_MODEL), F32),
        input_output_aliases={5: 0},
        compiler_params=_params("arbitrary"),
        name="experts",
    )(block_expert, n_used, first.astype(jnp.int32), nxt, slot, x_pad, g_ffn, w_gate, w_up, w_down)


def _final_kernel(d0_ref, d1_ref, d0n_ref, d1n_ref, x_ref, info_ref, p_ref, y_hbm, wpg_ref, wpp_ref, gple_ref,
                  gfin_ref, out_ref, ybuf, sem):
    tm = x_ref.shape[0]
    i = pl.program_id(0)
    buf = i % 2

    def row_copy(src_row, b, k, r):
        return pltpu.make_async_copy(y_hbm.at[pl.ds(src_row, 1), :], ybuf.at[b, k, pl.ds(r, 1), :], sem.at[b])

    def gather(da_ref, db_ref, b):
        def start(r, carry):
            row_copy(da_ref[0, 0, r], b, 0, r).start(priority=0)
            row_copy(db_ref[0, 0, r], b, 1, r).start(priority=1)
            return carry
        lax.fori_loop(0, tm, start, 0, unroll=DMA_UNROLL)

    def wait_all(b):
        def wait(r, carry):
            row_copy(0, b, 0, 0).wait()
            row_copy(0, b, 1, 0).wait()
            return carry
        lax.fori_loop(0, tm, wait, 0, unroll=DMA_UNROLL)

    @pl.when(i == 0)
    def _():
        gather(d0_ref, d1_ref, 0)

    wait_all(buf)
    for r in range(tm):
        row_copy(d0n_ref[0, 0, r], 1 - buf, 0, r).start()
        row_copy(d1n_ref[0, 0, r], 1 - buf, 1, r).start()

    info = info_ref[...]
    x2 = x_ref[...] + (ybuf[buf, 0] * info[:, 2:3] + ybuf[buf, 1] * info[:, 3:4])
    gate = jax.nn.sigmoid(jnp.dot(_rms(x2, gple_ref[...]).astype(BF16), wpg_ref[...], preferred_element_type=F32))
    proj = jnp.dot(p_ref[...].astype(BF16), wpp_ref[...], preferred_element_type=F32)
    out_ref[...] = _rms(x2 + gate * proj, gfin_ref[...])

    @pl.when(i == pl.num_programs(0) - 1)
    def _():
        wait_all(1 - buf)


def _final(dest, x1, info, p, y_pad, w_pg, w_pp, g_ple, g_final, tm):
    n = x1.shape[0]
    d0 = dest[0].reshape(n // tm, 1, tm)
    d1 = dest[1].reshape(n // tm, 1, tm)
    const = lambda i: (0, 0)
    last = n // tm - 1
    cur = pl.BlockSpec((1, 1, tm), lambda i: (i, 0, 0), memory_space=pltpu.SMEM)
    nxt = pl.BlockSpec((1, 1, tm), lambda i: (jnp.minimum(i + 1, last), 0, 0), memory_space=pltpu.SMEM)
    return pl.pallas_call(
        _final_kernel,
        grid=(n // tm,),
        in_specs=[
            cur, cur, nxt, nxt,
            pl.BlockSpec((tm, D_MODEL), lambda i: (i, 0)),
            pl.BlockSpec((tm, LANES), lambda i: (i, 0)),
            pl.BlockSpec((tm, PLE_DIM), lambda i: (i, 0)),
            pl.BlockSpec(memory_space=pl.ANY),
            pl.BlockSpec((D_MODEL, D_MODEL), const),
            pl.BlockSpec((PLE_DIM, D_MODEL), const),
            pl.BlockSpec((1, D_MODEL), const),
            pl.BlockSpec((1, D_MODEL), const),
        ],
        out_specs=pl.BlockSpec((tm, D_MODEL), lambda i: (i, 0)),
        out_shape=jax.ShapeDtypeStruct((n, D_MODEL), F32),
        scratch_shapes=[pltpu.VMEM((2, 2, tm, D_MODEL), F32), pltpu.SemaphoreType.DMA((2,))],
        compiler_params=_params("arbitrary"),
        name="final",
    )(d0, d1, d0, d1, x1, info, p, y_pad, w_pg, w_pp, g_ple, g_final)


def _pad_lanes(a):
    return jnp.pad(a, ((0, 0), (0, LANES - a.shape[1])))


def kernel(x_prompt, x_sample, p_prompt, p_sample, state_mlstm_C, state_mlstm_n, state_mlstm_m, state_swa_k, state_swa_v, g_mix, w_in, b_igate, b_fgate, g_mh, attn_sinks, w_out, g_ffn, w_router_g, b_router_g, w_router_e, b_router_e, w_gate, w_up, w_down, g_ple, w_ple_gate, w_ple_proj, g_final):
    batch, seq, d = x_prompt.shape
    dec = x_sample.shape[0]
    n_p = batch * seq
    xp = x_prompt.reshape(n_p, d)
    xs = x_sample.reshape(dec, d)

    w_in0 = w_in[0]
    head_order = jnp.array(SWA_HEAD_ORDER, jnp.int32)
    w_aq = w_in0[:, ATT_OFF:ATT_OFF + A_Q_W].reshape(d, A_HEADS, A_HEAD_DIM)[:, head_order].reshape(d, A_Q_W)
    zf_pad = ZF_W - (ZF_GATE + LANES)
    w_cat = jnp.concatenate(
        [w_in0[:, :3 * M_W], w_aq,
         w_in0[:, 3 * M_W:GATE_OFF], w_in0[:, ATT_OFF + A_Q_W:],
         _pad_lanes(w_in0[:, GATE_OFF:ATT_OFF]), jnp.zeros((d, zf_pad), F32)],
        axis=1).astype(BF16)
    gate_bias8 = jnp.concatenate([b_igate, b_fgate], axis=1)
    gate_bias = _pad_lanes(gate_bias8)
    w_oa = w_out[0, M_W:].reshape(A_HEADS, A_HEAD_DIM, d)[head_order].reshape(A_Q_W, d)
    w_o = jnp.concatenate([w_out[0, :M_W], w_oa], axis=0).astype(BF16)
    kv_of_head = jnp.arange(A_HEADS, dtype=jnp.int32) // A_GROUP
    head_mask = (kv_of_head[:, None] == jnp.arange(A_KV_HEADS, dtype=jnp.int32)[None, :]).astype(F32)
    gap = EXPERT_LANE0 - N_GROUPS
    w_r = _pad_lanes(jnp.concatenate([w_router_g[0], jnp.zeros((d, gap), F32), w_router_e[0]], axis=1))
    b_r = _pad_lanes(jnp.concatenate([b_router_g, jnp.zeros((1, gap), F32), b_router_e], axis=1))
    w_pg = w_ple_gate[0].astype(BF16)
    w_pp = w_ple_proj[0].astype(BF16)
    sink_col = attn_sinks.reshape(A_HEADS, 1)

    zb, zf = _inproj_split(xp, g_mix, w_cat, tm=INPROJ_ROWS, tn=INPROJ_COLS)
    om_p, c_p, n_p8, m_p8 = _mlstm_prompt(zb, zf, gate_bias, g_mh, batch, seq)
    oa_p = _swa_prompt(zb, zf, attn_sinks, batch, seq)
    x1p = _outproj(xp, om_p, oa_p, w_o, tm=OUTPROJ_ROWS)

    zs = _inproj(xs, g_mix, w_in0, tm=dec, tn=512, precision=HI)
    om_s, c_s, n_s, m_s = _mlstm_sample(
        zs.reshape(dec, 1, zs.shape[1]), gate_bias8,
        state_mlstm_C[0], state_mlstm_n[0].reshape(dec, 1, M_W), state_mlstm_m[0].reshape(dec, 1, M_HEADS), g_mh)
    ks_off = ATT_OFF + A_Q_W
    vs_off = ks_off + A_KV_W
    q_heads = zs[:, ATT_OFF:ks_off].reshape(dec, A_HEADS, 1, A_HEAD_DIM)
    qx = (q_heads * head_mask[None, :, :, None]).reshape(dec, A_HEADS, A_KV_W)
    ox_s, k_s, v_s = _swa_sample(
        qx,
        zs[:, ks_off:vs_off].reshape(dec, 1, A_KV_W),
        zs[:, vs_off:].reshape(dec, 1, A_KV_W),
        state_swa_k[0].reshape(dec, WINDOW, A_KV_W), state_swa_v[0].reshape(dec, WINDOW, A_KV_W), sink_col)
    oa_s = jnp.sum(ox_s.reshape(dec, A_HEADS, A_KV_HEADS, A_HEAD_DIM) * head_mask[None, :, :, None], axis=2)
    x1s = _outproj(xs, om_s.reshape(dec, M_W), oa_s.reshape(dec, A_Q_W), w_out[0], tm=dec, precision=HI)

    info_p, rows_p, cnt_p = _route(x1p, g_ffn, w_r, b_r, jnp.zeros((N_EXPERTS, LANES), F32), tm=ROUTE_ROWS)
    info_s, rows_s, cnt = _route(x1s, g_ffn, w_r, b_r, cnt_p, tm=dec)
    counts = cnt[:, 0].astype(jnp.int32)
    padded = ((counts + EXPERT_ROWS - 1) // EXPERT_ROWS) * EXPERT_ROWS
    pad_end = jnp.cumsum(padded)
    pad_start = pad_end - padded
    n_assign = 2 * (n_p + dec)
    n_blocks = -(-n_assign // EXPERT_ROWS) + N_EXPERTS
    block_row0 = jnp.arange(n_blocks, dtype=jnp.int32) * EXPERT_ROWS
    block_expert = jnp.minimum(
        jnp.sum((pad_end[None, :] <= block_row0[:, None]).astype(jnp.int32), axis=1), N_EXPERTS - 1)
    n_used = (pad_end[-1:] // EXPERT_ROWS).astype(jnp.int32)
    spare = n_used + jnp.arange(N_EXPERTS, dtype=jnp.int32)
    zero_rows = jnp.concatenate([
        jnp.where(padded > 0, pad_end - EXPERT_ROWS, -1),
        jnp.where(spare < n_blocks, spare * EXPERT_ROWS, -1)]).astype(jnp.int32)

    def dest_of(rows):
        e = rows[0:2].astype(jnp.int32)
        hit = e[:, :, None] == jnp.arange(N_EXPERTS, dtype=jnp.int32)
        return jnp.sum(jnp.where(hit, pad_start, 0), axis=2) + rows[4:6].astype(jnp.int32)

    dest_p = dest_of(rows_p)
    dest_s = dest_of(rows_s)

    x_pad = _dispatch(zero_rows, dest_p, dest_s, x1p, x1s, n_blocks * EXPERT_ROWS, tile=DISPATCH_ROWS)
    y_pad = _experts(block_expert, n_used, x_pad, g_ffn, w_gate, w_up, w_down)
    y_p = _final(dest_p, x1p, info_p, p_prompt[0].reshape(n_p, PLE_DIM), y_pad, w_pg, w_pp, g_ple, g_final.reshape(1, d), tm=FINAL_ROWS)
    y_s = _final(dest_s, x1s, info_s, p_sample[0].reshape(dec, PLE_DIM), y_pad, w_pg, w_pp, g_ple, g_final.reshape(1, d), tm=dec)

    zf3 = zf.reshape(batch, seq, ZF_W)
    swa_k_p = zf3[:, seq - WINDOW:, ZF_K:ZF_V].reshape(1, batch, WINDOW, A_KV_HEADS, A_HEAD_DIM)
    swa_v_p = zf3[:, seq - WINDOW:, ZF_V:ZF_GATE].reshape(1, batch, WINDOW, A_KV_HEADS, A_HEAD_DIM)
    return (
        y_p.reshape(batch, seq, d),
        y_s.reshape(dec, 1, d),
        c_p[None],
        n_p8[None, :, :M_HEADS, :],
        m_p8[None, :, :M_HEADS, 0],
        swa_k_p,
        swa_v_p,
        c_s[None],
        n_s.reshape(1, dec, M_HEADS, M_DK),
        m_s[None, :, 0, :M_HEADS],
        k_s.reshape(1, dec, WINDOW, A_KV_HEADS, A_HEAD_DIM),
        v_s.reshape(1, dec, WINDOW, A_KV_HEADS, A_HEAD_DIM),
    )
```
